```python
import math
import jax
import jax.numpy as jnp
from jax import lax
import numpy as np


D_MODEL = 1024
BATCH = 16
SEQ = 2048
DEPTH = 1

MIX_WIDTH = D_MODEL
MLSTM_WIDTH = MIX_WIDTH // 2
ATTN_WIDTH = MIX_WIDTH - MLSTM_WIDTH
MLSTM_HEADS = 4
MLSTM_HEAD_DIM = MLSTM_WIDTH // MLSTM_HEADS
MLSTM_QKV_BLOCK = 4
MLSTM_CONV = 5
MLSTM_CHUNK = 64
ATTN_HEAD_DIM = 64
ATTN_HEADS = ATTN_WIDTH // ATTN_HEAD_DIM
DILATED_PATTERNS = ((128, 1), (512, 4), (2048, 16))
ATTN_BLOCK = 64
REL_BUCKETS = 32
REL_MAX_DIST = 1024
N_EXPERTS = 16
EC_CAPACITY_FACTOR = 2
D_FF_EXPERT = 2 * D_MODEL
NORM_EPS = 1e-6
N_MOD = 6
PROJ_WIDTHS = (MLSTM_WIDTH, MLSTM_WIDTH, MLSTM_HEADS, MLSTM_HEADS, MLSTM_HEADS, MLSTM_HEADS,
               ATTN_WIDTH, ATTN_WIDTH, ATTN_WIDTH)
PROJ_TOTAL = sum(PROJ_WIDTHS)

kernel_name = "hybrid_mlstm_dilated_attn_ec_moe"


def rms_norm(x, g):
    xf = x.astype(jnp.float32)
    y = xf * lax.rsqrt(jnp.mean(xf * xf, axis=-1, keepdims=True) + NORM_EPS)
    return (y * g.astype(jnp.float32)).astype(x.dtype)


def t5_bucket(rel):
    half = REL_BUCKETS // 2
    exact = half // 2
    n = jnp.abs(rel)
    log_ratio = jnp.log(jnp.maximum(n, 1).astype(jnp.float32) / exact) / math.log(REL_MAX_DIST / exact)
    large = jnp.minimum(exact + (log_ratio * (half - exact)).astype(jnp.int32), half - 1)
    return jnp.where(rel > 0, half, 0) + jnp.where(n < exact, n, large)


def dilated_window_attention(q, k, v, rel_bias, window, dilation):
    B, S, H, E = q.shape
    half = window // (2 * dilation)
    L = S // dilation
    nb = -(-L // ATTN_BLOCK)
    pad = nb * ATTN_BLOCK - L

    def phases(t):
        return t.reshape(B, L, dilation, H, E).transpose(0, 2, 3, 1, 4).reshape(B * dilation, H, L, E)

    qb = jnp.pad(phases(q), ((0, 0), (0, 0), (0, pad), (0, 0))).reshape(B * dilation, H, nb, ATTN_BLOCK, E)

    def key_blocks(t):
        tp = jnp.pad(phases(t), ((0, 0), (0, 0), (ATTN_BLOCK, pad + ATTN_BLOCK), (0, 0)))
        tp = tp.reshape(B * dilation, H, nb + 2, ATTN_BLOCK, E)
        return jnp.concatenate([tp[:, :, 0:nb], tp[:, :, 1:nb + 1], tp[:, :, 2:nb + 2]], axis=3)

    kb, vb = key_blocks(k), key_blocks(v)
    qi = jnp.arange(ATTN_BLOCK)[:, None]
    kj = jnp.arange(3 * ATTN_BLOCK)[None, :] - ATTN_BLOCK
    rel = kj - qi
    key_pos = jnp.arange(nb)[:, None] * ATTN_BLOCK + kj
    valid = (jnp.abs(rel) <= half)[None] & ((key_pos >= 0) & (key_pos < L))[:, None, :]
    bias = rel_bias[t5_bucket(rel * dilation)].transpose(2, 0, 1).astype(jnp.float32)
    logits = jnp.einsum('nhcqe,nhcke->nhcqk', qb, kb).astype(jnp.float32) / math.sqrt(E) + bias[:, None]
    logits = jnp.where(valid, logits, -jnp.inf)
    lse = jax.nn.logsumexp(logits, axis=-1)
    p = jnp.exp(logits - lse[..., None]).astype(v.dtype)
    o = jnp.einsum('nhcqk,nhcke->nhcqe', p, vb)
    o = o.reshape(B * dilation, H, nb * ATTN_BLOCK, E)[:, :, :L]
    o = o.reshape(B, dilation, H, L, E).transpose(0, 3, 1, 2, 4).reshape(B, S, H, E)
    lse = lse.reshape(B * dilation, H, nb * ATTN_BLOCK)[:, :, :L]
    lse = lse.reshape(B, dilation, H, L).transpose(0, 3, 1, 2).reshape(B, S, H)
    return o, lse


def dilated_attention_mixer(a_q, a_k, a_v, q_norm_g, k_norm_g, rel_bias):
    B, S, _ = a_q.shape
    heads = lambda t: t.reshape(B, S, ATTN_HEADS, ATTN_HEAD_DIM)
    q = rms_norm(heads(a_q), q_norm_g)
    k = rms_norm(heads(a_k), k_norm_g)
    v = heads(a_v)
    outs, lses = zip(*[dilated_window_attention(q, k, v, rel_bias, w, d) for w, d in DILATED_PATTERNS])
    weights = jax.nn.softmax(jnp.stack(lses), axis=0)
    o = jnp.sum(weights[..., None] * jnp.stack(outs).astype(jnp.float32), axis=0)
    return o.reshape(B, S, ATTN_WIDTH).astype(a_q.dtype)


def mlstm_chunkwise(q, k, v, i_pre, log_f):
    B, H, S, E = q.shape
    NC = S // MLSTM_CHUNK
    ch = lambda t: t.reshape((B, H, NC, MLSTM_CHUNK) + t.shape[3:])
    q, k, v, i_pre, log_f = ch(q), ch(k), ch(v), ch(i_pre), ch(log_f)
    b = jnp.cumsum(log_f, axis=-1)
    g = b[..., -1]
    a = g[..., None] - b + i_pre

    def step(carry, xs):
        C, n, m = carry
        kc, vc, ac, gc = xs
        m_new = jnp.maximum(gc + m, jnp.max(ac, axis=-1))
        w = jnp.exp(ac - m_new[..., None])
        decay = jnp.exp(gc + m - m_new)
        C_new = decay[..., None, None] * C + jnp.einsum('bhl,bhld,bhle->bhde', w, kc, vc)
        n_new = decay[..., None] * n + jnp.einsum('bhl,bhld->bhd', w, kc)
        return (C_new, n_new, m_new), (C, n, m)

    init = (jnp.zeros((B, H, E, E), jnp.float32), jnp.zeros((B, H, E), jnp.float32), jnp.zeros((B, H), jnp.float32))
    xs = tuple(jnp.moveaxis(t, 2, 0) for t in (k, v, a, g))
    _, (C_prev, n_prev, m_prev) = lax.scan(step, init, xs)
    C_prev = jnp.moveaxis(C_prev, 0, 2)
    n_prev = jnp.moveaxis(n_prev, 0, 2)
    m_prev = jnp.moveaxis(m_prev, 0, 2)

    lower = jnp.tril(jnp.ones((MLSTM_CHUNK, MLSTM_CHUNK), dtype=bool))
    D = jnp.where(lower, b[..., :, None] - b[..., None, :] + i_pre[..., None, :], -jnp.inf)
    inter = b + m_prev[..., None]
    m_t = jnp.maximum(inter, jnp.max(D, axis=-1))
    w_intra = jnp.exp(D - m_t[..., None]) * jnp.einsum('bhcte,bhcje->bhctj', q, k)
    w_inter = jnp.exp(inter - m_t)
    num = w_inter[..., None] * jnp.einsum('bhcte,bhced->bhctd', q, C_prev) + jnp.einsum('bhctj,bhcjd->bhctd', w_intra, v)
    den = w_inter * jnp.einsum('bhcte,bhce->bhct', q, n_prev) + jnp.sum(w_intra, axis=-1)
    h = num / jnp.maximum(jnp.abs(den), jnp.exp(-m_t))[..., None]
    return h.reshape(B, H, S, E)


def mlstm_mixer(x_m, o_pre, i_fw, f_fw, i_bw, f_bw, conv_w, conv_b, w_q_blk, w_k_blk, w_v_blk,
                b_igate, b_fgate, out_norm_g, skip):
    B, S, _ = x_m.shape
    xc = lax.conv_general_dilated(x_m, conv_w[:, None, :], window_strides=(1,),
                                  padding=((MLSTM_CONV // 2, MLSTM_CONV // 2),),
                                  dimension_numbers=('NWC', 'WIO', 'NWC'),
                                  feature_group_count=MLSTM_WIDTH) + conv_b
    xc = jax.nn.silu(xc)

    def blockdiag(t, w):
        y = jnp.einsum('bsgi,gij->bsgj', t.reshape(B, S, -1, MLSTM_QKV_BLOCK), w)
        return y.reshape(B, S, MLSTM_HEADS, MLSTM_HEAD_DIM).transpose(0, 2, 1, 3).astype(jnp.float32)

    q = blockdiag(xc, w_q_blk)
    k = blockdiag(xc, w_k_blk) / math.sqrt(MLSTM_HEAD_DIM)
    v = blockdiag(x_m, w_v_blk)
    gate = lambda t, bias: (t.astype(jnp.float32) + bias.astype(jnp.float32)).transpose(0, 2, 1)
    i_f, logf_f = gate(i_fw, b_igate[0]), jax.nn.log_sigmoid(gate(f_fw, b_fgate[0]))
    i_b, logf_b = gate(i_bw, b_igate[1]), jax.nn.log_sigmoid(gate(f_bw, b_fgate[1]))
    flip = lambda t: jnp.flip(t, axis=2)
    h_fwd = mlstm_chunkwise(q, k, v, i_f, logf_f)
    h_bwd = flip(mlstm_chunkwise(flip(q), flip(k), flip(v), flip(i_b), flip(logf_b)))
    h = (h_fwd + h_bwd).transpose(0, 2, 1, 3)
    h = h * lax.rsqrt(jnp.mean(h * h, axis=-1, keepdims=True) + NORM_EPS)
    h = h * out_norm_g.astype(jnp.float32).reshape(MLSTM_HEADS, MLSTM_HEAD_DIM)
    h = h.reshape(B, S, MLSTM_WIDTH).astype(x_m.dtype) + skip * xc
    return h * jax.nn.sigmoid(o_pre)


def expert_choice_ffn(h, w_router, b_router, w_gate, w_up, w_down):
    B, S, D = h.shape
    cap = (EC_CAPACITY_FACTOR * S) // N_EXPERTS
    logits = (h @ w_router + b_router).astype(jnp.float32)
    affinity = jax.nn.softmax(logits, axis=-1)
    gates, idx = lax.top_k(affinity.transpose(0, 2, 1), cap)
    bidx = jnp.arange(B)[:, None, None]
    xin = h[bidx, idx]
    hid = jax.nn.silu(jnp.einsum('becd,edf->becf', xin, w_gate)) * jnp.einsum('becd,edf->becf', xin, w_up)
    y = jnp.einsum('becf,efd->becd', hid, w_down) * gates[..., None].astype(h.dtype)
    return jnp.zeros_like(h).at[bidx, idx].add(y)


def setup_inputs(seed: int = 0) -> dict:
    key = jax.random.key(seed)
    ks = jax.random.split(key, 26)
    f32 = jnp.float32
    nrm = lambda k, shape, scale: scale * jax.random.normal(k, shape, f32)
    L = DEPTH
    H = MLSTM_HEADS
    nblk = MLSTM_WIDTH // MLSTM_QKV_BLOCK
    return {
        "x": nrm(ks[0], (BATCH, SEQ, D_MODEL), 1.0),
        "c": nrm(ks[1], (BATCH, D_MODEL), 1.0),
        "w_ada": nrm(ks[2], (L, D_MODEL, N_MOD * D_MODEL), 0.5 * D_MODEL ** -0.5),
        "b_ada": nrm(ks[3], (L, N_MOD * D_MODEL), 0.02),
        "norm1_g": 1.0 + nrm(ks[4], (L, D_MODEL), 0.02),
        "w_in": nrm(ks[5], (L, D_MODEL, PROJ_TOTAL), D_MODEL ** -0.5),
        "conv_w": nrm(ks[6], (L, MLSTM_CONV, MLSTM_WIDTH), MLSTM_CONV ** -0.5),
        "conv_b": nrm(ks[7], (L, MLSTM_WIDTH), 0.02),
        "w_q_blk": nrm(ks[8], (L, nblk, MLSTM_QKV_BLOCK, MLSTM_QKV_BLOCK), MLSTM_QKV_BLOCK ** -0.5),
        "w_k_blk": nrm(ks[9], (L, nblk, MLSTM_QKV_BLOCK, MLSTM_QKV_BLOCK), MLSTM_QKV_BLOCK ** -0.5),
        "w_v_blk": nrm(ks[10], (L, nblk, MLSTM_QKV_BLOCK, MLSTM_QKV_BLOCK), MLSTM_QKV_BLOCK ** -0.5),
        "b_igate": nrm(ks[11], (L, 2, H), 0.1),
        "b_fgate": jnp.linspace(3.0, 6.0, H, dtype=f32)[None, None, :] + nrm(ks[12], (L, 2, H), 0.1),
        "mlstm_norm_g": 1.0 + nrm(ks[13], (L, MLSTM_WIDTH), 0.02),
        "mlstm_skip": 1.0 + nrm(ks[14], (L, MLSTM_WIDTH), 0.02),
        "q_norm_g": 1.0 + nrm(ks[15], (L, ATTN_HEAD_DIM), 0.02),
        "k_norm_g": 1.0 + nrm(ks[16], (L, ATTN_HEAD_DIM), 0.02),
        "rel_bias": nrm(ks[17], (REL_BUCKETS, ATTN_HEADS), 0.5),
        "w_out": nrm(ks[18], (L, MIX_WIDTH, D_MODEL), MIX_WIDTH ** -0.5),
        "norm2_g": 1.0 + nrm(ks[19], (L, D_MODEL), 0.02),
        "w_router": nrm(ks[20], (L, D_MODEL, N_EXPERTS), D_MODEL ** -0.5),
        "b_router": nrm(ks[21], (L, N_EXPERTS), 0.01),
        "w_gate": nrm(ks[22], (L, N_EXPERTS, D_MODEL, D_FF_EXPERT), D_MODEL ** -0.5),
        "w_up": nrm(ks[23], (L, N_EXPERTS, D_MODEL, D_FF_EXPERT), D_MODEL ** -0.5),
        "w_down": nrm(ks[24], (L, N_EXPERTS, D_FF_EXPERT, D_MODEL), D_FF_EXPERT ** -0.5),
    }


def reference(x, c, w_ada, b_ada, norm1_g, w_in, conv_w, conv_b, w_q_blk, w_k_blk, w_v_blk,
              b_igate, b_fgate, mlstm_norm_g, mlstm_skip, q_norm_g, k_norm_g, rel_bias, w_out,
              norm2_g, w_router, b_router, w_gate, w_up, w_down):
    split_points = [int(s) for s in np.cumsum(PROJ_WIDTHS)[:-1]]
    for l in range(DEPTH):
        mod = jax.nn.silu(c) @ w_ada[l] + b_ada[l]
        shift1, scale1, gate1, shift2, scale2, gate2 = jnp.split(mod[:, None, :], N_MOD, axis=-1)

        h = rms_norm(x, norm1_g[l]) * (1.0 + scale1) + shift1
        proj = h @ w_in[l]
        x_m, o_pre, i_fw, f_fw, i_bw, f_bw, a_q, a_k, a_v = jnp.split(proj, split_points, axis=-1)
        y_mlstm = mlstm_mixer(x_m, o_pre, i_fw, f_fw, i_bw, f_bw, conv_w[l], conv_b[l], w_q_blk[l],
                              w_k_blk[l], w_v_blk[l], b_igate[l], b_fgate[l], mlstm_norm_g[l], mlstm_skip[l])
        y_attn = dilated_attention_mixer(a_q, a_k, a_v, q_norm_g[l], k_norm_g[l], rel_bias)
        mix = jnp.concatenate([y_mlstm, y_attn], axis=-1) @ w_out[l]
        x = x + gate1 * mix

        h2 = rms_norm(x, norm2_g[l]) * (1.0 + scale2) + shift2
        x = x + gate2 * expert_choice_ffn(h2, w_router[l], b_router[l], w_gate[l], w_up[l], w_down[l])
    return x
```

```python
import functools
import math

import jax
import jax.numpy as jnp
from jax import lax
from jax.experimental import pallas as pl
from jax.experimental.pallas import tpu as pltpu

F32 = jnp.float32
BF16 = jnp.bfloat16
I32 = jnp.int32

NORM_EPS = 1e-6
N_MOD = 6
MLSTM_HEADS = 4
MLSTM_HEAD_DIM = 128
MLSTM_QKV_BLOCK = 4
MLSTM_CONV = 5
MLSTM_CHUNK = 128
ATTN_HEADS = 8
ATTN_HEAD_DIM = 64
DILATIONS = (1, 4, 16)
ATTN_HALF = 64
ATTN_QBLK = 128
ATTN_KBLK = 256
REL_BUCKETS = 32
REL_MAX_DIST = 1024
N_EXPERTS = 16
EC_CAPACITY_FACTOR = 2
NEG_BIG = -1e30

V7X_LANES = 128
V7X_VMEM_LIMIT = 56 * 1024 * 1024


def _sigmoid(x):
    return 1.0 / (1.0 + jnp.exp(-x))


def _dot(a, b):
    return jnp.dot(a, b, preferred_element_type=F32)


def _dot_nt(a, b):
    return lax.dot_general(a, b, (((1,), (1,)), ((), ())), preferred_element_type=F32)


def _dot_tn(a, b):
    return lax.dot_general(a, b, (((0,), (0,)), ((), ())), preferred_element_type=F32)


def _split_bf16(x):
    hi = x.astype(BF16)
    lo = (x - hi.astype(F32)).astype(BF16)
    return hi, lo


def _params(sem, vmem=None):
    return pltpu.CompilerParams(dimension_semantics=sem, vmem_limit_bytes=vmem)


def _ada_kernel(c_ref, w_ref, b_ref, o_ref):
    c = c_ref[...]
    s = c * _sigmoid(c)
    o_ref[...] = jnp.dot(s, w_ref[...], preferred_element_type=F32,
                         precision=lax.Precision.HIGHEST) + b_ref[...]


def _ada(c, w, b):
    B, D = c.shape
    N = w.shape[1]
    tn = 1024
    return pl.pallas_call(
        _ada_kernel,
        grid=(N // tn,),
        in_specs=[pl.BlockSpec((B, D), lambda j: (0, 0)),
                  pl.BlockSpec((D, tn), lambda j: (0, j)),
                  pl.BlockSpec((1, tn), lambda j: (0, j))],
        out_specs=pl.BlockSpec((B, tn), lambda j: (0, j)),
        out_shape=jax.ShapeDtypeStruct((B, N), F32),
        compiler_params=_params(("arbitrary",)),
        name="ada",
    )(c, w, b.reshape(1, N))


def _modulated_norm(x, g, scale, shift):
    ms = jnp.mean(x * x, axis=-1, keepdims=True)
    y = x * lax.rsqrt(ms + NORM_EPS) * g
    return y * (1.0 + scale) + shift


def _inproj_kernel(x_ref, sc_ref, sh_ref, g_ref, w_ref, xm_ref, op_ref, q_ref, k_ref, v_ref, gt_ref):
    h = _modulated_norm(x_ref[0], g_ref[...], sc_ref[0], sh_ref[0]).astype(BF16)
    for i, o in enumerate((xm_ref, op_ref, q_ref, k_ref, v_ref)):
        o[0] = _dot(h, w_ref[:, 512 * i:512 * (i + 1)])
    gt_ref[0] = _dot(h, w_ref[:, 2560:2688])[:, :16]


def _inproj(x, scale, shift, g, w):
    B, S, D = x.shape
    tm = 512
    row = lambda b, i: (b, i, 0)
    vec = lambda b, i: (b, 0, 0)
    outs = [jax.ShapeDtypeStruct((B, S, 512), F32)] * 5 + [jax.ShapeDtypeStruct((B, S, 16), F32)]
    return pl.pallas_call(
        _inproj_kernel,
        grid=(B, S // tm),
        in_specs=[pl.BlockSpec((1, tm, D), row),
                  pl.BlockSpec((1, 1, D), vec),
                  pl.BlockSpec((1, 1, D), vec),
                  pl.BlockSpec((1, D), lambda b, i: (0, 0)),
                  pl.BlockSpec(w.shape, lambda b, i: (0, 0))],
        out_specs=[pl.BlockSpec((1, tm, 512), row)] * 5 + [pl.BlockSpec((1, tm, 16), row)],
        out_shape=outs,
        compiler_params=_params(("parallel", "arbitrary"), V7X_VMEM_LIMIT),
        name="inproj",
    )(x, scale, shift, g, w)


def _chunk_cumsum(x, axis, reverse):
    n = x.shape[axis]
    idx = lax.broadcasted_iota(I32, x.shape, axis) & (MLSTM_CHUNK - 1)
    s = 1
    while s < MLSTM_CHUNK:
        if reverse:
            r = pltpu.roll(x, n - s, axis)
            x = x + jnp.where(idx < MLSTM_CHUNK - s, r, 0.0)
        else:
            r = pltpu.roll(x, s, axis)
            x = x + jnp.where(idx >= s, r, 0.0)
        s *= 2
    return x


def _log_sigmoid(x):
    return jnp.minimum(x, 0.0) - jnp.log(1.0 + jnp.exp(-jnp.abs(x)))


def _mlstm_kernel(xm_ref, op_ref, gcol_ref, grow_ref, bcol_ref, brow_ref, cw_ref, cb_ref,
                  wq_ref, wk_ref, wv_ref, ng_ref, sk_ref, y_ref,
                  q_s, k_s, v_s, xc_s, hf_s, hb_s, ccol_s, crow_s, graw_s):
    S = xm_ref.shape[1]
    L = MLSTM_CHUNK
    NC = S // L
    xm = xm_ref[0]

    rows = lax.broadcasted_iota(I32, xm.shape, 0)
    half = MLSTM_CONV // 2
    conv = xm * cw_ref[half:half + 1, :]
    for j in range(MLSTM_CONV):
        off = j - half
        if off == 0:
            continue
        shifted = pltpu.roll(xm, (-off) % S, 0)
        valid = rows >= -off if off < 0 else rows < S - off
        conv = conv + jnp.where(valid, shifted, 0.0) * cw_ref[j:j + 1, :]
    conv = conv + cb_ref[...]
    xc = conv * _sigmoid(conv)
    xc_s[...] = xc

    xcb = xc.astype(BF16)
    q_s[...] = _dot(xcb, wq_ref[0])
    k_s[...] = _dot(xcb, wk_ref[0]) * (1.0 / math.sqrt(MLSTM_HEAD_DIM))
    v_s[...] = _dot(xm.astype(BF16), wv_ref[0])

    gc = gcol_ref[0, 0] + bcol_ref[0]
    kind_c = lax.broadcasted_iota(I32, gc.shape, 1)
    gc = jnp.where((kind_c & 1) == 1, _log_sigmoid(gc), gc)
    ccol_s[...] = jnp.where(kind_c == 1, _chunk_cumsum(gc, 0, False),
                            jnp.where(kind_c == 3, _chunk_cumsum(gc, 0, True), gc))
    gr = grow_ref[0, 0] + brow_ref[0]
    kind_r = lax.broadcasted_iota(I32, gr.shape, 0)
    gr = jnp.where((kind_r & 1) == 1, _log_sigmoid(gr), gr)
    graw_s[...] = gr
    crow_s[...] = jnp.where(kind_r == 1, _chunk_cumsum(gr, 1, False),
                            jnp.where(kind_r == 3, _chunk_cumsum(gr, 1, True), gr))

    ti = lax.broadcasted_iota(I32, (L, L), 0)
    tj = lax.broadcasted_iota(I32, (L, L), 1)
    causal = (tj <= ti, tj >= ti)

    def chunk(c, dirn, state):
        C, n, m = state
        r0 = pl.multiple_of(c * L, L)
        qc = q_s[pl.ds(r0, L), :]
        kc = k_s[pl.ds(r0, L), :]
        vc = v_s[pl.ds(r0, L), :]
        cc = ccol_s[pl.ds(r0, L), :]
        cr = crow_s[:, pl.ds(r0, L)]
        graw = graw_s[:, pl.ds(r0, L)]
        ic, bc = cc[:, 2 * dirn:2 * dirn + 1], cc[:, 2 * dirn + 1:2 * dirn + 2]
        ir, br = cr[2 * dirn:2 * dirn + 1, :], cr[2 * dirn + 1:2 * dirn + 2, :]
        g = jnp.sum(graw[2 * dirn + 1:2 * dirn + 2, :], axis=1, keepdims=True)
        qb, kb, vb = qc.astype(BF16), kc.astype(BF16), vc.astype(BF16)

        D = jnp.where(causal[dirn], bc - br + ir, NEG_BIG)
        inter = bc + m
        m_t = jnp.maximum(inter, jnp.max(D, axis=1, keepdims=True))
        w_intra = jnp.exp(D - m_t) * _dot_nt(qb, kb)
        w_inter = jnp.exp(inter - m_t)
        num = w_inter * _dot(qb, C.astype(BF16)) + _dot(w_intra.astype(BF16), vb)
        den = w_inter * jnp.sum(qc * n, axis=1, keepdims=True) + jnp.sum(w_intra, axis=1, keepdims=True)
        h = num / jnp.maximum(jnp.abs(den), jnp.exp(-m_t))

        a_r = g - br + ir
        a_c = g - bc + ic
        m_new = jnp.maximum(g + m, jnp.max(a_r, axis=1, keepdims=True))
        decay = jnp.exp(g + m - m_new)
        kw = kc * jnp.exp(a_c - m_new)
        C_new = decay * C + _dot_tn(kw.astype(BF16), vb)
        n_new = decay * n + jnp.sum(kw, axis=0, keepdims=True)
        return h, (C_new, n_new, m_new)

    def body(c, carry):
        sf, sb = carry
        h_f, sf = chunk(c, 0, sf)
        hf_s[pl.ds(pl.multiple_of(c * L, L), L), :] = h_f
        cb = NC - 1 - c
        h_b, sb = chunk(cb, 1, sb)
        hb_s[pl.ds(pl.multiple_of(cb * L, L), L), :] = h_b
        return sf, sb

    E = MLSTM_HEAD_DIM
    init = (jnp.zeros((E, E), F32), jnp.zeros((1, E), F32), jnp.zeros((1, 1), F32))
    lax.fori_loop(0, NC, body, (init, init))

    h = hf_s[...] + hb_s[...]
    hn = h * lax.rsqrt(jnp.mean(h * h, axis=-1, keepdims=True) + NORM_EPS) * ng_ref[...]
    y = (hn + sk_ref[...] * xc_s[...]) * _sigmoid(op_ref[0])
    y_ref[0] = y.astype(y_ref.dtype)


def _blockdiag_dense(w_blk):
    nblk = w_blk.shape[0]
    per_head = nblk // MLSTM_HEADS
    w = w_blk.reshape(MLSTM_HEADS, per_head, MLSTM_QKV_BLOCK, MLSTM_QKV_BLOCK)
    eye = jnp.eye(per_head, dtype=w.dtype)
    dense = jnp.einsum('hgij,gk->hgikj', w, eye)
    return dense.reshape(MLSTM_HEADS, MLSTM_HEAD_DIM, MLSTM_HEAD_DIM)


def _mlstm(x_m, o_pre, gates, conv_w, conv_b, w_q_blk, w_k_blk, w_v_blk, b_igate, b_fgate, norm_g, skip):
    B, S, W = x_m.shape
    H, E = MLSTM_HEADS, MLSTM_HEAD_DIM
    g4 = gates.reshape(B, S, 4, H)
    gcol = g4.transpose(0, 3, 1, 2)
    grow = g4.transpose(0, 3, 2, 1)
    bk = jnp.stack([b_igate[0], b_fgate[0], b_igate[1], b_fgate[1]], axis=0)
    bcol = bk.T.reshape(H, 1, 4)
    brow = bk.T.reshape(H, 4, 1)
    wq, wk, wv = (_blockdiag_dense(w).astype(BF16) for w in (w_q_blk, w_k_blk, w_v_blk))
    headcol = lambda b, h: (b, 0, h)
    return pl.pallas_call(
        _mlstm_kernel,
        grid=(B, H),
        in_specs=[pl.BlockSpec((1, S, E), headcol),
                  pl.BlockSpec((1, S, E), headcol),
                  pl.BlockSpec((1, 1, S, 4), lambda b, h: (b, h, 0, 0)),
                  pl.BlockSpec((1, 1, 4, S), lambda b, h: (b, h, 0, 0)),
                  pl.BlockSpec((1, 1, 4), lambda b, h: (h, 0, 0)),
                  pl.BlockSpec((1, 4, 1), lambda b, h: (h, 0, 0)),
                  pl.BlockSpec((MLSTM_CONV, E), lambda b, h: (0, h)),
                  pl.BlockSpec((1, E), lambda b, h: (0, h)),
                  pl.BlockSpec((1, E, E), lambda b, h: (h, 0, 0)),
                  pl.BlockSpec((1, E, E), lambda b, h: (h, 0, 0)),
                  pl.BlockSpec((1, E, E), lambda b, h: (h, 0, 0)),
                  pl.BlockSpec((1, E), lambda b, h: (0, h)),
                  pl.BlockSpec((1, E), lambda b, h: (0, h))],
        out_specs=pl.BlockSpec((1, S, E), headcol),
        out_shape=jax.ShapeDtypeStruct((B, S, W), BF16),
        scratch_shapes=[pltpu.VMEM((S, E), F32)] * 6
                       + [pltpu.VMEM((S, 4), F32), pltpu.VMEM((4, S), F32), pltpu.VMEM((4, S), F32)],
        compiler_params=_params(("parallel", "arbitrary"), V7X_VMEM_LIMIT),
        name="mlstm",
    )(x_m, o_pre, gcol, grow, bcol, brow, conv_w, conv_b.reshape(1, W), wq, wk, wv,
      norm_g.reshape(1, W), skip.reshape(1, W))


def _t5_bucket(rel):
    half = REL_BUCKETS // 2
    exact = half // 2
    n = jnp.abs(rel)
    log_ratio = jnp.log(jnp.maximum(n, 1).astype(F32) / exact) / math.log(REL_MAX_DIST / exact)
    large = jnp.minimum(exact + (log_ratio * (half - exact)).astype(I32), half - 1)
    return jnp.where(rel > 0, half, 0) + jnp.where(n < exact, n, large)


def _attn_bias_tables(rel_bias):
    qi = jnp.arange(ATTN_QBLK)[:, None]
    kj = jnp.arange(ATTN_KBLK)[None, :]
    tabs = []
    for d in DILATIONS:
        for v in range(3):
            rel = kj - qi - ATTN_HALF * v
            b = rel_bias[_t5_bucket(rel * d)].astype(F32)
            b = jnp.where((jnp.abs(rel) <= ATTN_HALF)[..., None], b, NEG_BIG)
            tabs.append(b.transpose(2, 0, 1))
    t = jnp.stack(tabs, axis=0)
    t = t.reshape(9, ATTN_HEADS // 2, 2, ATTN_QBLK, ATTN_KBLK).transpose(1, 0, 2, 3, 4)
    return t.reshape(ATTN_HEADS // 2, 18, ATTN_QBLK, ATTN_KBLK)


def _attn_kernel(q_ref, k_ref, v_ref, qg_ref, kg_ref, ones_ref, bias_ref, y_ref, qs, ks, o_s, m_s, l_s):
    S = q_ref.shape[1]
    QB, KB = ATTN_QBLK, ATTN_KBLK

    def qk_norm(x, g):
        hi, lo = _split_bf16(x * x)
        ms = _dot(hi, ones_ref[...]) + _dot(lo, ones_ref[...])
        return x * lax.rsqrt(ms + NORM_EPS) * g

    qs[...] = qk_norm(q_ref[0], qg_ref[...])
    ks[...] = qk_norm(k_ref[0], kg_ref[...])

    lane = lax.broadcasted_iota(I32, (1, V7X_LANES), 1)
    head0 = lane < ATTN_HEAD_DIM
    lane_mask = (head0.astype(F32), 1.0 - head0.astype(F32))

    def rows(start, size, d):
        return pl.ds(start, size) if d == 1 else pl.ds(start, size, stride=d)

    def block(p, d, row0, key0, nk, variant):
        qb = qs[rows(row0, QB, d), :]
        kb = ks[rows(key0, nk, d), :].astype(BF16)
        vb = v_ref[0, rows(key0, nk, d), :].astype(BF16)
        res = []
        for a in range(2):
            s = _dot_nt((qb * lane_mask[a]).astype(BF16), kb) + bias_ref[0, p * 6 + variant * 2 + a][:, :nk]
            m = jnp.max(s, axis=1, keepdims=True)
            e = jnp.exp(s - m)
            l = jnp.sum(e, axis=1, keepdims=True)
            res.append((_dot(e.astype(BF16), vb), m, l))
        out_rows = rows(row0, QB, d)
        o_s[p, out_rows, :] = jnp.where(head0, res[0][0], res[1][0])
        m_s[p, out_rows, :] = jnp.where(head0, res[0][1], res[1][1])
        l_s[p, out_rows, :] = jnp.where(head0, res[0][2], res[1][2])

    for p, d in enumerate(DILATIONS):
        Lp = S // d
        nb = Lp // QB

        def phase(r, carry, p=p, d=d, Lp=Lp, nb=nb):
            if nb == 1:
                block(p, d, r, r, QB, 0)
                return carry
            block(p, d, r, r, KB, 0)

            def interior(j, c2):
                block(p, d, r + d * (QB * j), r + d * (QB * j - ATTN_HALF), KB, 1)
                return c2

            lax.fori_loop(1, nb - 1, interior, 0)
            block(p, d, r + d * (Lp - QB), r + d * (Lp - KB), KB, 2)
            return carry

        if d == 1:
            phase(0, 0)
        else:
            lax.fori_loop(0, d, phase, 0)

    mx = jnp.maximum(jnp.maximum(m_s[0], m_s[1]), m_s[2])
    num = jnp.zeros((S, V7X_LANES), F32)
    den = jnp.zeros((S, V7X_LANES), F32)
    for p in range(len(DILATIONS)):
        w = jnp.exp(m_s[p] - mx)
        num = num + w * o_s[p]
        den = den + w * l_s[p]
    y_ref[0] = (num / den).astype(y_ref.dtype)


def _attn(a_q, a_k, a_v, q_norm_g, k_norm_g, rel_bias):
    B, S, W = a_q.shape
    P = ATTN_HEADS // 2
    qg = jnp.tile(q_norm_g, 2).reshape(1, V7X_LANES) * (1.0 / math.sqrt(ATTN_HEAD_DIM))
    kg = jnp.tile(k_norm_g, 2).reshape(1, V7X_LANES)
    li = jnp.arange(V7X_LANES) // ATTN_HEAD_DIM
    ones = jnp.where(li[:, None] == li[None, :], 1.0 / ATTN_HEAD_DIM, 0.0).astype(BF16)
    bias = _attn_bias_tables(rel_bias)
    pair = lambda b, p: (b, 0, p)
    return pl.pallas_call(
        _attn_kernel,
        grid=(B, P),
        in_specs=[pl.BlockSpec((1, S, V7X_LANES), pair)] * 3
                 + [pl.BlockSpec((1, V7X_LANES), lambda b, p: (0, 0))] * 2
                 + [pl.BlockSpec((V7X_LANES, V7X_LANES), lambda b, p: (0, 0)),
                    pl.BlockSpec((1, 18, ATTN_QBLK, ATTN_KBLK), lambda b, p: (p, 0, 0, 0))],
        out_specs=pl.BlockSpec((1, S, V7X_LANES), pair),
        out_shape=jax.ShapeDtypeStruct((B, S, W), BF16),
        scratch_shapes=[pltpu.VMEM((S, V7X_LANES), F32)] * 2
                       + [pltpu.VMEM((len(DILATIONS), S, V7X_LANES), F32)] * 3,
        compiler_params=_params(("parallel", "arbitrary"), V7X_VMEM_LIMIT),
        name="attn",
    )(a_q, a_k, a_v, qg, kg, ones, bias)


def _outproj_kernel(ym_ref, ya_ref, x_ref, g1_ref, w1_ref, w2_ref, o_ref):
    mix = _dot(ym_ref[0], w1_ref[...]) + _dot(ya_ref[0], w2_ref[...])
    o_ref[0] = x_ref[0] + g1_ref[0] * mix


def _outproj(y_m, y_a, x, gate1, w_out):
    B, S, D = x.shape
    Wm = y_m.shape[-1]
    tm = 512
    w1 = w_out[:Wm].astype(BF16)
    w2 = w_out[Wm:].astype(BF16)
    row = lambda b, i: (b, i, 0)
    return pl.pallas_call(
        _outproj_kernel,
        grid=(B, S // tm),
        in_specs=[pl.BlockSpec((1, tm, Wm), row),
                  pl.BlockSpec((1, tm, y_a.shape[-1]), row),
                  pl.BlockSpec((1, tm, D), row),
                  pl.BlockSpec((1, 1, D), lambda b, i: (b, 0, 0)),
                  pl.BlockSpec(w1.shape, lambda b, i: (0, 0)),
                  pl.BlockSpec(w2.shape, lambda b, i: (0, 0))],
        out_specs=pl.BlockSpec((1, tm, D), row),
        out_shape=jax.ShapeDtypeStruct((B, S, D), F32),
        compiler_params=_params(("parallel", "arbitrary"), V7X_VMEM_LIMIT),
        name="outproj",
    )(y_m, y_a, x, gate1, w1, w2)


def _route_kernel(x_ref, sc_ref, sh_ref, g_ref, whi_ref, wlo_ref, br_ref, tri_ref,
                  xin_ref, pos_ref, gate_ref, h_s, pos_s, *, cap):
    S = x_ref.shape[1]
    NE = N_EXPERTS
    e = pl.program_id(1)

    @pl.when(e == 0)
    def _():
        h = _modulated_norm(x_ref[0], g_ref[...], sc_ref[0], sh_ref[0])
        hi, lo = _split_bf16(h)
        h_s[...] = hi
        logits = _dot(hi, whi_ref[...]) + _dot(lo, whi_ref[...]) + _dot(hi, wlo_ref[...])
        lt = logits.T[:NE, :] + br_ref[...]
        ex = jnp.exp(lt - jnp.max(lt, axis=0, keepdims=True))
        aff = ex / jnp.sum(ex, axis=0, keepdims=True)
        gate_ref[0] = aff

        bits = pltpu.bitcast(aff, I32)

        def search(i, v):
            cand = v | lax.shift_left(jnp.int32(1), 30 - i)
            cnt = jnp.sum((bits >= cand).astype(F32), axis=1, keepdims=True)
            return jnp.where(cnt >= cap, cand, v)

        thr = lax.fori_loop(0, 31, search, jnp.zeros((NE, 1), I32))
        gt = (bits > thr).astype(F32)
        eq = (bits == thr).astype(F32)
        need = cap - jnp.sum(gt, axis=1, keepdims=True)

        def prefix_count(mask):
            off = jnp.zeros((NE, 1), F32)
            parts = []
            for j in range(S // V7X_LANES):
                t = mask[:, j * V7X_LANES:(j + 1) * V7X_LANES]
                parts.append(_dot(t.astype(BF16), tri_ref[...]) + off)
                off = off + jnp.sum(t, axis=1, keepdims=True)
            return jnp.concatenate(parts, axis=1)

        sel = jnp.maximum(gt, jnp.where(prefix_count(eq) < need, eq, 0.0))
        pos = jnp.where(sel > 0.0, prefix_count(sel), -1.0).astype(I32)
        pos_s[...] = pos
        pos_ref[0] = pos

    prow = pos_s[pl.ds(e, 1), :]
    slot = lax.broadcasted_iota(I32, (cap, S), 0)
    onehot = jnp.where(prow == slot, 1.0, 0.0).astype(BF16)
    xin_ref[0, 0] = _dot(onehot, h_s[...]).astype(xin_ref.dtype)


def _route(x1, scale, shift, g, w_router, b_router):
    B, S, D = x1.shape
    NE = N_EXPERTS
    cap = (EC_CAPACITY_FACTOR * S) // NE
    wpad = jnp.zeros((D, V7X_LANES), F32).at[:, :NE].set(w_router)
    whi, wlo = _split_bf16(wpad)
    ti = jnp.arange(V7X_LANES)
    tri = (ti[:, None] < ti[None, :]).astype(BF16)
    vec = lambda b, e: (b, 0, 0)
    const = lambda b, e: (0, 0)
    return pl.pallas_call(
        functools.partial(_route_kernel, cap=cap),
        grid=(B, NE),
        in_specs=[pl.BlockSpec((1, S, D), vec),
                  pl.BlockSpec((1, 1, D), vec),
                  pl.BlockSpec((1, 1, D), vec),
                  pl.BlockSpec((1, D), const),
                  pl.BlockSpec((D, V7X_LANES), const),
                  pl.BlockSpec((D, V7X_LANES), const),
                  pl.BlockSpec((NE, 1), const),
                  pl.BlockSpec((V7X_LANES, V7X_LANES), const)],
        out_specs=[pl.BlockSpec((1, 1, cap, D), lambda b, e: (e, b, 0, 0)),
                   pl.BlockSpec((1, NE, S), vec),
                   pl.BlockSpec((1, NE, S), vec)],
        out_shape=[jax.ShapeDtypeStruct((NE, B, cap, D), BF16),
                   jax.ShapeDtypeStruct((B, NE, S), I32),
                   jax.ShapeDtypeStruct((B, NE, S), F32)],
        scratch_shapes=[pltpu.VMEM((S, D), BF16), pltpu.VMEM((NE, S), I32)],
        compiler_params=_params(("parallel", "arbitrary"), V7X_VMEM_LIMIT),
        name="route",
    )(x1, scale, shift, g, whi, wlo, b_router.reshape(NE, 1), tri)


def _expert_kernel(x_ref, wg_ref, wu_ref, wd_ref, o_ref, acc_s, wg_s, wu_s, wd_s, *, row_tile):
    f = pl.program_id(2)
    nf = pl.num_programs(2)
    wg_s[...] = wg_ref[0].astype(BF16)
    wu_s[...] = wu_ref[0].astype(BF16)
    wd_s[...] = wd_ref[0].astype(BF16)
    n_tiles = x_ref.shape[1] // row_tile

    def sweep(first, last):
        def body(i, carry):
            r = pl.ds(pl.multiple_of(i * row_tile, row_tile), row_tile)
            xb = x_ref[0, r, :]
            g = _dot(xb, wg_s[...])
            u = _dot(xb, wu_s[...])
            y = _dot((g * _sigmoid(g) * u).astype(BF16), wd_s[...])
            if not first:
                y = y + acc_s[r, :]
            if last:
                o_ref[0, r, :] = y.astype(o_ref.dtype)
            else:
                acc_s[r, :] = y
            return carry
        lax.fori_loop(0, n_tiles, body, 0)

    @pl.when(f == 0)
    def _():
        sweep(True, False)

    @pl.when(jnp.logical_and(f > 0, f < nf - 1))
    def _():
        sweep(False, False)

    @pl.when(f == nf - 1)
    def _():
        sweep(False, True)


def _experts(xin, w_gate, w_up, w_down):
    NE, R, D = xin.shape
    F = w_gate.shape[-1]
    tr = min(R, 2048)
    tf = 512
    row_tile = min(tr, 512)
    return pl.pallas_call(
        functools.partial(_expert_kernel, row_tile=row_tile),
        grid=(NE, R // tr, F // tf),
        in_specs=[pl.BlockSpec((1, tr, D), lambda e, r, f: (e, r, 0)),
                  pl.BlockSpec((1, D, tf), lambda e, r, f: (e, 0, f)),
                  pl.BlockSpec((1, D, tf), lambda e, r, f: (e, 0, f)),
                  pl.BlockSpec((1, tf, D), lambda e, r, f: (e, f, 0))],
        out_specs=pl.BlockSpec((1, tr, D), lambda e, r, f: (e, r, 0)),
        out_shape=jax.ShapeDtypeStruct((NE, R, D), BF16),
        scratch_shapes=[pltpu.VMEM((tr, D), F32), pltpu.VMEM((D, tf), BF16),
                        pltpu.VMEM((D, tf), BF16), pltpu.VMEM((tf, D), BF16)],
        compiler_params=_params(("parallel", "parallel", "arbitrary"), V7X_VMEM_LIMIT),
        name="experts",
    )(xin, w_gate, w_up, w_down)


def _combine_kernel(y_ref, pos_ref, gate_ref, x_ref, g2_ref, o_ref):
    ts = x_ref.shape[1]
    cap = y_ref.shape[2]
    pos = pos_ref[0]
    gate = gate_ref[0]
    slot = lax.broadcasted_iota(I32, (ts, cap), 1)
    acc = jnp.zeros((ts, x_ref.shape[2]), F32)
    for e in range(N_EXPERTS):
        scatter = jnp.where(pos[:, e:e + 1] == slot, gate[:, e:e + 1], 0.0).astype(BF16)
        acc = acc + _dot(scatter, y_ref[e, 0])
    o_ref[0] = x_ref[0] + g2_ref[0] * acc


def _combine(y, pos_t, gate_t, x1, gate2):
    B, S, D = x1.shape
    NE, _, cap, _ = y.shape
    ts = 512
    row = lambda b, i: (b, i, 0)
    return pl.pallas_call(
        _combine_kernel,
        grid=(B, S // ts),
        in_specs=[pl.BlockSpec((NE, 1, cap, D), lambda b, i: (0, b, 0, 0)),
                  pl.BlockSpec((1, ts, NE), row),
                  pl.BlockSpec((1, ts, NE), row),
                  pl.BlockSpec((1, ts, D), row),
                  pl.BlockSpec((1, 1, D), lambda b, i: (b, 0, 0))],
        out_specs=pl.BlockSpec((1, ts, D), row),
        out_shape=jax.ShapeDtypeStruct((B, S, D), F32),
        compiler_params=_params(("parallel", "arbitrary"), V7X_VMEM_LIMIT),
        name="combine",
    )(y, pos_t, gate_t, x1, gate2)


def _inproj_weight(w_in):
    D = w_in.shape[0]
    main = jnp.concatenate([w_in[:, :1024], w_in[:, 1040:2576], w_in[:, 1024:1040],
                            jnp.zeros((D, V7X_LANES - 16), w_in.dtype)], axis=1)
    return main.astype(BF16)


def kernel(x, c, w_ada, b_ada, norm1_g, w_in, conv_w, conv_b, w_q_blk, w_k_blk, w_v_blk, b_igate, b_fgate,
           mlstm_norm_g, mlstm_skip, q_norm_g, k_norm_g, rel_bias, w_out, norm2_g, w_router, b_router,
           w_gate, w_up, w_down):
    B, S, D = x.shape
    depth = w_ada.shape[0]
    for l in range(depth):
        mod = _ada(c, w_ada[l], b_ada[l])
        shift1, scale1, gate1, shift2, scale2, gate2 = (
            mod[:, i * D:(i + 1) * D].reshape(B, 1, D) for i in range(N_MOD))

        x_m, o_pre, a_q, a_k, a_v, gates = _inproj(x, scale1, shift1, norm1_g[l].reshape(1, D),
                                                   _inproj_weight(w_in[l]))
        y_m = _mlstm(x_m, o_pre, gates, conv_w[l], conv_b[l], w_q_blk[l], w_k_blk[l], w_v_blk[l],
                     b_igate[l], b_fgate[l], mlstm_norm_g[l], mlstm_skip[l])
        y_a = _attn(a_q, a_k, a_v, q_norm_g[l], k_norm_g[l], rel_bias)
        x1 = _outproj(y_m, y_a, x, gate1, w_out[l])

        xin, pos, aff = _route(x1, scale2, shift2, norm2_g[l].reshape(1, D), w_router[l], b_router[l])
        NE, _, cap, _ = xin.shape
        y = _experts(xin.reshape(NE, B * cap, D), w_gate[l], w_up[l], w_down[l]).reshape(NE, B, cap, D)
        x = _combine(y, pos.transpose(0, 2, 1), aff.transpose(0, 2, 1), x1, gate2)
    return x
```

```python
import functools
import math

import numpy as np
import jax
import jax.numpy as jnp
from jax import lax
from jax.experimental import pallas as pl
from jax.experimental.pallas import tpu as pltpu

F32 = jnp.float32
BF16 = jnp.bfloat16
I32 = jnp.int32

NORM_EPS = 1e-6
N_MOD = 6
MLSTM_HEADS = 4
MLSTM_HEAD_DIM = 128
MLSTM_QKV_BLOCK = 4
MLSTM_CONV = 5
MLSTM_CHUNK = 128
MLSTM_UNROLL = 4
ATTN_HEADS = 8
ATTN_HEAD_DIM = 64
ATTN_WIDTH = ATTN_HEADS * ATTN_HEAD_DIM
DILATIONS = (1, 4, 16)
ATTN_HALF = 64
ATTN_QBLK = 128
ATTN_KBLK = 256
REL_BUCKETS = 32
REL_MAX_DIST = 1024
N_EXPERTS = 16
EC_CAPACITY_FACTOR = 2
NEG_BIG = -1e30

V7X_LANES = 128
V7X_VMEM_LIMIT = 56 * 1024 * 1024


def _sigmoid(x):
    return 1.0 / (1.0 + jnp.exp(-x))


def _dot(a, b):
    return jnp.dot(a, b, preferred_element_type=F32)


def _dot_nt(a, b):
    return lax.dot_general(a, b, (((1,), (1,)), ((), ())), preferred_element_type=F32)


def _split_bf16(x):
    hi = x.astype(BF16)
    lo = (x - hi.astype(F32)).astype(BF16)
    return hi, lo


def _params(sem, vmem=None):
    return pltpu.CompilerParams(dimension_semantics=sem, vmem_limit_bytes=vmem)


def _ada_kernel(c_ref, w_ref, b_ref, o_ref):
    c = c_ref[...]
    s = c * _sigmoid(c)
    o_ref[...] = jnp.dot(s, w_ref[...], preferred_element_type=F32,
                         precision=lax.Precision.HIGHEST) + b_ref[...]


def _ada(c, w, b):
    B, D = c.shape
    N = w.shape[1]
    tn = 1024
    return pl.pallas_call(
        _ada_kernel,
        grid=(N // tn,),
        in_specs=[pl.BlockSpec((B, D), lambda j: (0, 0)),
                  pl.BlockSpec((D, tn), lambda j: (0, j)),
                  pl.BlockSpec((1, tn), lambda j: (0, j))],
        out_specs=pl.BlockSpec((B, tn), lambda j: (0, j)),
        out_shape=jax.ShapeDtypeStruct((B, N), F32),
        compiler_params=_params(("arbitrary",)),
        name="ada",
    )(c, w, b.reshape(1, N))


def _modulated_norm(x, g, scale, shift):
    ms = jnp.mean(x * x, axis=-1, keepdims=True)
    y = x * lax.rsqrt(ms + NORM_EPS) * g
    return y * (1.0 + scale) + shift


def _inproj_kernel(x_ref, sc_ref, sh_ref, g_ref, w_ref, hm_ref, qg_ref, kg_ref,
                   xm_ref, op_ref, q_ref, k_ref, v_ref, gt_ref):
    h = _modulated_norm(x_ref[0], g_ref[...], sc_ref[0], sh_ref[0]).astype(BF16)
    col = lambda i: _dot(h, w_ref[:, 512 * i:512 * (i + 1)])
    xm_ref[0] = col(0)
    op_ref[0] = col(1)

    def head_norm(t, g):
        ms = _dot((t * t).astype(BF16), hm_ref[...])
        return (t * lax.rsqrt(ms + NORM_EPS) * g).astype(BF16)

    q_ref[0] = head_norm(col(2), qg_ref[...])
    k_ref[0] = head_norm(col(3), kg_ref[...])
    v_ref[0] = col(4).astype(BF16)
    gt_ref[0] = _dot(h, w_ref[:, 2560:2688])[:, :16]


def _inproj(x, scale, shift, g, w, q_norm_g, k_norm_g):
    B, S, D = x.shape
    tm = 512
    W = ATTN_WIDTH
    hid = jnp.arange(W) // ATTN_HEAD_DIM
    head_mean = jnp.where(hid[:, None] == hid[None, :], 1.0 / ATTN_HEAD_DIM, 0.0).astype(BF16)
    qg = jnp.tile(q_norm_g, ATTN_HEADS).reshape(1, W) * (1.0 / math.sqrt(ATTN_HEAD_DIM))
    kg = jnp.tile(k_norm_g, ATTN_HEADS).reshape(1, W)
    row = lambda b, i: (b, i, 0)
    vec = lambda b, i: (b, 0, 0)
    const = lambda b, i: (0, 0)
    outs = ([jax.ShapeDtypeStruct((B, S, 512), F32)] * 2 + [jax.ShapeDtypeStruct((B, S, W), BF16)] * 3
            + [jax.ShapeDtypeStruct((B, S, 16), F32)])
    return pl.pallas_call(
        _inproj_kernel,
        grid=(B, S // tm),
        in_specs=[pl.BlockSpec((1, tm, D), row),
                  pl.BlockSpec((1, 1, D), vec),
                  pl.BlockSpec((1, 1, D), vec),
                  pl.BlockSpec((1, D), const),
                  pl.BlockSpec(w.shape, const),
                  pl.BlockSpec((W, W), const),
                  pl.BlockSpec((1, W), const),
                  pl.BlockSpec((1, W), const)],
        out_specs=[pl.BlockSpec((1, tm, 512), row)] * 5 + [pl.BlockSpec((1, tm, 16), row)],
        out_shape=outs,
        compiler_params=_params(("parallel", "arbitrary"), V7X_VMEM_LIMIT),
        name="inproj",
    )(x, scale, shift, g, w, head_mean, qg, kg)


def _chunk_scan(x, op, reverse):
    n = x.shape[1]
    idx = lax.broadcasted_iota(I32, x.shape, 1) & (MLSTM_CHUNK - 1)
    s = 1
    while s < MLSTM_CHUNK:
        if reverse:
            r = pltpu.roll(x, n - s, 1)
            x = jnp.where(idx < MLSTM_CHUNK - s, op(x, r), x)
        else:
            r = pltpu.roll(x, s, 1)
            x = jnp.where(idx >= s, op(x, r), x)
        s *= 2
    return x


def _log_sigmoid(x):
    return jnp.minimum(x, 0.0) - jnp.log(1.0 + jnp.exp(-jnp.abs(x)))


def _mlstm_kernel(xm_ref, op_ref, grow_ref, brow_ref, cw_ref, cb_ref,
                  wq_ref, wkt_ref, wv_ref, ng_ref, sk_ref, y_ref,
                  q_s, kt_s, va_s, xc_s, hf_s, hb_s, row_s, col_s):
    S = xm_ref.shape[1]
    L = MLSTM_CHUNK
    NC = S // L
    E = MLSTM_HEAD_DIM
    xm = xm_ref[0]

    rows = lax.broadcasted_iota(I32, xm.shape, 0)
    half = MLSTM_CONV // 2
    conv = xm * cw_ref[half:half + 1, :]
    for j in range(MLSTM_CONV):
        off = j - half
        if off == 0:
            continue
        shifted = pltpu.roll(xm, (-off) % S, 0)
        valid = rows >= -off if off < 0 else rows < S - off
        conv = conv + jnp.where(valid, shifted, 0.0) * cw_ref[j:j + 1, :]
    conv = conv + cb_ref[...]
    xc = conv * _sigmoid(conv)
    xc_s[...] = xc

    xcb = xc.astype(BF16)
    q_s[...] = _dot(xcb, wq_ref[0]).astype(BF16)
    kt_s[...] = _dot_nt(wkt_ref[0], xcb) * (1.0 / math.sqrt(E))
    va_s[:, :E] = _dot(xm.astype(BF16), wv_ref[0]).astype(BF16)
    va_s[:, E:] = jnp.ones((S, E), BF16)

    gr = grow_ref[0, 0] + brow_ref[0]
    kind = lax.broadcasted_iota(I32, gr.shape, 0)
    gr = jnp.where((kind & 1) == 1, _log_sigmoid(gr), gr)
    b_f = _chunk_scan(gr[1:2], jnp.add, False)
    b_b = _chunk_scan(gr[3:4], jnp.add, True)
    u_f = gr[0:1] - b_f
    u_b = gr[2:3] - b_b
    m_f = _chunk_scan(u_f, jnp.maximum, False)
    m_b = _chunk_scan(u_b, jnp.maximum, True)
    stats = jnp.concatenate([b_f, m_f, b_b, m_b, u_f, u_b, gr[1:2], gr[3:4]], axis=0)
    row_s[...] = stats
    col_s[...] = stats.T

    ti = lax.broadcasted_iota(I32, (L, L), 0)
    tj = lax.broadcasted_iota(I32, (L, L), 1)
    causal = (tj <= ti, tj >= ti)

    def chunk(c, dirn, state):
        Ca, m = state
        r0 = pl.multiple_of(c * L, L)
        qb = q_s[pl.ds(r0, L), :]
        kt = kt_s[:, pl.ds(r0, L)]
        va = va_s[pl.ds(r0, L), :]
        cc = col_s[pl.ds(r0, L), :]
        rr = row_s[:, pl.ds(r0, L)]
        b_c, m_c = cc[:, 2 * dirn:2 * dirn + 1], cc[:, 2 * dirn + 1:2 * dirn + 2]
        u_r = rr[4 + dirn:5 + dirn, :]
        g = jnp.sum(rr[6 + dirn:7 + dirn, :], axis=1, keepdims=True)

        p = jnp.exp(jnp.where(causal[dirn], u_r - m_c, NEG_BIG)) * _dot(qb, kt.astype(BF16))
        y_loc = _dot(p.astype(BF16), va)
        m_loc = g + jnp.max(u_r, axis=1, keepdims=True)
        d_state = _dot((kt * jnp.exp(g + u_r - m_loc)).astype(BF16), va)

        x_st = _dot(qb, Ca.astype(BF16))
        mm = jnp.maximum(m, m_c)
        z = jnp.exp(m - mm) * x_st + jnp.exp(m_c - mm) * y_loc
        h = z[:, :E] / jnp.maximum(jnp.abs(z[:, E:]), jnp.exp(-b_c - mm))
        m_new = jnp.maximum(g + m, m_loc)
        Ca_new = jnp.exp(g + m - m_new) * Ca + jnp.exp(m_loc - m_new) * d_state
        return h, (Ca_new, m_new)

    def body(i, carry):
        sf, sb = carry
        for k in range(MLSTM_UNROLL):
            c = i * MLSTM_UNROLL + k
            h_f, sf = chunk(c, 0, sf)
            hf_s[pl.ds(pl.multiple_of(c * L, L), L), :] = h_f
            cb = NC - 1 - c
            h_b, sb = chunk(cb, 1, sb)
            hb_s[pl.ds(pl.multiple_of(cb * L, L), L), :] = h_b
        return sf, sb

    init = (jnp.zeros((E, 2 * E), F32), jnp.zeros((1, 1), F32))
    lax.fori_loop(0, NC // MLSTM_UNROLL, body, (init, init))

    h = hf_s[...] + hb_s[...]
    hn = h * lax.rsqrt(jnp.mean(h * h, axis=-1, keepdims=True) + NORM_EPS) * ng_ref[...]
    y = (hn + sk_ref[...] * xc_s[...]) * _sigmoid(op_ref[0])
    y_ref[0] = y.astype(y_ref.dtype)


def _blockdiag_dense(w_blk):
    nblk = w_blk.shape[0]
    per_head = nblk // MLSTM_HEADS
    w = w_blk.reshape(MLSTM_HEADS, per_head, MLSTM_QKV_BLOCK, MLSTM_QKV_BLOCK)
    eye = jnp.eye(per_head, dtype=w.dtype)
    dense = jnp.einsum('hgij,gk->hgikj', w, eye)
    return dense.reshape(MLSTM_HEADS, MLSTM_HEAD_DIM, MLSTM_HEAD_DIM)


def _mlstm(x_m, o_pre, gates, conv_w, conv_b, w_q_blk, w_k_blk, w_v_blk, b_igate, b_fgate, norm_g, skip):
    B, S, W = x_m.shape
    H, E = MLSTM_HEADS, MLSTM_HEAD_DIM
    assert S % (MLSTM_CHUNK * MLSTM_UNROLL) == 0
    grow = gates.reshape(B, S, 4, H).transpose(0, 3, 2, 1)
    bk = jnp.stack([b_igate[0], b_fgate[0], b_igate[1], b_fgate[1]], axis=0)
    brow = bk.T.reshape(H, 4, 1)
    wq = _blockdiag_dense(w_q_blk).astype(BF16)
    wkt = _blockdiag_dense(w_k_blk).transpose(0, 2, 1).astype(BF16)
    wv = _blockdiag_dense(w_v_blk).astype(BF16)
    headcol = lambda b, h: (b, 0, h)
    perhead = lambda b, h: (h, 0, 0)
    lanes = lambda b, h: (0, h)
    return pl.pallas_call(
        _mlstm_kernel,
        grid=(B, H),
        in_specs=[pl.BlockSpec((1, S, E), headcol),
                  pl.BlockSpec((1, S, E), headcol),
                  pl.BlockSpec((1, 1, 4, S), lambda b, h: (b, h, 0, 0)),
                  pl.BlockSpec((1, 4, 1), perhead),
                  pl.BlockSpec((MLSTM_CONV, E), lanes),
                  pl.BlockSpec((1, E), lanes),
                  pl.BlockSpec((1, E, E), perhead),
                  pl.BlockSpec((1, E, E), perhead),
                  pl.BlockSpec((1, E, E), perhead),
                  pl.BlockSpec((1, E), lanes),
                  pl.BlockSpec((1, E), lanes)],
        out_specs=pl.BlockSpec((1, S, E), headcol),
        out_shape=jax.ShapeDtypeStruct((B, S, W), BF16),
        scratch_shapes=[pltpu.VMEM((S, E), BF16), pltpu.VMEM((E, S), F32), pltpu.VMEM((S, 2 * E), BF16),
                        pltpu.VMEM((S, E), F32), pltpu.VMEM((S, E), F32), pltpu.VMEM((S, E), F32),
                        pltpu.VMEM((8, S), F32), pltpu.VMEM((S, 8), F32)],
        compiler_params=_params(("parallel", "arbitrary"), V7X_VMEM_LIMIT),
        name="mlstm",
    )(x_m, o_pre, grow, brow, conv_w, conv_b.reshape(1, W), wq, wkt, wv,
      norm_g.reshape(1, W), skip.reshape(1, W))


def _t5_bucket_static(rel):
    half = REL_BUCKETS // 2
    exact = half // 2
    n = np.abs(rel)
    log_ratio = (np.log(np.maximum(n, 1).astype(np.float32) / np.float32(exact))
                 / np.float32(math.log(REL_MAX_DIST / exact)))
    large = np.minimum(exact + (log_ratio * np.float32(half - exact)).astype(np.int32), half - 1)
    return np.where(rel > 0, half, 0) + np.where(n < exact, n, large)


def _attn_bias_tables(rel_bias, d):
    Q, K = ATTN_QBLK, ATTN_KBLK
    n = Q + K
    diag = np.arange(n)[None, :] - (Q - 1) - ATTN_HALF * np.arange(3)[:, None]
    valid = np.abs(diag) <= ATTN_HALF
    onehot = (_t5_bucket_static(diag * d)[..., None] == np.arange(REL_BUCKETS)) & valid[..., None]
    w = jnp.einsum('vnb,bh->vhn', jnp.asarray(onehot, F32), rel_bias.astype(F32),
                   precision=lax.Precision.HIGHEST)
    w = jnp.where(jnp.asarray(valid)[:, None, :], w, NEG_BIG)
    skew = jnp.tile(w, (1, 1, Q + 1))[..., :Q * (n + 1)].reshape(3, ATTN_HEADS, Q, n + 1)
    t = jnp.flip(skew, axis=2)[..., :K]
    t = t.reshape(3, ATTN_HEADS // 2, 2, Q, K).transpose(1, 0, 2, 3, 4)
    return t.reshape(ATTN_HEADS // 2, 6, Q, K)


def _attn_kernel(q_ref, k_ref, v_ref, bias_ref, o_ref, lse_ref):
    L = q_ref.shape[1]
    G = q_ref.shape[2] // V7X_LANES
    QB = ATTN_QBLK
    nqb = L // QB
    nk = min(L, ATTN_KBLK)
    lane = lax.broadcasted_iota(I32, (1, V7X_LANES), 1)
    in_head = (lane < ATTN_HEAD_DIM, lane >= ATTN_HEAD_DIM)
    for g in range(G):
        lanes = slice(g * V7X_LANES, (g + 1) * V7X_LANES)
        pair = 0 if bias_ref.shape[0] == 1 else g % bias_ref.shape[0]
        for qb in range(nqb):
            if nqb == 1 or qb == 0:
                k0, variant = 0, 0
            elif qb == nqb - 1:
                k0, variant = L - nk, 2
            else:
                k0, variant = qb * QB - ATTN_HALF, 1
            rows = slice(qb * QB, (qb + 1) * QB)
            qt = q_ref[0, rows, lanes]
            kt = k_ref[0, k0:k0 + nk, lanes]
            vt = v_ref[0, k0:k0 + nk, lanes]
            acc, mx = [], []
            for a in range(2):
                qa = jnp.where(in_head[a], qt, jnp.zeros_like(qt))
                s = _dot_nt(qa, kt) + bias_ref[pair, variant * 2 + a][:, :nk]
                m = jnp.max(s, axis=1, keepdims=True)
                e = jnp.exp(s - m).astype(BF16)
                acc.append(_dot(e, jnp.where(in_head[a], vt, jnp.ones_like(vt))))
                mx.append(m)
            num = jnp.where(in_head[0], acc[0], acc[1])
            den = pltpu.roll(jnp.where(in_head[0], acc[1], acc[0]), ATTN_HEAD_DIM, 1)
            o_ref[0, rows, lanes] = (num / den).astype(o_ref.dtype)
            lse_ref[0, rows, lanes] = jnp.where(in_head[0], mx[0], mx[1]) + jnp.log(den)


def _attn_pattern(q, k, v, bias, d):
    B, S, W = q.shape
    L = S // d
    view = lambda t: t.reshape(B, L, d * W)
    lane_blocks = 4
    lw = d * W // lane_blocks
    blk = pl.BlockSpec((1, L, lw), lambda b, j: (b, 0, j))
    if lw == V7X_LANES:
        bias_spec = pl.BlockSpec((1,) + bias.shape[1:], lambda b, j: (j, 0, 0, 0))
    else:
        bias_spec = pl.BlockSpec(bias.shape, lambda b, j: (0, 0, 0, 0))
    o, lse = pl.pallas_call(
        _attn_kernel,
        grid=(B, lane_blocks),
        in_specs=[blk, blk, blk, bias_spec],
        out_specs=[blk, blk],
        out_shape=[jax.ShapeDtypeStruct((B, L, d * W), BF16), jax.ShapeDtypeStruct((B, L, d * W), F32)],
        compiler_params=_params(("parallel", "arbitrary"), V7X_VMEM_LIMIT),
        name=f"attn_d{d}",
    )(view(q), view(k), view(v), bias)
    return o.reshape(B, S, W), lse.reshape(B, S, W)


def _outproj_kernel(ym_ref, o1_ref, o2_ref, o3_ref, l1_ref, l2_ref, l3_ref, x_ref, g1_ref, w1_ref, w2_ref, o_ref):
    ls = (l1_ref[0], l2_ref[0], l3_ref[0])
    mx = jnp.maximum(jnp.maximum(ls[0], ls[1]), ls[2])
    num = jnp.zeros(mx.shape, F32)
    den = jnp.zeros(mx.shape, F32)
    for l, o in zip(ls, (o1_ref, o2_ref, o3_ref)):
        w = jnp.exp(l - mx)
        num = num + w * o[0].astype(F32)
        den = den + w
    ya = (num / den).astype(BF16)
    mix = _dot(ym_ref[0], w1_ref[...]) + _dot(ya, w2_ref[...])
    o_ref[0] = x_ref[0] + g1_ref[0] * mix


def _outproj(y_m, attn_o, attn_lse, x, gate1, w_out):
    B, S, D = x.shape
    Wm = y_m.shape[-1]
    Wa = attn_o[0].shape[-1]
    tm = 512
    w1 = w_out[:Wm].astype(BF16)
    w2 = w_out[Wm:].astype(BF16)
    row = lambda b, i: (b, i, 0)
    const = lambda b, i: (0, 0)
    return pl.pallas_call(
        _outproj_kernel,
        grid=(B, S // tm),
        in_specs=[pl.BlockSpec((1, tm, Wm), row)] + [pl.BlockSpec((1, tm, Wa), row)] * 6
                 + [pl.BlockSpec((1, tm, D), row),
                    pl.BlockSpec((1, 1, D), lambda b, i: (b, 0, 0)),
                    pl.BlockSpec(w1.shape, const),
                    pl.BlockSpec(w2.shape, const)],
        out_specs=pl.BlockSpec((1, tm, D), row),
        out_shape=jax.ShapeDtypeStruct((B, S, D), F32),
        compiler_params=_params(("parallel", "arbitrary"), V7X_VMEM_LIMIT),
        name="outproj",
    )(y_m, *attn_o, *attn_lse, x, gate1, w1, w2)


def _route_kernel(x_ref, sc_ref, sh_ref, g_ref, whi_ref, wlo_ref, br_ref, tri_ref,
                  xin_ref, pos_ref, gate_ref, h_s, pos_s, *, cap):
    S = x_ref.shape[1]
    NE = N_EXPERTS
    e = pl.program_id(1)

    @pl.when(e == 0)
    def _():
        h = _modulated_norm(x_ref[0], g_ref[...], sc_ref[0], sh_ref[0])
        hi, lo = _split_bf16(h)
        h_s[...] = hi
        logits = _dot(hi, whi_ref[...]) + _dot(lo, whi_ref[...]) + _dot(hi, wlo_ref[...])
        lt = logits.T[:NE, :] + br_ref[...]
        ex = jnp.exp(lt - jnp.max(lt, axis=0, keepdims=True))
        aff = ex / jnp.sum(ex, axis=0, keepdims=True)
        gate_ref[0] = aff

        bits = pltpu.bitcast(aff, I32)

        def search(i, v):
            cand = v | lax.shift_left(jnp.int32(1), 30 - i)
            cnt = jnp.sum((bits >= cand).astype(F32), axis=1, keepdims=True)
            return jnp.where(cnt >= cap, cand, v)

        thr = lax.fori_loop(0, 31, search, jnp.zeros((NE, 1), I32))
        gt = (bits > thr).astype(F32)
        eq = (bits == thr).astype(F32)
        need = cap - jnp.sum(gt, axis=1, keepdims=True)

        def prefix_count(mask):
            off = jnp.zeros((NE, 1), F32)
            parts = []
            for j in range(S // V7X_LANES):
                t = mask[:, j * V7X_LANES:(j + 1) * V7X_LANES]
                parts.append(_dot(t.astype(BF16), tri_ref[...]) + off)
                off = off + jnp.sum(t, axis=1, keepdims=True)
            return jnp.concatenate(parts, axis=1)

        sel = jnp.maximum(gt, jnp.where(prefix_count(eq) < need, eq, 0.0))
        pos = jnp.where(sel > 0.0, prefix_count(sel), -1.0).astype(I32)
        pos_s[...] = pos
        pos_ref[0] = pos

    prow = pos_s[pl.ds(e, 1), :]
    slot = lax.broadcasted_iota(I32, (cap, S), 0)
    onehot = jnp.where(prow == slot, 1.0, 0.0).astype(BF16)
    xin_ref[0, 0] = _dot(onehot, h_s[...]).astype(xin_ref.dtype)


def _route(x1, scale, shift, g, w_router, b_router):
    B, S, D = x1.shape
    NE = N_EXPERTS
    cap = (EC_CAPACITY_FACTOR * S) // NE
    wpad = jnp.zeros((D, V7X_LANES), F32).at[:, :NE].set(w_router)
    whi, wlo = _split_bf16(wpad)
    ti = jnp.arange(V7X_LANES)
    tri = (ti[:, None] < ti[None, :]).astype(BF16)
    vec = lambda b, e: (b, 0, 0)
    const = lambda b, e: (0, 0)
    return pl.pallas_call(
        functools.partial(_route_kernel, cap=cap),
        grid=(B, NE),
        in_specs=[pl.BlockSpec((1, S, D), vec),
                  pl.BlockSpec((1, 1, D), vec),
                  pl.BlockSpec((1, 1, D), vec),
                  pl.BlockSpec((1, D), const),
                  pl.BlockSpec((D, V7X_LANES), const),
                  pl.BlockSpec((D, V7X_LANES), const),
                  pl.BlockSpec((NE, 1), const),
                  pl.BlockSpec((V7X_LANES, V7X_LANES), const)],
        out_specs=[pl.BlockSpec((1, 1, cap, D), lambda b, e: (e, b, 0, 0)),
                   pl.BlockSpec((1, NE, S), vec),
                   pl.BlockSpec((1, NE, S), vec)],
        out_shape=[jax.ShapeDtypeStruct((NE, B, cap, D), BF16),
                   jax.ShapeDtypeStruct((B, NE, S), I32),
                   jax.ShapeDtypeStruct((B, NE, S), F32)],
        scratch_shapes=[pltpu.VMEM((S, D), BF16), pltpu.VMEM((NE, S), I32)],
        compiler_params=_params(("parallel", "arbitrary"), V7X_VMEM_LIMIT),
        name="route",
    )(x1, scale, shift, g, whi, wlo, b_router.reshape(NE, 1), tri)


def _expert_kernel(x_ref, wg_ref, wu_ref, wd_ref, o_ref, acc_s, wg_s, wu_s, wd_s, *, row_tile):
    f = pl.program_id(2)
    nf = pl.num_programs(2)
    wg_s[...] = wg_ref[0].astype(BF16)
    wu_s[...] = wu_ref[0].astype(BF16)
    wd_s[...] = wd_ref[0].astype(BF16)
    n_tiles = x_ref.shape[1] // row_tile

    def sweep(first, last):
        def body(i, carry):
            r = pl.ds(pl.multiple_of(i * row_tile, row_tile), row_tile)
            xb = x_ref[0, r, :]
            g = _dot(xb, wg_s[...])
            u = _dot(xb, wu_s[...])
            y = _dot((g * _sigmoid(g) * u).astype(BF16), wd_s[...])
            if not first:
                y = y + acc_s[r, :]
            if last:
                o_ref[0, r, :] = y.astype(o_ref.dtype)
            else:
                acc_s[r, :] = y
            return carry
        lax.fori_loop(0, n_tiles, body, 0)

    @pl.when(f == 0)
    def _():
        sweep(True, False)

    @pl.when(jnp.logical_and(f > 0, f < nf - 1))
    def _():
        sweep(False, False)

    @pl.when(f == nf - 1)
    def _():
        sweep(False, True)


def _experts(xin, w_gate, w_up, w_down):
    NE, R, D = xin.shape
    F = w_gate.shape[-1]
    tr = min(R, 2048)
    tf = 512
    row_tile = min(tr, 512)
    return pl.pallas_call(
        functools.partial(_expert_kernel, row_tile=row_tile),
        grid=(NE, R // tr, F // tf),
        in_specs=[pl.BlockSpec((1, tr, D), lambda e, r, f: (e, r, 0)),
                  pl.BlockSpec((1, D, tf), lambda e, r, f: (e, 0, f)),
                  pl.BlockSpec((1, D, tf), lambda e, r, f: (e, 0, f)),
                  pl.BlockSpec((1, tf, D), lambda e, r, f: (e, f, 0))],
        out_specs=pl.BlockSpec((1, tr, D), lambda e, r, f: (e, r, 0)),
        out_shape=jax.ShapeDtypeStruct((NE, R, D), BF16),
        scratch_shapes=[pltpu.VMEM((tr, D), F32), pltpu.VMEM((D, tf), BF16),
                        pltpu.VMEM((D, tf), BF16), pltpu.VMEM((tf, D), BF16)],
        compiler_params=_params(("parallel", "parallel", "arbitrary"), V7X_VMEM_LIMIT),
        name="experts",
    )(xin, w_gate, w_up, w_down)


def _combine_kernel(y_ref, pos_ref, gate_ref, x_ref, g2_ref, o_ref):
    ts = x_ref.shape[1]
    cap = y_ref.shape[2]
    pos = pos_ref[0]
    gate = gate_ref[0]
    slot = lax.broadcasted_iota(I32, (ts, cap), 1)
    acc = jnp.zeros((ts, x_ref.shape[2]), F32)
    for e in range(N_EXPERTS):
        scatter = jnp.where(pos[:, e:e + 1] == slot, gate[:, e:e + 1], 0.0).astype(BF16)
        acc = acc + _dot(scatter, y_ref[e, 0])
    o_ref[0] = x_ref[0] + g2_ref[0] * acc


def _combine(y, pos_t, gate_t, x1, gate2):
    B, S, D = x1.shape
    NE, _, cap, _ = y.shape
    ts = 512
    row = lambda b, i: (b, i, 0)
    return pl.pallas_call(
        _combine_kernel,
        grid=(B, S // ts),
        in_specs=[pl.BlockSpec((NE, 1, cap, D), lambda b, i: (0, b, 0, 0)),
                  pl.BlockSpec((1, ts, NE), row),
                  pl.BlockSpec((1, ts, NE), row),
                  pl.BlockSpec((1, ts, D), row),
                  pl.BlockSpec((1, 1, D), lambda b, i: (b, 0, 0))],
        out_specs=pl.BlockSpec((1, ts, D), row),
        out_shape=jax.ShapeDtypeStruct((B, S, D), F32),
        compiler_params=_params(("parallel", "arbitrary"), V7X_VMEM_LIMIT),
        name="combine",
    )(y, pos_t, gate_t, x1, gate2)


def _inproj_weight(w_in):
    D = w_in.shape[0]
    main = jnp.concatenate([w_in[:, :1024], w_in[:, 1040:2576], w_in[:, 1024:1040],
                            jnp.zeros((D, V7X_LANES - 16), w_in.dtype)], axis=1)
    return main.astype(BF16)


def kernel(x, c, w_ada, b_ada, norm1_g, w_in, conv_w, conv_b, w_q_blk, w_k_blk, w_v_blk, b_igate, b_fgate,
           mlstm_norm_g, mlstm_skip, q_norm_g, k_norm_g, rel_bias, w_out, norm2_g, w_router, b_router,
           w_gate, w_up, w_down):
    B, S, D = x.shape
    depth = w_ada.shape[0]
    biases = [_attn_bias_tables(rel_bias, d) for d in DILATIONS]
    for l in range(depth):
        mod = _ada(c, w_ada[l], b_ada[l])
        shift1, scale1, gate1, shift2, scale2, gate2 = (
            mod[:, i * D:(i + 1) * D].reshape(B, 1, D) for i in range(N_MOD))

        x_m, o_pre, a_q, a_k, a_v, gates = _inproj(x, scale1, shift1, norm1_g[l].reshape(1, D),
                                                   _inproj_weight(w_in[l]), q_norm_g[l], k_norm_g[l])
        y_m = _mlstm(x_m, o_pre, gates, conv_w[l], conv_b[l], w_q_blk[l], w_k_blk[l], w_v_blk[l],
                     b_igate[l], b_fgate[l], mlstm_norm_g[l], mlstm_skip[l])
        attn = [_attn_pattern(a_q, a_k, a_v, bias, d) for bias, d in zip(biases, DILATIONS)]
        x1 = _outproj(y_m, [o for o, _ in attn], [s for _, s in attn], x, gate1, w_out[l])

        xin, pos, aff = _route(x1, scale2, shift2, norm2_g[l].reshape(1, D), w_router[l], b_router[l])
        NE, _, cap, _ = xin.shape
        y = _experts(xin.reshape(NE, B * cap, D), w_gate[l], w_up[l], w_down[l]).reshape(NE, B, cap, D)
        x = _combine(y, pos.transpose(0, 2, 1), aff.transpose(0, 2, 1), x1, gate2)
    return x
```

```python
import functools
import math

import numpy as np
import jax
import jax.numpy as jnp
from jax import lax
from jax.experimental import pallas as pl
from jax.experimental.pallas import tpu as pltpu

F32 = jnp.float32
BF16 = jnp.bfloat16
I32 = jnp.int32

NORM_EPS = 1e-6
N_MOD = 6
MLSTM_HEADS = 4
MLSTM_HEAD_DIM = 128
MLSTM_QKV_BLOCK = 4
MLSTM_CONV = 5
MLSTM_CHUNK = 128
MLSTM_UNROLL = 8
ATTN_HEADS = 8
ATTN_HEAD_DIM = 64
ATTN_WIDTH = ATTN_HEADS * ATTN_HEAD_DIM
DILATIONS = (1, 4, 16)
ATTN_HALF = 64
ATTN_QBLK = 128
ATTN_KBLK = 256
ATTN_GROUP = 2
LOG2E = math.log2(math.e)
REL_BUCKETS = 32
REL_MAX_DIST = 1024
N_EXPERTS = 16
EC_CAPACITY_FACTOR = 2
NEG_BIG = -1e30

V7X_LANES = 128
V7X_VMEM_LIMIT = 56 * 1024 * 1024


def _sigmoid(x):
    return 1.0 / (1.0 + jnp.exp(-x))


def _dot(a, b):
    return jnp.dot(a, b, preferred_element_type=F32)


def _dot_nt(a, b):
    return lax.dot_general(a, b, (((1,), (1,)), ((), ())), preferred_element_type=F32)


def _split_bf16(x):
    hi = x.astype(BF16)
    lo = (x - hi.astype(F32)).astype(BF16)
    return hi, lo


def _params(sem, vmem=None):
    return pltpu.CompilerParams(dimension_semantics=sem, vmem_limit_bytes=vmem)


def _ada_kernel(c_ref, w_ref, b_ref, o_ref):
    c = c_ref[...]
    s = c * _sigmoid(c)
    o_ref[...] = jnp.dot(s, w_ref[...], preferred_element_type=F32,
                         precision=lax.Precision.HIGHEST) + b_ref[...]


def _ada(c, w, b):
    B, D = c.shape
    N = w.shape[1]
    tn = 1024
    return pl.pallas_call(
        _ada_kernel,
        grid=(N // tn,),
        in_specs=[pl.BlockSpec((B, D), lambda j: (0, 0)),
                  pl.BlockSpec((D, tn), lambda j: (0, j)),
                  pl.BlockSpec((1, tn), lambda j: (0, j))],
        out_specs=pl.BlockSpec((B, tn), lambda j: (0, j)),
        out_shape=jax.ShapeDtypeStruct((B, N), F32),
        compiler_params=_params(("arbitrary",)),
        name="ada",
    )(c, w, b.reshape(1, N))


def _modulated_norm(x, g, scale, shift):
    ms = jnp.mean(x * x, axis=-1, keepdims=True)
    y = x * lax.rsqrt(ms + NORM_EPS) * g
    return y * (1.0 + scale) + shift


def _inproj_kernel(x_ref, sc_ref, sh_ref, g_ref, w_ref, hm_ref, qg_ref, kg_ref,
                   xm_ref, op_ref, q_ref, k_ref, v_ref, gt_ref):
    h = _modulated_norm(x_ref[0], g_ref[...], sc_ref[0], sh_ref[0]).astype(BF16)
    col = lambda i: _dot(h, w_ref[:, 512 * i:512 * (i + 1)])
    xm_ref[0] = col(0)
    op_ref[0] = col(1)

    def head_norm(t, g):
        ms = _dot((t * t).astype(BF16), hm_ref[...])
        return t * lax.rsqrt(ms + NORM_EPS) * g

    q_ref[0] = head_norm(col(2), qg_ref[...])
    k_ref[0] = head_norm(col(3), kg_ref[...])
    v_ref[0] = col(4)
    gt_ref[0] = _dot(h, w_ref[:, 2560:2688])[:, :16]


def _inproj(x, scale, shift, g, w, q_norm_g, k_norm_g):
    B, S, D = x.shape
    tm = 512
    W = ATTN_WIDTH
    hid = jnp.arange(W) // ATTN_HEAD_DIM
    head_mean = jnp.where(hid[:, None] == hid[None, :], 1.0 / ATTN_HEAD_DIM, 0.0).astype(BF16)
    qg = jnp.tile(q_norm_g, ATTN_HEADS).reshape(1, W) * (LOG2E / math.sqrt(ATTN_HEAD_DIM))
    kg = jnp.tile(k_norm_g, ATTN_HEADS).reshape(1, W)
    row = lambda b, i: (b, i, 0)
    vec = lambda b, i: (b, 0, 0)
    const = lambda b, i: (0, 0)
    outs = ([jax.ShapeDtypeStruct((B, S, 512), F32)] * 2 + [jax.ShapeDtypeStruct((B, S, W), F32)] * 3
            + [jax.ShapeDtypeStruct((B, S, 16), F32)])
    return pl.pallas_call(
        _inproj_kernel,
        grid=(B, S // tm),
        in_specs=[pl.BlockSpec((1, tm, D), row),
                  pl.BlockSpec((1, 1, D), vec),
                  pl.BlockSpec((1, 1, D), vec),
                  pl.BlockSpec((1, D), const),
                  pl.BlockSpec(w.shape, const),
                  pl.BlockSpec((W, W), const),
                  pl.BlockSpec((1, W), const),
                  pl.BlockSpec((1, W), const)],
        out_specs=[pl.BlockSpec((1, tm, 512), row)] * 5 + [pl.BlockSpec((1, tm, 16), row)],
        out_shape=outs,
        compiler_params=_params(("parallel", "arbitrary"), V7X_VMEM_LIMIT),
        name="inproj",
    )(x, scale, shift, g, w, head_mean, qg, kg)


def _chunk_scan(x, op, reverse):
    n = x.shape[1]
    idx = lax.broadcasted_iota(I32, x.shape, 1) & (MLSTM_CHUNK - 1)
    s = 1
    while s < MLSTM_CHUNK:
        if reverse:
            r = pltpu.roll(x, n - s, 1)
            x = jnp.where(idx < MLSTM_CHUNK - s, op(x, r), x)
        else:
            r = pltpu.roll(x, s, 1)
            x = jnp.where(idx >= s, op(x, r), x)
        s *= 2
    return x


def _log_sigmoid(x):
    return jnp.minimum(x, 0.0) - jnp.log(1.0 + jnp.exp(-jnp.abs(x)))


def _mlstm_kernel(xm_ref, op_ref, grow_ref, brow_ref, cw_ref, cb_ref,
                  wq_ref, wkt_ref, wv_ref, ng_ref, sk_ref, y_ref,
                  q_s, kt_s, va_s, xc_s, hf_s, hb_s, row_s, col_s):
    S = xm_ref.shape[1]
    L = MLSTM_CHUNK
    NC = S // L
    E = MLSTM_HEAD_DIM
    xm = xm_ref[0]

    rows = lax.broadcasted_iota(I32, xm.shape, 0)
    half = MLSTM_CONV // 2
    conv = xm * cw_ref[half:half + 1, :]
    for j in range(MLSTM_CONV):
        off = j - half
        if off == 0:
            continue
        shifted = pltpu.roll(xm, (-off) % S, 0)
        valid = rows >= -off if off < 0 else rows < S - off
        conv = conv + jnp.where(valid, shifted, 0.0) * cw_ref[j:j + 1, :]
    conv = conv + cb_ref[...]
    xc = conv * _sigmoid(conv)
    xc_s[...] = xc

    xcb = xc.astype(BF16)
    q_s[...] = _dot(xcb, wq_ref[0]).astype(BF16)
    kt_s[...] = _dot_nt(wkt_ref[0], xcb) * (1.0 / math.sqrt(E))
    va_s[:, :E] = _dot(xm.astype(BF16), wv_ref[0]).astype(BF16)
    va_s[:, E:] = jnp.ones((S, E), BF16)

    gr = grow_ref[0, 0] + brow_ref[0]
    kind = lax.broadcasted_iota(I32, gr.shape, 0)
    gr = jnp.where((kind & 1) == 1, _log_sigmoid(gr), gr)
    b_f = _chunk_scan(gr[1:2], jnp.add, False)
    b_b = _chunk_scan(gr[3:4], jnp.add, True)
    u_f = gr[0:1] - b_f
    u_b = gr[2:3] - b_b
    m_f = _chunk_scan(u_f, jnp.maximum, False)
    m_b = _chunk_scan(u_b, jnp.maximum, True)
    row_s[...] = jnp.concatenate([u_f, u_b, gr[1:2], gr[3:4]], axis=0)
    for k, stat in enumerate((b_f, m_f, b_b, m_b)):
        for c in range(NC):
            col_s[k, c * L:(c + 1) * L, :] = jnp.broadcast_to(stat[:, c * L:(c + 1) * L], (L, L)).T

    ti = lax.broadcasted_iota(I32, (L, L), 0)
    tj = lax.broadcasted_iota(I32, (L, L), 1)
    causal = (tj <= ti, tj >= ti)

    def local_part(c, dirn):
        r0 = pl.multiple_of(c * L, L)
        qb = q_s[pl.ds(r0, L), :]
        kt = kt_s[:, pl.ds(r0, L)]
        va = va_s[pl.ds(r0, L), :]
        rr = row_s[:, pl.ds(r0, L)]
        u_r = rr[dirn:dirn + 1, :]
        g = jnp.sum(rr[2 + dirn:3 + dirn, :], axis=1, keepdims=True)
        m_loc = g + jnp.max(u_r, axis=1, keepdims=True)
        return dict(
            r0=r0, dirn=dirn, qb=qb, va=va, u_r=u_r, g=g, m_loc=m_loc,
            b_c=col_s[2 * dirn, pl.ds(r0, L), :], m_c=col_s[2 * dirn + 1, pl.ds(r0, L), :],
            s=_dot(qb, kt.astype(BF16)),
            d_state=_dot((kt * jnp.exp(g + u_r - m_loc)).astype(BF16), va))

    def intra_part(t):
        p = jnp.exp(jnp.where(causal[t["dirn"]], t["u_r"] - t["m_c"], NEG_BIG)) * t.pop("s")
        t["y_loc"] = _dot(p.astype(BF16), t["va"])

    def state_part(t, state):
        Ca, m = state
        x_st = _dot(t["qb"], Ca.astype(BF16))
        mm = jnp.maximum(m, t["m_c"])
        w_st, w_loc = jnp.exp(m - mm), jnp.exp(t["m_c"] - mm)
        num = w_st * x_st[:, :E] + w_loc * t["y_loc"][:, :E]
        den = w_st * x_st[:, E:] + w_loc * t["y_loc"][:, E:]
        h = num / jnp.maximum(jnp.abs(den), jnp.exp(-t["b_c"] - mm))
        m_new = jnp.maximum(t["g"] + m, t["m_loc"])
        Ca_new = jnp.exp(t["g"] + m - m_new) * Ca + jnp.exp(t["m_loc"] - m_new) * t["d_state"]
        return h, (Ca_new, m_new)

    def body(i, carry):
        states = list(carry)
        out_s = (hf_s, hb_s)
        work = []
        for k in range(MLSTM_UNROLL):
            c = i * MLSTM_UNROLL + k
            work += [local_part(c, 0), local_part(NC - 1 - c, 1)]
        for t in work:
            intra_part(t)
        for t in work:
            h, states[t["dirn"]] = state_part(t, states[t["dirn"]])
            out_s[t["dirn"]][pl.ds(t["r0"], L), :] = h
        return tuple(states)

    init = (jnp.zeros((E, 2 * E), F32), jnp.zeros((1, 1), F32))
    lax.fori_loop(0, NC // MLSTM_UNROLL, body, (init, init))

    h = hf_s[...] + hb_s[...]
    hn = h * lax.rsqrt(jnp.mean(h * h, axis=-1, keepdims=True) + NORM_EPS) * ng_ref[...]
    y = (hn + sk_ref[...] * xc_s[...]) * _sigmoid(op_ref[0])
    y_ref[0] = y.astype(y_ref.dtype)


def _blockdiag_dense(w_blk):
    nblk = w_blk.shape[0]
    per_head = nblk // MLSTM_HEADS
    w = w_blk.reshape(MLSTM_HEADS, per_head, MLSTM_QKV_BLOCK, MLSTM_QKV_BLOCK)
    eye = jnp.eye(per_head, dtype=w.dtype)
    dense = jnp.einsum('hgij,gk->hgikj', w, eye)
    return dense.reshape(MLSTM_HEADS, MLSTM_HEAD_DIM, MLSTM_HEAD_DIM)


def _mlstm(x_m, o_pre, gates, conv_w, conv_b, w_q_blk, w_k_blk, w_v_blk, b_igate, b_fgate, norm_g, skip):
    B, S, W = x_m.shape
    H, E = MLSTM_HEADS, MLSTM_HEAD_DIM
    assert S % (MLSTM_CHUNK * MLSTM_UNROLL) == 0
    grow = gates.reshape(B, S, 4, H).transpose(0, 3, 2, 1)
    bk = jnp.stack([b_igate[0], b_fgate[0], b_igate[1], b_fgate[1]], axis=0)
    brow = bk.T.reshape(H, 4, 1)
    wq = _blockdiag_dense(w_q_blk).astype(BF16)
    wkt = _blockdiag_dense(w_k_blk).transpose(0, 2, 1).astype(BF16)
    wv = _blockdiag_dense(w_v_blk).astype(BF16)
    headcol = lambda b, h: (b, 0, h)
    perhead = lambda b, h: (h, 0, 0)
    lanes = lambda b, h: (0, h)
    return pl.pallas_call(
        _mlstm_kernel,
        grid=(B, H),
        in_specs=[pl.BlockSpec((1, S, E), headcol),
                  pl.BlockSpec((1, S, E), headcol),
                  pl.BlockSpec((1, 1, 4, S), lambda b, h: (b, h, 0, 0)),
                  pl.BlockSpec((1, 4, 1), perhead),
                  pl.BlockSpec((MLSTM_CONV, E), lanes),
                  pl.BlockSpec((1, E), lanes),
                  pl.BlockSpec((1, E, E), perhead),
                  pl.BlockSpec((1, E, E), perhead),
                  pl.BlockSpec((1, E, E), perhead),
                  pl.BlockSpec((1, E), lanes),
                  pl.BlockSpec((1, E), lanes)],
        out_specs=pl.BlockSpec((1, S, E), headcol),
        out_shape=jax.ShapeDtypeStruct((B, S, W), BF16),
        scratch_shapes=[pltpu.VMEM((S, E), BF16), pltpu.VMEM((E, S), F32), pltpu.VMEM((S, 2 * E), BF16),
                        pltpu.VMEM((S, E), F32), pltpu.VMEM((S, E), F32), pltpu.VMEM((S, E), F32),
                        pltpu.VMEM((4, S), F32), pltpu.VMEM((4, S, E), F32)],
        compiler_params=_params(("parallel", "arbitrary"), V7X_VMEM_LIMIT),
        name="mlstm",
    )(x_m, o_pre, grow, brow, conv_w, conv_b.reshape(1, W), wq, wkt, wv,
      norm_g.reshape(1, W), skip.reshape(1, W))


def _t5_bucket_static(rel):
    half = REL_BUCKETS // 2
    exact = half // 2
    n = np.abs(rel)
    log_ratio = (np.log(np.maximum(n, 1).astype(np.float32) / np.float32(exact))
                 / np.float32(math.log(REL_MAX_DIST / exact)))
    large = np.minimum(exact + (log_ratio * np.float32(half - exact)).astype(np.int32), half - 1)
    return np.where(rel > 0, half, 0) + np.where(n < exact, n, large)


def _attn_bias_tables(rel_bias):
    Q, K = ATTN_QBLK, ATTN_KBLK
    n = Q + K
    nv = 3 * len(DILATIONS)
    diag = np.arange(n)[None, :] - (Q - 1) - ATTN_HALF * np.arange(3)[:, None]
    valid = np.tile(np.abs(diag) <= ATTN_HALF, (len(DILATIONS), 1))
    bucket = np.concatenate([_t5_bucket_static(diag * d) for d in DILATIONS], axis=0)
    onehot = (bucket[..., None] == np.arange(REL_BUCKETS)) & valid[..., None]
    w = jnp.einsum('vnb,bh->vhn', jnp.asarray(onehot, F32), rel_bias.astype(F32) * LOG2E,
                   precision=lax.Precision.HIGHEST)
    w = jnp.where(jnp.asarray(valid)[:, None, :], w, NEG_BIG)
    skew = jnp.tile(w, (1, 1, Q + 1))[..., :Q * (n + 1)].reshape(nv, ATTN_HEADS, Q, n + 1)
    t = jnp.flip(skew, axis=2)[..., :K]
    t = t.reshape(nv, ATTN_HEADS // 2, 2, Q, K).transpose(1, 0, 2, 3, 4)
    return t.reshape(ATTN_HEADS // 2, 2 * nv, Q, K)


def _attn_kernel(q_ref, k_ref, v_ref, bias_ref, y_ref, q0_s, q1_s, k_s, v0_s, v1_s, o_s, l_s):
    S = q_ref.shape[1]
    QB, KB = ATTN_QBLK, ATTN_KBLK
    lane = lax.broadcasted_iota(I32, (1, V7X_LANES), 1)
    in_head = (lane < ATTN_HEAD_DIM, lane >= ATTN_HEAD_DIM)
    q_s = (q0_s, q1_s)
    v_s = (v0_s, v1_s)

    def strided(start, size, d):
        return pl.ds(start, size) if d == 1 else pl.ds(start, size, stride=d)

    for p, d in enumerate(DILATIONS):
        L = S // d
        for r in range(d):
            src = strided(r, L, d)
            dst = slice(r * L, (r + 1) * L)
            q = q_ref[0, src, :]
            v = v_ref[0, src, :]
            k_s[p, dst, :] = k_ref[0, src, :].astype(BF16)
            for a in range(2):
                q_s[a][p, dst, :] = jnp.where(in_head[a], q, 0.0).astype(BF16)
                v_s[a][p, dst, :] = jnp.where(in_head[a], v, 1.0).astype(BF16)

    def logits(p, d, r, qb):
        L = S // d
        nqb = L // QB
        nk = min(L, KB)
        if nqb == 1 or qb == 0:
            k0, variant = 0, 0
        elif qb == nqb - 1:
            k0, variant = L - nk, 2
        else:
            k0, variant = qb * QB - ATTN_HALF, 1
        qrows = slice(r * L + qb * QB, r * L + (qb + 1) * QB)
        krows = slice(r * L + k0, r * L + k0 + nk)
        kt = k_s[p, krows, :]
        s = [_dot_nt(q_s[a][p, qrows, :], kt) + bias_ref[0, p * 6 + variant * 2 + a][:, :nk] for a in range(2)]
        return dict(p=p, krows=krows, out_rows=strided(r + d * qb * QB, QB, d), s=s)

    def softmax(t):
        t["m"] = [jnp.max(s, axis=1, keepdims=True) for s in t["s"]]
        t["e"] = [jnp.exp2(s - m).astype(BF16) for s, m in zip(t.pop("s"), t["m"])]

    def outputs(t):
        p = t["p"]
        acc = [_dot(t["e"][a], v_s[a][p, t["krows"], :]) for a in range(2)]
        num = jnp.where(in_head[0], acc[0], acc[1])
        den = pltpu.roll(jnp.where(in_head[0], acc[1], acc[0]), ATTN_HEAD_DIM, 1)
        o_s[p, t["out_rows"], :] = num / den
        l_s[p, t["out_rows"], :] = jnp.where(in_head[0], t["m"][0], t["m"][1]) + jnp.log2(den)

    units = [(p, d, r, qb) for p, d in enumerate(DILATIONS) for r in range(d) for qb in range(S // d // QB)]
    prev = []
    for i in range(0, len(units), ATTN_GROUP):
        cur = [logits(*u) for u in units[i:i + ATTN_GROUP]]
        for t in prev:
            outputs(t)
        for t in cur:
            softmax(t)
        prev = cur
    for t in prev:
        outputs(t)

    mx = jnp.maximum(jnp.maximum(l_s[0], l_s[1]), l_s[2])
    num = jnp.zeros((S, V7X_LANES), F32)
    den = jnp.zeros((S, V7X_LANES), F32)
    for p in range(len(DILATIONS)):
        w = jnp.exp2(l_s[p] - mx)
        num = num + w * o_s[p]
        den = den + w
    y_ref[0] = (num / den).astype(y_ref.dtype)


def _attn(a_q, a_k, a_v, bias):
    B, S, W = a_q.shape
    P = ATTN_HEADS // 2
    NP = len(DILATIONS)
    pair = lambda b, p: (b, 0, p)
    blk = pl.BlockSpec((1, S, V7X_LANES), pair)
    return pl.pallas_call(
        _attn_kernel,
        grid=(B, P),
        in_specs=[blk, blk, blk, pl.BlockSpec((1,) + bias.shape[1:], lambda b, p: (p, 0, 0, 0))],
        out_specs=blk,
        out_shape=jax.ShapeDtypeStruct((B, S, W), BF16),
        scratch_shapes=[pltpu.VMEM((NP, S, V7X_LANES), BF16)] * 5 + [pltpu.VMEM((NP, S, V7X_LANES), F32)] * 2,
        compiler_params=_params(("parallel", "arbitrary"), V7X_VMEM_LIMIT),
        name="attn",
    )(a_q, a_k, a_v, bias)


def _outproj_kernel(ym_ref, ya_ref, x_ref, g1_ref, w1_ref, w2_ref, o_ref):
    mix = _dot(ym_ref[0], w1_ref[...]) + _dot(ya_ref[0], w2_ref[...])
    o_ref[0] = x_ref[0] + g1_ref[0] * mix


def _outproj(y_m, y_a, x, gate1, w_out):
    B, S, D = x.shape
    Wm = y_m.shape[-1]
    tm = 512
    w1 = w_out[:Wm].astype(BF16)
    w2 = w_out[Wm:].astype(BF16)
    row = lambda b, i: (b, i, 0)
    const = lambda b, i: (0, 0)
    return pl.pallas_call(
        _outproj_kernel,
        grid=(B, S // tm),
        in_specs=[pl.BlockSpec((1, tm, Wm), row),
                  pl.BlockSpec((1, tm, y_a.shape[-1]), row),
                  pl.BlockSpec((1, tm, D), row),
                  pl.BlockSpec((1, 1, D), lambda b, i: (b, 0, 0)),
                  pl.BlockSpec(w1.shape, const),
                  pl.BlockSpec(w2.shape, const)],
        out_specs=pl.BlockSpec((1, tm, D), row),
        out_shape=jax.ShapeDtypeStruct((B, S, D), F32),
        compiler_params=_params(("parallel", "arbitrary"), V7X_VMEM_LIMIT),
        name="outproj",
    )(y_m, y_a, x, gate1, w1, w2)


def _route_kernel(x_ref, sc_ref, sh_ref, g_ref, whi_ref, wlo_ref, br_ref, tri_ref,
                  xin_ref, pos_ref, gate_ref, h_s, pos_s, *, cap):
    S = x_ref.shape[1]
    NE = N_EXPERTS
    e = pl.program_id(1)

    @pl.when(e == 0)
    def _():
        h = _modulated_norm(x_ref[0], g_ref[...], sc_ref[0], sh_ref[0])
        hi, lo = _split_bf16(h)
        h_s[...] = hi
        logits = _dot(hi, whi_ref[...]) + _dot(lo, whi_ref[...]) + _dot(hi, wlo_ref[...])
        lt = logits.T[:NE, :] + br_ref[...]
        ex = jnp.exp(lt - jnp.max(lt, axis=0, keepdims=True))
        aff = ex / jnp.sum(ex, axis=0, keepdims=True)
        gate_ref[0] = aff

        bits = pltpu.bitcast(aff, I32)

        def search(i, v):
            cand = v | lax.shift_left(jnp.int32(1), 30 - i)
            cnt = jnp.sum((bits >= cand).astype(F32), axis=1, keepdims=True)
            return jnp.where(cnt >= cap, cand, v)

        thr = lax.fori_loop(0, 31, search, jnp.zeros((NE, 1), I32))
        gt = (bits > thr).astype(F32)
        eq = (bits == thr).astype(F32)
        need = cap - jnp.sum(gt, axis=1, keepdims=True)

        def prefix_count(mask):
            off = jnp.zeros((NE, 1), F32)
            parts = []
            for j in range(S // V7X_LANES):
                t = mask[:, j * V7X_LANES:(j + 1) * V7X_LANES]
                parts.append(_dot(t.astype(BF16), tri_ref[...]) + off)
                off = off + jnp.sum(t, axis=1, keepdims=True)
            return jnp.concatenate(parts, axis=1)

        sel = jnp.maximum(gt, jnp.where(prefix_count(eq) < need, eq, 0.0))
        pos = jnp.where(sel > 0.0, prefix_count(sel), -1.0).astype(I32)
        pos_s[...] = pos
        pos_ref[0] = pos

    prow = pos_s[pl.ds(e, 1), :]
    slot = lax.broadcasted_iota(I32, (cap, S), 0)
    onehot = jnp.where(prow == slot, 1.0, 0.0).astype(BF16)
    xin_ref[0, 0] = _dot(onehot, h_s[...]).astype(xin_ref.dtype)


def _route(x1, scale, shift, g, w_router, b_router):
    B, S, D = x1.shape
    NE = N_EXPERTS
    cap = (EC_CAPACITY_FACTOR * S) // NE
    wpad = jnp.zeros((D, V7X_LANES), F32).at[:, :NE].set(w_router)
    whi, wlo = _split_bf16(wpad)
    ti = jnp.arange(V7X_LANES)
    tri = (ti[:, None] < ti[None, :]).astype(BF16)
    vec = lambda b, e: (b, 0, 0)
    const = lambda b, e: (0, 0)
    return pl.pallas_call(
        functools.partial(_route_kernel, cap=cap),
        grid=(B, NE),
        in_specs=[pl.BlockSpec((1, S, D), vec),
                  pl.BlockSpec((1, 1, D), vec),
                  pl.BlockSpec((1, 1, D), vec),
                  pl.BlockSpec((1, D), const),
                  pl.BlockSpec((D, V7X_LANES), const),
                  pl.BlockSpec((D, V7X_LANES), const),
                  pl.BlockSpec((NE, 1), const),
                  pl.BlockSpec((V7X_LANES, V7X_LANES), const)],
        out_specs=[pl.BlockSpec((1, 1, cap, D), lambda b, e: (e, b, 0, 0)),
                   pl.BlockSpec((1, NE, S), vec),
                   pl.BlockSpec((1, NE, S), vec)],
        out_shape=[jax.ShapeDtypeStruct((NE, B, cap, D), BF16),
                   jax.ShapeDtypeStruct((B, NE, S), I32),
                   jax.ShapeDtypeStruct((B, NE, S), F32)],
        scratch_shapes=[pltpu.VMEM((S, D), BF16), pltpu.VMEM((NE, S), I32)],
        compiler_params=_params(("parallel", "arbitrary"), V7X_VMEM_LIMIT),
        name="route",
    )(x1, scale, shift, g, whi, wlo, b_router.reshape(NE, 1), tri)


def _expert_kernel(x_ref, wg_ref, wu_ref, wd_ref, o_ref, acc_s, wg_s, wu_s, wd_s, *, row_tile):
    f = pl.program_id(2)
    nf = pl.num_programs(2)
    wg_s[...] = wg_ref[0].astype(BF16)
    wu_s[...] = wu_ref[0].astype(BF16)
    wd_s[...] = wd_ref[0].astype(BF16)
    n_tiles = x_ref.shape[1] // row_tile

    def sweep(first, last):
        def body(i, carry):
            r = pl.ds(pl.multiple_of(i * row_tile, row_tile), row_tile)
            xb = x_ref[0, r, :]
            g = _dot(xb, wg_s[...])
            u = _dot(xb, wu_s[...])
            y = _dot((g * _sigmoid(g) * u).astype(BF16), wd_s[...])
            if not first:
                y = y + acc_s[r, :]
            if last:
                o_ref[0, r, :] = y.astype(o_ref.dtype)
            else:
                acc_s[r, :] = y
            return carry
        lax.fori_loop(0, n_tiles, body, 0)

    @pl.when(f == 0)
    def _():
        sweep(True, False)

    @pl.when(jnp.logical_and(f > 0, f < nf - 1))
    def _():
        sweep(False, False)

    @pl.when(f == nf - 1)
    def _():
        sweep(False, True)


def _experts(xin, w_gate, w_up, w_down):
    NE, R, D = xin.shape
    F = w_gate.shape[-1]
    tr = min(R, 2048)
    tf = 512
    row_tile = min(tr, 512)
    return pl.pallas_call(
        functools.partial(_expert_kernel, row_tile=row_tile),
        grid=(NE, R // tr, F // tf),
        in_specs=[pl.BlockSpec((1, tr, D), lambda e, r, f: (e, r, 0)),
                  pl.BlockSpec((1, D, tf), lambda e, r, f: (e, 0, f)),
                  pl.BlockSpec((1, D, tf), lambda e, r, f: (e, 0, f)),
                  pl.BlockSpec((1, tf, D), lambda e, r, f: (e, f, 0))],
        out_specs=pl.BlockSpec((1, tr, D), lambda e, r, f: (e, r, 0)),
        out_shape=jax.ShapeDtypeStruct((NE, R, D), BF16),
        scratch_shapes=[pltpu.VMEM((tr, D), F32), pltpu.VMEM((D, tf), BF16),
                        pltpu.VMEM((D, tf), BF16), pltpu.VMEM((tf, D), BF16)],
        compiler_params=_params(("parallel", "parallel", "arbitrary"), V7X_VMEM_LIMIT),
        name="experts",
    )(xin, w_gate, w_up, w_down)


def _combine_kernel(y_ref, pos_ref, gate_ref, x_ref, g2_ref, o_ref):
    ts = x_ref.shape[1]
    cap = y_ref.shape[2]
    pos = pos_ref[0]
    gate = gate_ref[0]
    slot = lax.broadcasted_iota(I32, (ts, cap), 1)
    acc = jnp.zeros((ts, x_ref.shape[2]), F32)
    for e in range(N_EXPERTS):
        scatter = jnp.where(pos[:, e:e + 1] == slot, gate[:, e:e + 1], 0.0).astype(BF16)
        acc = acc + _dot(scatter, y_ref[e, 0])
    o_ref[0] = x_ref[0] + g2_ref[0] * acc


def _combine(y, pos_t, gate_t, x1, gate2):
    B, S, D = x1.shape
    NE, _, cap, _ = y.shape
    ts = 512
    row = lambda b, i: (b, i, 0)
    return pl.pallas_call(
        _combine_kernel,
        grid=(B, S // ts),
        in_specs=[pl.BlockSpec((NE, 1, cap, D), lambda b, i: (0, b, 0, 0)),
                  pl.BlockSpec((1, ts, NE), row),
                  pl.BlockSpec((1, ts, NE), row),
                  pl.BlockSpec((1, ts, D), row),
                  pl.BlockSpec((1, 1, D), lambda b, i: (b, 0, 0))],
        out_specs=pl.BlockSpec((1, ts, D), row),
        out_shape=jax.ShapeDtypeStruct((B, S, D), F32),
        compiler_params=_params(("parallel", "arbitrary"), V7X_VMEM_LIMIT),
        name="combine",
    )(y, pos_t, gate_t, x1, gate2)


def _inproj_weight(w_in):
    D = w_in.shape[0]
    main = jnp.concatenate([w_in[:, :1024], w_in[:, 1040:2576], w_in[:, 1024:1040],
                            jnp.zeros((D, V7X_LANES - 16), w_in.dtype)], axis=1)
    return main.astype(BF16)


def kernel(x, c, w_ada, b_ada, norm1_g, w_in, conv_w, conv_b, w_q_blk, w_k_blk, w_v_blk, b_igate, b_fgate,
           mlstm_norm_g, mlstm_skip, q_norm_g, k_norm_g, rel_bias, w_out, norm2_g, w_router, b_router,
           w_gate, w_up, w_down):
    B, S, D = x.shape
    depth = w_ada.shape[0]
    bias = _attn_bias_tables(rel_bias)
    for l in range(depth):
        mod = _ada(c, w_ada[l], b_ada[l])
        shift1, scale1, gate1, shift2, scale2, gate2 = (
            mod[:, i * D:(i + 1) * D].reshape(B, 1, D) for i in range(N_MOD))

        x_m, o_pre, a_q, a_k, a_v, gates = _inproj(x, scale1, shift1, norm1_g[l].reshape(1, D),
                                                   _inproj_weight(w_in[l]), q_norm_g[l], k_norm_g[l])
        y_m = _mlstm(x_m, o_pre, gates, conv_w[l], conv_b[l], w_q_blk[l], w_k_blk[l], w_v_blk[l],
                     b_igate[l], b_fgate[l], mlstm_norm_g[l], mlstm_skip[l])
        y_a = _attn(a_q, a_k, a_v, bias)
        x1 = _outproj(y_m, y_a, x, gate1, w_out[l])

        xin, pos, aff = _route(x1, scale2, shift2, norm2_g[l].reshape(1, D), w_router[l], b_router[l])
        NE, _, cap, _ = xin.shape
        y = _experts(xin.reshape(NE, B * cap, D), w_gate[l], w_up[l], w_down[l]).reshape(NE, B, cap, D)
        x = _combine(y, pos.transpose(0, 2, 1), aff.transpose(0, 2, 1), x1, gate2)
    return x
```

```python
import functools
import math

import numpy as np
import jax
import jax.numpy as jnp
from jax import lax
from jax.experimental import pallas as pl
from jax.experimental.pallas import tpu as pltpu

F32 = jnp.float32
BF16 = jnp.bfloat16
I32 = jnp.int32

NORM_EPS = 1e-6
N_MOD = 6
MLSTM_HEADS = 4
MLSTM_HEAD_DIM = 128
MLSTM_QKV_BLOCK = 4
MLSTM_CONV = 5
MLSTM_CHUNK = 128
MLSTM_UNROLL = 8
ATTN_HEADS = 8
ATTN_HEAD_DIM = 64
ATTN_WIDTH = ATTN_HEADS * ATTN_HEAD_DIM
DILATIONS = (1, 4, 16)
ATTN_HALF = 64
ATTN_QBLK = 128
ATTN_KBLK = 256
ATTN_GROUP = 2
LOG2E = math.log2(math.e)
REL_BUCKETS = 32
REL_MAX_DIST = 1024
N_EXPERTS = 16
EC_CAPACITY_FACTOR = 2
NEG_BIG = -1e30

V7X_LANES = 128
V7X_VMEM_LIMIT = 56 * 1024 * 1024


def _sigmoid(x):
    return 1.0 / (1.0 + jnp.exp(-x))


def _dot(a, b):
    return jnp.dot(a, b, preferred_element_type=F32)


def _dot_nt(a, b):
    return lax.dot_general(a, b, (((1,), (1,)), ((), ())), preferred_element_type=F32)


def _split_bf16(x):
    hi = x.astype(BF16)
    lo = (x - hi.astype(F32)).astype(BF16)
    return hi, lo


def _params(sem, vmem=None):
    return pltpu.CompilerParams(dimension_semantics=sem, vmem_limit_bytes=vmem)


def _ada_kernel(c_ref, w_ref, b_ref, o_ref):
    c = c_ref[...]
    s = c * _sigmoid(c)
    o_ref[...] = jnp.dot(s, w_ref[...], preferred_element_type=F32,
                         precision=lax.Precision.HIGHEST) + b_ref[...]


def _ada(c, w, b):
    B, D = c.shape
    N = w.shape[1]
    tn = 1024
    return pl.pallas_call(
        _ada_kernel,
        grid=(N // tn,),
        in_specs=[pl.BlockSpec((B, D), lambda j: (0, 0)),
                  pl.BlockSpec((D, tn), lambda j: (0, j)),
                  pl.BlockSpec((1, tn), lambda j: (0, j))],
        out_specs=pl.BlockSpec((B, tn), lambda j: (0, j)),
        out_shape=jax.ShapeDtypeStruct((B, N), F32),
        compiler_params=_params(("arbitrary",)),
        name="ada",
    )(c, w, b.reshape(1, N))


def _modulated_norm(x, g, scale, shift):
    ms = jnp.mean(x * x, axis=-1, keepdims=True)
    y = x * lax.rsqrt(ms + NORM_EPS) * g
    return y * (1.0 + scale) + shift


def _inproj_kernel(x_ref, sc_ref, sh_ref, g_ref, w_ref, hm_ref, qg_ref, kg_ref,
                   xm_ref, op_ref, q_ref, k_ref, v_ref, gt_ref):
    h = _modulated_norm(x_ref[0], g_ref[...], sc_ref[0], sh_ref[0]).astype(BF16)
    col = lambda i: _dot(h, w_ref[:, 512 * i:512 * (i + 1)])
    q = col(2)
    k = col(3)
    xm_ref[0] = col(0)
    op_ref[0] = col(1)
    v_ref[0] = col(4)
    gt_ref[0] = _dot(h, w_ref[:, 2560:2688])[:, :16]

    def head_norm(t, g):
        ms = _dot((t * t).astype(BF16), hm_ref[...])
        return t * lax.rsqrt(ms + NORM_EPS) * g

    q_ref[0] = head_norm(q, qg_ref[...])
    k_ref[0] = head_norm(k, kg_ref[...])


def _inproj(x, scale, shift, g, w, q_norm_g, k_norm_g):
    B, S, D = x.shape
    tm = 512
    W = ATTN_WIDTH
    hid = jnp.arange(W) // ATTN_HEAD_DIM
    head_mean = jnp.where(hid[:, None] == hid[None, :], 1.0 / ATTN_HEAD_DIM, 0.0).astype(BF16)
    qg = jnp.tile(q_norm_g, ATTN_HEADS).reshape(1, W) * (LOG2E / math.sqrt(ATTN_HEAD_DIM))
    kg = jnp.tile(k_norm_g, ATTN_HEADS).reshape(1, W)
    row = lambda b, i: (b, i, 0)
    vec = lambda b, i: (b, 0, 0)
    const = lambda b, i: (0, 0)
    outs = ([jax.ShapeDtypeStruct((B, S, 512), F32)] * 2 + [jax.ShapeDtypeStruct((B, S, W), F32)] * 3
            + [jax.ShapeDtypeStruct((B, S, 16), F32)])
    return pl.pallas_call(
        _inproj_kernel,
        grid=(B, S // tm),
        in_specs=[pl.BlockSpec((1, tm, D), row),
                  pl.BlockSpec((1, 1, D), vec),
                  pl.BlockSpec((1, 1, D), vec),
                  pl.BlockSpec((1, D), const),
                  pl.BlockSpec(w.shape, const),
                  pl.BlockSpec((W, W), const),
                  pl.BlockSpec((1, W), const),
                  pl.BlockSpec((1, W), const)],
        out_specs=[pl.BlockSpec((1, tm, 512), row)] * 5 + [pl.BlockSpec((1, tm, 16), row)],
        out_shape=outs,
        compiler_params=_params(("parallel", "arbitrary"), V7X_VMEM_LIMIT),
        name="inproj",
    )(x, scale, shift, g, w, head_mean, qg, kg)


def _chunk_scan(x, op, reverse):
    n = x.shape[1]
    idx = lax.broadcasted_iota(I32, x.shape, 1) & (MLSTM_CHUNK - 1)
    s = 1
    while s < MLSTM_CHUNK:
        if reverse:
            r = pltpu.roll(x, n - s, 1)
            x = jnp.where(idx < MLSTM_CHUNK - s, op(x, r), x)
        else:
            r = pltpu.roll(x, s, 1)
            x = jnp.where(idx >= s, op(x, r), x)
        s *= 2
    return x


def _log_sigmoid(x):
    return jnp.minimum(x, 0.0) - jnp.log(1.0 + jnp.exp(-jnp.abs(x)))


def _mlstm_kernel(xm_ref, op_ref, grow_ref, brow_ref, cw_ref, cb_ref,
                  wq_ref, wkt_ref, wv_ref, ng_ref, sk_ref, y_ref,
                  q_s, kt_s, va_s, xc_s, hf_s, hb_s, row_s, col_s):
    S = xm_ref.shape[1]
    L = MLSTM_CHUNK
    NC = S // L
    E = MLSTM_HEAD_DIM
    xm = xm_ref[0]

    rows = lax.broadcasted_iota(I32, xm.shape, 0)
    half = MLSTM_CONV // 2
    conv = xm * cw_ref[half:half + 1, :]
    for j in range(MLSTM_CONV):
        off = j - half
        if off == 0:
            continue
        shifted = pltpu.roll(xm, (-off) % S, 0)
        valid = rows >= -off if off < 0 else rows < S - off
        conv = conv + jnp.where(valid, shifted, 0.0) * cw_ref[j:j + 1, :]
    conv = conv + cb_ref[...]
    xc = conv * _sigmoid(conv)
    xc_s[...] = xc

    xcb = xc.astype(BF16)
    q_s[...] = _dot(xcb, wq_ref[0]).astype(BF16)
    kt_s[...] = _dot_nt(wkt_ref[0], xcb) * (1.0 / math.sqrt(E))
    va_s[:, :E] = _dot(xm.astype(BF16), wv_ref[0]).astype(BF16)
    va_s[:, E:] = jnp.ones((S, E), BF16)

    gr = grow_ref[0, 0] + brow_ref[0]
    kind = lax.broadcasted_iota(I32, gr.shape, 0)
    gr = jnp.where((kind & 1) == 1, _log_sigmoid(gr), gr)
    b_f = _chunk_scan(gr[1:2], jnp.add, False)
    b_b = _chunk_scan(gr[3:4], jnp.add, True)
    u_f = gr[0:1] - b_f
    u_b = gr[2:3] - b_b
    m_f = _chunk_scan(u_f, jnp.maximum, False)
    m_b = _chunk_scan(u_b, jnp.maximum, True)
    row_s[...] = jnp.concatenate([u_f, u_b, gr[1:2], gr[3:4]], axis=0)
    for k, stat in enumerate((b_f, m_f, b_b, m_b)):
        for c in range(NC):
            col_s[k, c * L:(c + 1) * L, :] = jnp.broadcast_to(stat[:, c * L:(c + 1) * L], (L, L)).T

    ti = lax.broadcasted_iota(I32, (L, L), 0)
    tj = lax.broadcasted_iota(I32, (L, L), 1)
    causal = (tj <= ti, tj >= ti)

    def local_part(c, dirn):
        r0 = pl.multiple_of(c * L, L)
        qb = q_s[pl.ds(r0, L), :]
        kt = kt_s[:, pl.ds(r0, L)]
        va = va_s[pl.ds(r0, L), :]
        rr = row_s[:, pl.ds(r0, L)]
        u_r = rr[dirn:dirn + 1, :]
        g = jnp.sum(rr[2 + dirn:3 + dirn, :], axis=1, keepdims=True)
        m_loc = g + jnp.max(u_r, axis=1, keepdims=True)
        return dict(
            r0=r0, dirn=dirn, qb=qb, va=va, u_r=u_r, g=g, m_loc=m_loc,
            b_c=col_s[2 * dirn, pl.ds(r0, L), :], m_c=col_s[2 * dirn + 1, pl.ds(r0, L), :],
            s=_dot(qb, kt.astype(BF16)),
            d_state=_dot((kt * jnp.exp(g + u_r - m_loc)).astype(BF16), va))

    def intra_part(t):
        p = jnp.exp(jnp.where(causal[t["dirn"]], t["u_r"] - t["m_c"], NEG_BIG)) * t.pop("s")
        t["y_loc"] = _dot(p.astype(BF16), t["va"])

    def state_part(t, state):
        Ca, m = state
        x_st = _dot(t["qb"], Ca.astype(BF16))
        mm = jnp.maximum(m, t["m_c"])
        w_st, w_loc = jnp.exp(m - mm), jnp.exp(t["m_c"] - mm)
        num = w_st * x_st[:, :E] + w_loc * t["y_loc"][:, :E]
        den = w_st * x_st[:, E:] + w_loc * t["y_loc"][:, E:]
        h = num / jnp.maximum(jnp.abs(den), jnp.exp(-t["b_c"] - mm))
        m_new = jnp.maximum(t["g"] + m, t["m_loc"])
        Ca_new = jnp.exp(t["g"] + m - m_new) * Ca + jnp.exp(t["m_loc"] - m_new) * t["d_state"]
        return h, (Ca_new, m_new)

    def body(i, carry):
        states = list(carry)
        out_s = (hf_s, hb_s)
        work = []
        for k in range(MLSTM_UNROLL):
            c = i * MLSTM_UNROLL + k
            work += [local_part(c, 0), local_part(NC - 1 - c, 1)]
        for t in work:
            intra_part(t)
        for t in work:
            h, states[t["dirn"]] = state_part(t, states[t["dirn"]])
            out_s[t["dirn"]][pl.ds(t["r0"], L), :] = h
        return tuple(states)

    init = (jnp.zeros((E, 2 * E), F32), jnp.zeros((1, 1), F32))
    lax.fori_loop(0, NC // MLSTM_UNROLL, body, (init, init))

    h = hf_s[...] + hb_s[...]
    hn = h * lax.rsqrt(jnp.mean(h * h, axis=-1, keepdims=True) + NORM_EPS) * ng_ref[...]
    y = (hn + sk_ref[...] * xc_s[...]) * _sigmoid(op_ref[0])
    y_ref[0] = y.astype(y_ref.dtype)


def _blockdiag_dense(w_blk):
    nblk = w_blk.shape[0]
    per_head = nblk // MLSTM_HEADS
    w = w_blk.reshape(MLSTM_HEADS, per_head, MLSTM_QKV_BLOCK, MLSTM_QKV_BLOCK)
    eye = jnp.eye(per_head, dtype=w.dtype)
    dense = jnp.einsum('hgij,gk->hgikj', w, eye)
    return dense.reshape(MLSTM_HEADS, MLSTM_HEAD_DIM, MLSTM_HEAD_DIM)


def _mlstm(x_m, o_pre, gates, conv_w, conv_b, w_q_blk, w_k_blk, w_v_blk, b_igate, b_fgate, norm_g, skip):
    B, S, W = x_m.shape
    H, E = MLSTM_HEADS, MLSTM_HEAD_DIM
    assert S % (MLSTM_CHUNK * MLSTM_UNROLL) == 0
    grow = gates.reshape(B, S, 4, H).transpose(0, 3, 2, 1)
    bk = jnp.stack([b_igate[0], b_fgate[0], b_igate[1], b_fgate[1]], axis=0)
    brow = bk.T.reshape(H, 4, 1)
    wq = _blockdiag_dense(w_q_blk).astype(BF16)
    wkt = _blockdiag_dense(w_k_blk).transpose(0, 2, 1).astype(BF16)
    wv = _blockdiag_dense(w_v_blk).astype(BF16)
    headcol = lambda b, h: (b, 0, h)
    perhead = lambda b, h: (h, 0, 0)
    lanes = lambda b, h: (0, h)
    return pl.pallas_call(
        _mlstm_kernel,
        grid=(B, H),
        in_specs=[pl.BlockSpec((1, S, E), headcol),
                  pl.BlockSpec((1, S, E), headcol),
                  pl.BlockSpec((1, 1, 4, S), lambda b, h: (b, h, 0, 0)),
                  pl.BlockSpec((1, 4, 1), perhead),
                  pl.BlockSpec((MLSTM_CONV, E), lanes),
                  pl.BlockSpec((1, E), lanes),
                  pl.BlockSpec((1, E, E), perhead),
                  pl.BlockSpec((1, E, E), perhead),
                  pl.BlockSpec((1, E, E), perhead),
                  pl.BlockSpec((1, E), lanes),
                  pl.BlockSpec((1, E), lanes)],
        out_specs=pl.BlockSpec((1, S, E), headcol),
        out_shape=jax.ShapeDtypeStruct((B, S, W), BF16),
        scratch_shapes=[pltpu.VMEM((S, E), BF16), pltpu.VMEM((E, S), F32), pltpu.VMEM((S, 2 * E), BF16),
                        pltpu.VMEM((S, E), F32), pltpu.VMEM((S, E), F32), pltpu.VMEM((S, E), F32),
                        pltpu.VMEM((4, S), F32), pltpu.VMEM((4, S, E), F32)],
        compiler_params=_params(("parallel", "arbitrary"), V7X_VMEM_LIMIT),
        name="mlstm",
    )(x_m, o_pre, grow, brow, conv_w, conv_b.reshape(1, W), wq, wkt, wv,
      norm_g.reshape(1, W), skip.reshape(1, W))


def _t5_bucket_static(rel):
    half = REL_BUCKETS // 2
    exact = half // 2
    n = np.abs(rel)
    log_ratio = (np.log(np.maximum(n, 1).astype(np.float32) / np.float32(exact))
                 / np.float32(math.log(REL_MAX_DIST / exact)))
    large = np.minimum(exact + (log_ratio * np.float32(half - exact)).astype(np.int32), half - 1)
    return np.where(rel > 0, half, 0) + np.where(n < exact, n, large)


def _attn_bias_tables(rel_bias):
    Q, K = ATTN_QBLK, ATTN_KBLK
    n = Q + K
    nv = 3 * len(DILATIONS)
    diag = np.arange(n)[None, :] - (Q - 1) - ATTN_HALF * np.arange(3)[:, None]
    valid = np.tile(np.abs(diag) <= ATTN_HALF, (len(DILATIONS), 1))
    bucket = np.concatenate([_t5_bucket_static(diag * d) for d in DILATIONS], axis=0)
    onehot = (bucket[..., None] == np.arange(REL_BUCKETS)) & valid[..., None]
    w = jnp.einsum('vnb,bh->vhn', jnp.asarray(onehot, F32), rel_bias.astype(F32) * LOG2E,
                   precision=lax.Precision.HIGHEST)
    w = jnp.where(jnp.asarray(valid)[:, None, :], w, NEG_BIG)
    skew = jnp.tile(w, (1, 1, Q))[..., :Q * (n - 1)].reshape(nv, ATTN_HEADS, Q, n - 1)
    t = skew[..., Q - 1:]
    t = t.reshape(nv, ATTN_HEADS // 2, 2, Q, K).transpose(1, 0, 2, 3, 4)
    return t.reshape(ATTN_HEADS // 2, 2 * nv, Q, K)


def _attn_kernel(q_ref, k_ref, v_ref, bias_ref, y_ref, q0_s, q1_s, k_s, v0_s, v1_s, o_s, l_s):
    S = q_ref.shape[1]
    QB, KB = ATTN_QBLK, ATTN_KBLK
    lane = lax.broadcasted_iota(I32, (1, V7X_LANES), 1)
    in_head = (lane < ATTN_HEAD_DIM, lane >= ATTN_HEAD_DIM)
    q_s = (q0_s, q1_s)
    v_s = (v0_s, v1_s)

    def strided(start, size, d):
        return pl.ds(start, size) if d == 1 else pl.ds(start, size, stride=d)

    for p, d in enumerate(DILATIONS):
        L = S // d
        for r in range(d):
            src = strided(r, L, d)
            dst = slice(r * L, (r + 1) * L)
            q = q_ref[0, src, :]
            v = v_ref[0, src, :]
            k_s[p, dst, :] = k_ref[0, src, :].astype(BF16)
            for a in range(2):
                q_s[a][p, dst, :] = jnp.where(in_head[a], q, 0.0).astype(BF16)
                v_s[a][p, dst, :] = jnp.where(in_head[a], v, 1.0).astype(BF16)

    def logits(p, d, r, qb):
        L = S // d
        nqb = L // QB
        nk = min(L, KB)
        if nqb == 1 or qb == 0:
            k0, variant = 0, 0
        elif qb == nqb - 1:
            k0, variant = L - nk, 2
        else:
            k0, variant = qb * QB - ATTN_HALF, 1
        qrows = slice(r * L + qb * QB, r * L + (qb + 1) * QB)
        krows = slice(r * L + k0, r * L + k0 + nk)
        kt = k_s[p, krows, :]
        s = [_dot_nt(q_s[a][p, qrows, :], kt) + bias_ref[0, p * 6 + variant * 2 + a][:, :nk] for a in range(2)]
        return dict(p=p, krows=krows, out_rows=strided(r + d * qb * QB, QB, d), s=s)

    def softmax(t):
        t["m"] = [jnp.max(s, axis=1, keepdims=True) for s in t["s"]]
        t["e"] = [jnp.exp2(s - m).astype(BF16) for s, m in zip(t.pop("s"), t["m"])]

    def outputs(t):
        p = t["p"]
        acc = [_dot(t["e"][a], v_s[a][p, t["krows"], :]) for a in range(2)]
        num = jnp.where(in_head[0], acc[0], acc[1])
        den = pltpu.roll(jnp.where(in_head[0], acc[1], acc[0]), ATTN_HEAD_DIM, 1)
        o_s[p, t["out_rows"], :] = num / den
        l_s[p, t["out_rows"], :] = jnp.where(in_head[0], t["m"][0], t["m"][1]) + jnp.log2(den)

    units = [(p, d, r, qb) for p, d in enumerate(DILATIONS) for r in range(d) for qb in range(S // d // QB)]
    prev = []
    for i in range(0, len(units), ATTN_GROUP):
        cur = [logits(*u) for u in units[i:i + ATTN_GROUP]]
        for t in prev:
            outputs(t)
        for t in cur:
            softmax(t)
        prev = cur
    for t in prev:
        outputs(t)

    mx = jnp.maximum(jnp.maximum(l_s[0], l_s[1]), l_s[2])
    num = jnp.zeros((S, V7X_LANES), F32)
    den = jnp.zeros((S, V7X_LANES), F32)
    for p in range(len(DILATIONS)):
        w = jnp.exp2(l_s[p] - mx)
        num = num + w * o_s[p]
        den = den + w
    y_ref[0] = (num / den).astype(y_ref.dtype)


def _attn(a_q, a_k, a_v, bias):
    B, S, W = a_q.shape
    P = ATTN_HEADS // 2
    NP = len(DILATIONS)
    pair = lambda b, p: (b, 0, p)
    blk = pl.BlockSpec((1, S, V7X_LANES), pair)
    return pl.pallas_call(
        _attn_kernel,
        grid=(B, P),
        in_specs=[blk, blk, blk, pl.BlockSpec((1,) + bias.shape[1:], lambda b, p: (p, 0, 0, 0))],
        out_specs=blk,
        out_shape=jax.ShapeDtypeStruct((B, S, W), BF16),
        scratch_shapes=[pltpu.VMEM((NP, S, V7X_LANES), BF16)] * 5 + [pltpu.VMEM((NP, S, V7X_LANES), F32)] * 2,
        compiler_params=_params(("parallel", "arbitrary"), V7X_VMEM_LIMIT),
        name="attn",
    )(a_q, a_k, a_v, bias)


def _outproj_kernel(ym_ref, ya_ref, x_ref, g1_ref, w1_ref, w2_ref, o_ref):
    mix = _dot(ym_ref[0], w1_ref[...]) + _dot(ya_ref[0], w2_ref[...])
    o_ref[0] = x_ref[0] + g1_ref[0] * mix


def _outproj(y_m, y_a, x, gate1, w_out):
    B, S, D = x.shape
    Wm = y_m.shape[-1]
    tm = 512
    w1 = w_out[:Wm].astype(BF16)
    w2 = w_out[Wm:].astype(BF16)
    row = lambda b, i: (b, i, 0)
    const = lambda b, i: (0, 0)
    return pl.pallas_call(
        _outproj_kernel,
        grid=(B, S // tm),
        in_specs=[pl.BlockSpec((1, tm, Wm), row),
                  pl.BlockSpec((1, tm, y_a.shape[-1]), row),
                  pl.BlockSpec((1, tm, D), row),
                  pl.BlockSpec((1, 1, D), lambda b, i: (b, 0, 0)),
                  pl.BlockSpec(w1.shape, const),
                  pl.BlockSpec(w2.shape, const)],
        out_specs=pl.BlockSpec((1, tm, D), row),
        out_shape=jax.ShapeDtypeStruct((B, S, D), F32),
        compiler_params=_params(("parallel", "arbitrary"), V7X_VMEM_LIMIT),
        name="outproj",
    )(y_m, y_a, x, gate1, w1, w2)


def _route_kernel(x_ref, sc_ref, sh_ref, g_ref, whi_ref, wlo_ref, br_ref, tri_ref,
                  xin_ref, pos_ref, gate_ref, h_s, pos_s, *, cap):
    S = x_ref.shape[1]
    NE = N_EXPERTS
    e = pl.program_id(1)

    @pl.when(e == 0)
    def _():
        h = _modulated_norm(x_ref[0], g_ref[...], sc_ref[0], sh_ref[0])
        hi, lo = _split_bf16(h)
        h_s[...] = hi
        logits = _dot(hi, whi_ref[...]) + _dot(lo, whi_ref[...]) + _dot(hi, wlo_ref[...])
        lt = logits.T[:NE, :] + br_ref[...]
        ex = jnp.exp(lt - jnp.max(lt, axis=0, keepdims=True))
        aff = ex / jnp.sum(ex, axis=0, keepdims=True)
        gate_ref[0] = aff

        bits = pltpu.bitcast(aff, I32)

        def search(i, v):
            cand = v | lax.shift_left(jnp.int32(1), 30 - i)
            cnt = jnp.sum((bits >= cand).astype(F32), axis=1, keepdims=True)
            return jnp.where(cnt >= cap, cand, v)

        thr = lax.fori_loop(0, 31, search, jnp.zeros((NE, 1), I32))
        gt = (bits > thr).astype(F32)
        eq = (bits == thr).astype(F32)
        need = cap - jnp.sum(gt, axis=1, keepdims=True)

        def prefix_count(mask):
            off = jnp.zeros((NE, 1), F32)
            parts = []
            for j in range(S // V7X_LANES):
                t = mask[:, j * V7X_LANES:(j + 1) * V7X_LANES]
                parts.append(_dot(t.astype(BF16), tri_ref[...]) + off)
                off = off + jnp.sum(t, axis=1, keepdims=True)
            return jnp.concatenate(parts, axis=1)

        sel = jnp.maximum(gt, jnp.where(prefix_count(eq) < need, eq, 0.0))
        pos = jnp.where(sel > 0.0, prefix_count(sel), -1.0).astype(I32)
        pos_s[...] = pos
        pos_ref[0] = pos

    prow = pos_s[pl.ds(e, 1), :]
    slot = lax.broadcasted_iota(I32, (cap, S), 0)
    onehot = jnp.where(prow == slot, 1.0, 0.0).astype(BF16)
    xin_ref[0, 0] = _dot(onehot, h_s[...]).astype(xin_ref.dtype)


def _route(x1, scale, shift, g, w_router, b_router):
    B, S, D = x1.shape
    NE = N_EXPERTS
    cap = (EC_CAPACITY_FACTOR * S) // NE
    wpad = jnp.zeros((D, V7X_LANES), F32).at[:, :NE].set(w_router)
    whi, wlo = _split_bf16(wpad)
    ti = jnp.arange(V7X_LANES)
    tri = (ti[:, None] < ti[None, :]).astype(BF16)
    vec = lambda b, e: (b, 0, 0)
    const = lambda b, e: (0, 0)
    return pl.pallas_call(
        functools.partial(_route_kernel, cap=cap),
        grid=(B, NE),
        in_specs=[pl.BlockSpec((1, S, D), vec),
                  pl.BlockSpec((1, 1, D), vec),
                  pl.BlockSpec((1, 1, D), vec),
                  pl.BlockSpec((1, D), const),
                  pl.BlockSpec((D, V7X_LANES), const),
                  pl.BlockSpec((D, V7X_LANES), const),
                  pl.BlockSpec((NE, 1), const),
                  pl.BlockSpec((V7X_LANES, V7X_LANES), const)],
        out_specs=[pl.BlockSpec((1, 1, cap, D), lambda b, e: (e, b, 0, 0)),
                   pl.BlockSpec((1, NE, S), vec),
                   pl.BlockSpec((1, NE, S), vec)],
        out_shape=[jax.ShapeDtypeStruct((NE, B, cap, D), BF16),
                   jax.ShapeDtypeStruct((B, NE, S), I32),
                   jax.ShapeDtypeStruct((B, NE, S), F32)],
        scratch_shapes=[pltpu.VMEM((S, D), BF16), pltpu.VMEM((NE, S), I32)],
        compiler_params=_params(("parallel", "arbitrary"), V7X_VMEM_LIMIT),
        name="route",
    )(x1, scale, shift, g, whi, wlo, b_router.reshape(NE, 1), tri)


def _expert_kernel(x_ref, wg_ref, wu_ref, wd_ref, o_ref, acc_s, wg_s, wu_s, wd_s, *, row_tile):
    f = pl.program_id(2)
    nf = pl.num_programs(2)
    n_tiles = x_ref.shape[1] // row_tile

    def sweep(first, last):
        wg_s[...] = wg_ref[0].astype(BF16)
        wu_s[...] = wu_ref[0].astype(BF16)
        wd_s[...] = wd_ref[0].astype(BF16)

        def down(r, hid):
            y = _dot(hid, wd_s[...])
            if not first:
                y = y + acc_s[r, :]
            if last:
                o_ref[0, r, :] = y.astype(o_ref.dtype)
            else:
                acc_s[r, :] = y

        pending = None
        for i in range(n_tiles):
            r = slice(i * row_tile, (i + 1) * row_tile)
            xb = x_ref[0, r, :]
            g = _dot(xb, wg_s[...])
            u = _dot(xb, wu_s[...])
            if pending is not None:
                down(*pending)
            pending = (r, (g * _sigmoid(g) * u).astype(BF16))
        down(*pending)

    @pl.when(f == 0)
    def _():
        sweep(True, False)

    @pl.when(jnp.logical_and(f > 0, f < nf - 1))
    def _():
        sweep(False, False)

    @pl.when(f == nf - 1)
    def _():
        sweep(False, True)


def _experts(xin, w_gate, w_up, w_down):
    NE, R, D = xin.shape
    F = w_gate.shape[-1]
    tr = min(R, 2048)
    tf = 512
    row_tile = min(tr, 512)
    return pl.pallas_call(
        functools.partial(_expert_kernel, row_tile=row_tile),
        grid=(NE, R // tr, F // tf),
        in_specs=[pl.BlockSpec((1, tr, D), lambda e, r, f: (e, r, 0)),
                  pl.BlockSpec((1, D, tf), lambda e, r, f: (e, 0, f)),
                  pl.BlockSpec((1, D, tf), lambda e, r, f: (e, 0, f)),
                  pl.BlockSpec((1, tf, D), lambda e, r, f: (e, f, 0))],
        out_specs=pl.BlockSpec((1, tr, D), lambda e, r, f: (e, r, 0)),
        out_shape=jax.ShapeDtypeStruct((NE, R, D), BF16),
        scratch_shapes=[pltpu.VMEM((tr, D), F32), pltpu.VMEM((D, tf), BF16),
                        pltpu.VMEM((D, tf), BF16), pltpu.VMEM((tf, D), BF16)],
        compiler_params=_params(("parallel", "parallel", "arbitrary"), V7X_VMEM_LIMIT),
        name="experts",
    )(xin, w_gate, w_up, w_down)


def _combine_kernel(y_ref, pos_ref, gate_ref, x_ref, g2_ref, o_ref):
    ts = x_ref.shape[1]
    cap = y_ref.shape[2]
    pos = pos_ref[0]
    gate = gate_ref[0]
    slot = lax.broadcasted_iota(I32, (ts, cap), 1)
    acc = jnp.zeros((ts, x_ref.shape[2]), F32)
    for e in range(N_EXPERTS):
        scatter = jnp.where(pos[:, e:e + 1] == slot, gate[:, e:e + 1], 0.0).astype(BF16)
        acc = acc + _dot(scatter, y_ref[e, 0])
    o_ref[0] = x_ref[0] + g2_ref[0] * acc


def _combine(y, pos_t, gate_t, x1, gate2):
    B, S, D = x1.shape
    NE, _, cap, _ = y.shape
    ts = 512
    row = lambda b, i: (b, i, 0)
    return pl.pallas_call(
        _combine_kernel,
        grid=(B, S // ts),
        in_specs=[pl.BlockSpec((NE, 1, cap, D), lambda b, i: (0, b, 0, 0)),
                  pl.BlockSpec((1, ts, NE), row),
                  pl.BlockSpec((1, ts, NE), row),
                  pl.BlockSpec((1, ts, D), row),
                  pl.BlockSpec((1, 1, D), lambda b, i: (b, 0, 0))],
        out_specs=pl.BlockSpec((1, ts, D), row),
        out_shape=jax.ShapeDtypeStruct((B, S, D), F32),
        compiler_params=_params(("parallel", "arbitrary"), V7X_VMEM_LIMIT),
        name="combine",
    )(y, pos_t, gate_t, x1, gate2)


def _inproj_weight(w_in):
    D = w_in.shape[0]
    main = jnp.concatenate([w_in[:, :1024], w_in[:, 1040:2576], w_in[:, 1024:1040],
                            jnp.zeros((D, V7X_LANES - 16), w_in.dtype)], axis=1)
    return main.astype(BF16)


def kernel(x, c, w_ada, b_ada, norm1_g, w_in, conv_w, conv_b, w_q_blk, w_k_blk, w_v_blk, b_igate, b_fgate,
           mlstm_norm_g, mlstm_skip, q_norm_g, k_norm_g, rel_bias, w_out, norm2_g, w_router, b_router,
           w_gate, w_up, w_down):
    B, S, D = x.shape
    depth = w_ada.shape[0]
    bias = _attn_bias_tables(rel_bias)
    for l in range(depth):
        mod = _ada(c, w_ada[l], b_ada[l])
        shift1, scale1, gate1, shift2, scale2, gate2 = (
            mod[:, i * D:(i + 1) * D].reshape(B, 1, D) for i in range(N_MOD))

        x_m, o_pre, a_q, a_k, a_v, gates = _inproj(x, scale1, shift1, norm1_g[l].reshape(1, D),
                                                   _inproj_weight(w_in[l]), q_norm_g[l], k_norm_g[l])
        y_m = _mlstm(x_m, o_pre, gates, conv_w[l], conv_b[l], w_q_blk[l], w_k_blk[l], w_v_blk[l],
                     b_igate[l], b_fgate[l], mlstm_norm_g[l], mlstm_skip[l])
        y_a = _attn(a_q, a_k, a_v, bias)
        x1 = _outproj(y_m, y_a, x, gate1, w_out[l])

        xin, pos, aff = _route(x1, scale2, shift2, norm2_g[l].reshape(1, D), w_router[l], b_router[l])
        NE, _, cap, _ = xin.shape
        y = _experts(xin.reshape(NE, B * cap, D), w_gate[l], w_up[l], w_down[l]).reshape(NE, B, cap, D)
        x = _combine(y, pos.transpose(0, 2, 1), aff.transpose(0, 2, 1), x1, gate2)
    return x
```

```python
import functools
import math

import numpy as np
import jax
import jax.numpy as jnp
from jax import lax
from jax.experimental import pallas as pl
from jax.experimental.pallas import tpu as pltpu

F32 = jnp.float32
BF16 = jnp.bfloat16
I32 = jnp.int32

NORM_EPS = 1e-6
N_MOD = 6
MLSTM_HEADS = 4
MLSTM_HEAD_DIM = 128
MLSTM_QKV_BLOCK = 4
MLSTM_CONV = 5
MLSTM_CHUNK = 128
MLSTM_UNROLL = 8
MLSTM_ONES_ROWS = 16
ATTN_HEADS = 8
ATTN_HEAD_DIM = 64
ATTN_WIDTH = ATTN_HEADS * ATTN_HEAD_DIM
DILATIONS = (1, 4, 16)
ATTN_HALF = 64
ATTN_QBLK = 128
ATTN_KBLK = 256
ATTN_DIAG = 512
ATTN_GROUP = 2
LOG2E = math.log2(math.e)
REL_BUCKETS = 32
REL_MAX_DIST = 1024
N_EXPERTS = 16
EC_CAPACITY_FACTOR = 2
NEG_BIG = -1e30

V7X_LANES = 128
V7X_VMEM_LIMIT = 56 * 1024 * 1024


def _sigmoid(x):
    return 1.0 / (1.0 + jnp.exp(-x))


def _dot(a, b):
    return jnp.dot(a, b, preferred_element_type=F32)


def _dot_nt(a, b):
    return lax.dot_general(a, b, (((1,), (1,)), ((), ())), preferred_element_type=F32)


def _split_bf16(x):
    hi = x.astype(BF16)
    lo = (x - hi.astype(F32)).astype(BF16)
    return hi, lo


def _params(sem, vmem=None):
    return pltpu.CompilerParams(dimension_semantics=sem, vmem_limit_bytes=vmem)


def _ada_kernel(c_ref, w_ref, b_ref, o_ref):
    c = c_ref[...]
    s = c * _sigmoid(c)
    o_ref[...] = jnp.dot(s, w_ref[...], preferred_element_type=F32,
                         precision=lax.Precision.HIGHEST) + b_ref[...]


def _ada(c, w, b):
    B, D = c.shape
    N = w.shape[1]
    tn = 1024
    return pl.pallas_call(
        _ada_kernel,
        grid=(N // tn,),
        in_specs=[pl.BlockSpec((B, D), lambda j: (0, 0)),
                  pl.BlockSpec((D, tn), lambda j: (0, j)),
                  pl.BlockSpec((1, tn), lambda j: (0, j))],
        out_specs=pl.BlockSpec((B, tn), lambda j: (0, j)),
        out_shape=jax.ShapeDtypeStruct((B, N), F32),
        compiler_params=_params(("arbitrary",)),
        name="ada",
    )(c, w, b.reshape(1, N))


def _modulated_norm(x, g, scale, shift):
    ms = jnp.mean(x * x, axis=-1, keepdims=True)
    return x * lax.rsqrt(ms + NORM_EPS) * (g * (1.0 + scale)) + shift


def _inproj_kernel(x_ref, sc_ref, sh_ref, g_ref, w_ref, hm_ref, qg_ref, kg_ref,
                   xm_ref, op_ref, q_ref, k_ref, v_ref, gt_ref):
    h = _modulated_norm(x_ref[0], g_ref[...], sc_ref[0], sh_ref[0]).astype(BF16)
    col = lambda i: _dot(h, w_ref[:, 512 * i:512 * (i + 1)])
    q = col(2)
    k = col(3)
    xm_ref[0] = col(0)
    op_ref[0] = col(1)
    v_ref[0] = col(4)
    gt_ref[0] = _dot(h, w_ref[:, 2560:2688])[:, :16]

    def head_norm(t, g):
        ms = _dot((t * t).astype(BF16), hm_ref[...])
        return t * lax.rsqrt(ms + NORM_EPS) * g

    q_ref[0] = head_norm(q, qg_ref[...])
    k_ref[0] = head_norm(k, kg_ref[...])


def _inproj(x, scale, shift, g, w, q_norm_g, k_norm_g):
    B, S, D = x.shape
    tm = 512
    W = ATTN_WIDTH
    hid = jnp.arange(W) // ATTN_HEAD_DIM
    head_mean = jnp.where(hid[:, None] == hid[None, :], 1.0 / ATTN_HEAD_DIM, 0.0).astype(BF16)
    qg = jnp.tile(q_norm_g, ATTN_HEADS).reshape(1, W) * (LOG2E / math.sqrt(ATTN_HEAD_DIM))
    kg = jnp.tile(k_norm_g, ATTN_HEADS).reshape(1, W)
    row = lambda b, i: (b, i, 0)
    vec = lambda b, i: (b, 0, 0)
    const = lambda b, i: (0, 0)
    outs = ([jax.ShapeDtypeStruct((B, S, 512), F32)] * 2 + [jax.ShapeDtypeStruct((B, S, W), F32)] * 3
            + [jax.ShapeDtypeStruct((B, S, 16), F32)])
    return pl.pallas_call(
        _inproj_kernel,
        grid=(B, S // tm),
        in_specs=[pl.BlockSpec((1, tm, D), row),
                  pl.BlockSpec((1, 1, D), vec),
                  pl.BlockSpec((1, 1, D), vec),
                  pl.BlockSpec((1, D), const),
                  pl.BlockSpec(w.shape, const),
                  pl.BlockSpec((W, W), const),
                  pl.BlockSpec((1, W), const),
                  pl.BlockSpec((1, W), const)],
        out_specs=[pl.BlockSpec((1, tm, 512), row)] * 5 + [pl.BlockSpec((1, tm, 16), row)],
        out_shape=outs,
        compiler_params=_params(("parallel", "arbitrary"), V7X_VMEM_LIMIT),
        name="inproj",
    )(x, scale, shift, g, w, head_mean, qg, kg)


def _chunk_scan(x, op, reverse):
    n = x.shape[1]
    idx = lax.broadcasted_iota(I32, x.shape, 1) & (MLSTM_CHUNK - 1)
    s = 1
    while s < MLSTM_CHUNK:
        if reverse:
            r = pltpu.roll(x, n - s, 1)
            x = jnp.where(idx < MLSTM_CHUNK - s, op(x, r), x)
        else:
            r = pltpu.roll(x, s, 1)
            x = jnp.where(idx >= s, op(x, r), x)
        s *= 2
    return x


def _log_sigmoid(x):
    return jnp.minimum(x, 0.0) - jnp.log(1.0 + jnp.exp(-jnp.abs(x)))


def _mlstm_kernel(xm_ref, op_ref, grow_ref, brow_ref, cw_ref, cb_ref,
                  wqt_ref, wk_ref, wvt_ref, ng_ref, sk_ref, y_ref,
                  qt_s, k_s, vat_s, xc_s, hf_s, hb_s, row_s, col_s):
    S = xm_ref.shape[1]
    L = MLSTM_CHUNK
    NC = S // L
    E = MLSTM_HEAD_DIM
    A = E + MLSTM_ONES_ROWS
    xm = xm_ref[0]

    rows = lax.broadcasted_iota(I32, xm.shape, 0)
    half = MLSTM_CONV // 2
    conv = xm * cw_ref[half:half + 1, :]
    for j in range(MLSTM_CONV):
        off = j - half
        if off == 0:
            continue
        shifted = pltpu.roll(xm, (-off) % S, 0)
        valid = rows >= -off if off < 0 else rows < S - off
        conv = conv + jnp.where(valid, shifted, 0.0) * cw_ref[j:j + 1, :]
    conv = conv + cb_ref[...]
    xc = conv * _sigmoid(conv)
    xc_s[...] = xc

    xcb = xc.astype(BF16)
    qt_s[...] = _dot_nt(wqt_ref[0], xcb).astype(BF16)
    k_s[...] = (_dot(xcb, wk_ref[0]) * (1.0 / math.sqrt(E))).astype(BF16)
    vat_s[:E, :] = _dot_nt(wvt_ref[0], xm.astype(BF16))
    vat_s[E:, :] = jnp.ones((MLSTM_ONES_ROWS, S), F32)

    gr = grow_ref[0, 0] + brow_ref[0]
    kind = lax.broadcasted_iota(I32, gr.shape, 0)
    gr = jnp.where((kind & 1) == 1, _log_sigmoid(gr), gr)
    b_f = _chunk_scan(gr[1:2], jnp.add, False)
    b_b = _chunk_scan(gr[3:4], jnp.add, True)
    u_f = gr[0:1] - b_f
    u_b = gr[2:3] - b_b
    m_f = _chunk_scan(u_f, jnp.maximum, False)
    m_b = _chunk_scan(u_b, jnp.maximum, True)
    row_s[...] = jnp.concatenate([b_f, m_f, u_f, gr[1:2], b_b, m_b, u_b, gr[3:4]], axis=0)
    for k, stat in enumerate((u_f, u_b)):
        for c in range(NC):
            col_s[k, c * L:(c + 1) * L, :] = jnp.broadcast_to(stat[:, c * L:(c + 1) * L], (L, L)).T

    kj = lax.broadcasted_iota(I32, (L, L), 0)
    qi = lax.broadcasted_iota(I32, (L, L), 1)
    causal = (kj <= qi, kj >= qi)

    def local_part(c, dirn):
        r0 = pl.multiple_of(c * L, L)
        qt = qt_s[:, pl.ds(r0, L)]
        kc = k_s[pl.ds(r0, L), :]
        vat = vat_s[:, pl.ds(r0, L)]
        rr = row_s[:, pl.ds(r0, L)]
        b_r, m_r, u_r = (rr[4 * dirn + i:4 * dirn + i + 1, :] for i in range(3))
        g = jnp.sum(rr[4 * dirn + 3:4 * dirn + 4, :], axis=1, keepdims=True)
        m_loc = g + jnp.max(u_r, axis=1, keepdims=True)
        return dict(
            r0=r0, dirn=dirn, qt=qt, vat=vat.astype(BF16), b_r=b_r, m_r=m_r, g=g, m_loc=m_loc,
            u_c=col_s[dirn, pl.ds(r0, L), :],
            s=_dot(kc, qt),
            d_state=_dot((vat * jnp.exp(g + u_r - m_loc)).astype(BF16), kc))

    def intra_part(t):
        p = jnp.exp(jnp.where(causal[t["dirn"]], t["u_c"] - t["m_r"], NEG_BIG)) * t.pop("s")
        t["y_loc"] = _dot(t["vat"], p.astype(BF16))

    def state_part(t, state):
        Ct, m = state
        x_st = _dot(Ct.astype(BF16), t["qt"])
        mm = jnp.maximum(m, t["m_r"])
        z = jnp.exp(m - mm) * x_st + jnp.exp(t["m_r"] - mm) * t["y_loc"]
        h = z[:E, :] / jnp.maximum(jnp.abs(z[E:E + 1, :]), jnp.exp(-t["b_r"] - mm))
        m_new = jnp.maximum(t["g"] + m, t["m_loc"])
        Ct_new = jnp.exp(t["g"] + m - m_new) * Ct + jnp.exp(t["m_loc"] - m_new) * t["d_state"]
        return h, (Ct_new, m_new)

    def body(i, carry):
        states = list(carry)
        out_s = (hf_s, hb_s)
        work = []
        for k in range(MLSTM_UNROLL):
            c = i * MLSTM_UNROLL + k
            work += [local_part(c, 0), local_part(NC - 1 - c, 1)]
        for t in work:
            intra_part(t)
        for t in work:
            h, states[t["dirn"]] = state_part(t, states[t["dirn"]])
            out_s[t["dirn"]][:, pl.ds(t["r0"], L)] = h
        return tuple(states)

    init = (jnp.zeros((A, E), F32), jnp.zeros((1, 1), F32))
    lax.fori_loop(0, NC // MLSTM_UNROLL, body, (init, init))

    h = (hf_s[...] + hb_s[...]).T
    hn = h * lax.rsqrt(jnp.mean(h * h, axis=-1, keepdims=True) + NORM_EPS) * ng_ref[...]
    y = (hn + sk_ref[...] * xc_s[...]) * _sigmoid(op_ref[0])
    y_ref[0] = y.astype(y_ref.dtype)


def _blockdiag_dense(w_blk):
    nblk = w_blk.shape[0]
    per_head = nblk // MLSTM_HEADS
    w = w_blk.reshape(MLSTM_HEADS, per_head, MLSTM_QKV_BLOCK, MLSTM_QKV_BLOCK)
    eye = jnp.eye(per_head, dtype=w.dtype)
    dense = jnp.einsum('hgij,gk->hgikj', w, eye)
    return dense.reshape(MLSTM_HEADS, MLSTM_HEAD_DIM, MLSTM_HEAD_DIM)


def _mlstm(x_m, o_pre, gates, conv_w, conv_b, w_q_blk, w_k_blk, w_v_blk, b_igate, b_fgate, norm_g, skip):
    B, S, W = x_m.shape
    H, E = MLSTM_HEADS, MLSTM_HEAD_DIM
    assert S % (MLSTM_CHUNK * MLSTM_UNROLL) == 0
    grow = gates.reshape(B, S, 4, H).transpose(0, 3, 2, 1)
    bk = jnp.stack([b_igate[0], b_fgate[0], b_igate[1], b_fgate[1]], axis=0)
    brow = bk.T.reshape(H, 4, 1)
    wqt = _blockdiag_dense(w_q_blk).transpose(0, 2, 1).astype(BF16)
    wk = _blockdiag_dense(w_k_blk).astype(BF16)
    wvt = _blockdiag_dense(w_v_blk).transpose(0, 2, 1).astype(BF16)
    headcol = lambda b, h: (b, 0, h)
    perhead = lambda b, h: (h, 0, 0)
    lanes = lambda b, h: (0, h)
    return pl.pallas_call(
        _mlstm_kernel,
        grid=(B, H),
        in_specs=[pl.BlockSpec((1, S, E), headcol),
                  pl.BlockSpec((1, S, E), headcol),
                  pl.BlockSpec((1, 1, 4, S), lambda b, h: (b, h, 0, 0)),
                  pl.BlockSpec((1, 4, 1), perhead),
                  pl.BlockSpec((MLSTM_CONV, E), lanes),
                  pl.BlockSpec((1, E), lanes),
                  pl.BlockSpec((1, E, E), perhead),
                  pl.BlockSpec((1, E, E), perhead),
                  pl.BlockSpec((1, E, E), perhead),
                  pl.BlockSpec((1, E), lanes),
                  pl.BlockSpec((1, E), lanes)],
        out_specs=pl.BlockSpec((1, S, E), headcol),
        out_shape=jax.ShapeDtypeStruct((B, S, W), BF16),
        scratch_shapes=[pltpu.VMEM((E, S), BF16), pltpu.VMEM((S, E), BF16),
                        pltpu.VMEM((E + MLSTM_ONES_ROWS, S), F32),
                        pltpu.VMEM((S, E), F32), pltpu.VMEM((E, S), F32), pltpu.VMEM((E, S), F32),
                        pltpu.VMEM((8, S), F32), pltpu.VMEM((2, S, E), F32)],
        compiler_params=_params(("parallel", "arbitrary"), V7X_VMEM_LIMIT),
        name="mlstm",
    )(x_m, o_pre, grow, brow, conv_w, conv_b.reshape(1, W), wqt, wk, wvt,
      norm_g.reshape(1, W), skip.reshape(1, W))


def _t5_bucket_static(rel):
    half = REL_BUCKETS // 2
    exact = half // 2
    n = np.abs(rel)
    log_ratio = (np.log(np.maximum(n, 1).astype(np.float32) / np.float32(exact))
                 / np.float32(math.log(REL_MAX_DIST / exact)))
    large = np.minimum(exact + (log_ratio * np.float32(half - exact)).astype(np.int32), half - 1)
    return np.where(rel > 0, half, 0) + np.where(n < exact, n, large)


def _attn_bias_diagonals(rel_bias):
    n = ATTN_DIAG
    nv = 3 * len(DILATIONS)
    x = np.arange(n)
    offset = np.where(x <= ATTN_KBLK, x, x - n)
    rel = offset[None, :] - ATTN_HALF * np.arange(3)[:, None]
    valid = np.tile(np.abs(rel) <= ATTN_HALF, (len(DILATIONS), 1))
    bucket = np.concatenate([_t5_bucket_static(rel * d) for d in DILATIONS], axis=0)
    onehot = (bucket[..., None] == np.arange(REL_BUCKETS)) & valid[..., None]
    w = jnp.einsum('vnb,bh->vhn', jnp.asarray(onehot, F32), rel_bias.astype(F32) * LOG2E,
                   precision=lax.Precision.HIGHEST)
    w = jnp.where(jnp.asarray(valid)[:, None, :], w, NEG_BIG)
    w = w.reshape(nv, ATTN_HEADS // 2, 2, n).transpose(1, 0, 2, 3)
    return w.reshape(ATTN_HEADS // 2, 2 * nv, n)


def _attn_kernel(q_ref, k_ref, v_ref, diag_ref, y_ref, q0_s, q1_s, k_s, v0_s, v1_s, o_s, l_s, bias_s):
    S = q_ref.shape[1]
    QB, KB = ATTN_QBLK, ATTN_KBLK
    lane = lax.broadcasted_iota(I32, (1, V7X_LANES), 1)
    in_head = (lane < ATTN_HEAD_DIM, lane >= ATTN_HEAD_DIM)
    q_s = (q0_s, q1_s)
    v_s = (v0_s, v1_s)

    pair = pl.program_id(1)

    @pl.when(pl.program_id(0) == 0)
    def _():
        for i in range(bias_s.shape[1]):
            rows_i = jnp.broadcast_to(diag_ref[0, i:i + 1, :], (QB, ATTN_DIAG))
            bias_s[pair, i] = pltpu.roll(rows_i, 0, 1, stride=1, stride_axis=0)[:, :KB]

    def strided(start, size, d):
        return pl.ds(start, size) if d == 1 else pl.ds(start, size, stride=d)

    for p, d in enumerate(DILATIONS):
        L = S // d
        for r in range(d):
            src = strided(r, L, d)
            dst = slice(r * L, (r + 1) * L)
            q = q_ref[0, src, :]
            v = v_ref[0, src, :]
            k_s[p, dst, :] = k_ref[0, src, :].astype(BF16)
            for a in range(2):
                q_s[a][p, dst, :] = jnp.where(in_head[a], q, 0.0).astype(BF16)
                v_s[a][p, dst, :] = jnp.where(in_head[a], v, 1.0).astype(BF16)

    def logits(p, d, r, qb):
        L = S // d
        nqb = L // QB
        nk = min(L, KB)
        if nqb == 1 or qb == 0:
            k0, variant = 0, 0
        elif qb == nqb - 1:
            k0, variant = L - nk, 2
        else:
            k0, variant = qb * QB - ATTN_HALF, 1
        qrows = slice(r * L + qb * QB, r * L + (qb + 1) * QB)
        krows = slice(r * L + k0, r * L + k0 + nk)
        kt = k_s[p, krows, :]
        s = [_dot_nt(q_s[a][p, qrows, :], kt) + bias_s[pair, p * 6 + variant * 2 + a][:, :nk] for a in range(2)]
        return dict(p=p, krows=krows, out_rows=strided(r + d * qb * QB, QB, d), s=s)

    def softmax(t):
        t["m"] = [jnp.max(s, axis=1, keepdims=True) for s in t["s"]]
        t["e"] = [jnp.exp2(s - m).astype(BF16) for s, m in zip(t.pop("s"), t["m"])]

    def outputs(t):
        p = t["p"]
        acc = [_dot(t["e"][a], v_s[a][p, t["krows"], :]) for a in range(2)]
        num = jnp.where(in_head[0], acc[0], acc[1])
        den = pltpu.roll(jnp.where(in_head[0], acc[1], acc[0]), ATTN_HEAD_DIM, 1)
        o_s[p, t["out_rows"], :] = num / den
        l_s[p, t["out_rows"], :] = jnp.where(in_head[0], t["m"][0], t["m"][1]) + jnp.log2(den)

    units = [(p, d, r, qb) for p, d in enumerate(DILATIONS) for r in range(d) for qb in range(S // d // QB)]
    prev = []
    for i in range(0, len(units), ATTN_GROUP):
        cur = [logits(*u) for u in units[i:i + ATTN_GROUP]]
        for t in prev:
            outputs(t)
        for t in cur:
            softmax(t)
        prev = cur
    for t in prev:
        outputs(t)

    mx = jnp.maximum(jnp.maximum(l_s[0], l_s[1]), l_s[2])
    num = jnp.zeros((S, V7X_LANES), F32)
    den = jnp.zeros((S, V7X_LANES), F32)
    for p in range(len(DILATIONS)):
        w = jnp.exp2(l_s[p] - mx)
        num = num + w * o_s[p]
        den = den + w
    y_ref[0] = (num / den).astype(y_ref.dtype)


def _attn(a_q, a_k, a_v, diag):
    B, S, W = a_q.shape
    P = ATTN_HEADS // 2
    NP = len(DILATIONS)
    pair = lambda b, p: (b, 0, p)
    blk = pl.BlockSpec((1, S, V7X_LANES), pair)
    return pl.pallas_call(
        _attn_kernel,
        grid=(B, P),
        in_specs=[blk, blk, blk, pl.BlockSpec((1,) + diag.shape[1:], lambda b, p: (p, 0, 0))],
        out_specs=blk,
        out_shape=jax.ShapeDtypeStruct((B, S, W), BF16),
        scratch_shapes=[pltpu.VMEM((NP, S, V7X_LANES), BF16)] * 5 + [pltpu.VMEM((NP, S, V7X_LANES), F32)] * 2
                       + [pltpu.VMEM((P, diag.shape[1], ATTN_QBLK, ATTN_KBLK), F32)],
        compiler_params=_params(("arbitrary", "arbitrary"), V7X_VMEM_LIMIT),
        name="attn",
    )(a_q, a_k, a_v, diag)


def _outproj_kernel(ym_ref, ya_ref, x_ref, g1_ref, w1_ref, w2_ref, o_ref):
    mix = _dot(ym_ref[0], w1_ref[...]) + _dot(ya_ref[0], w2_ref[...])
    o_ref[0] = x_ref[0] + g1_ref[0] * mix


def _outproj(y_m, y_a, x, gate1, w_out):
    B, S, D = x.shape
    Wm = y_m.shape[-1]
    tm = 512
    w1 = w_out[:Wm].astype(BF16)
    w2 = w_out[Wm:].astype(BF16)
    row = lambda b, i: (b, i, 0)
    const = lambda b, i: (0, 0)
    return pl.pallas_call(
        _outproj_kernel,
        grid=(B, S // tm),
        in_specs=[pl.BlockSpec((1, tm, Wm), row),
                  pl.BlockSpec((1, tm, y_a.shape[-1]), row),
                  pl.BlockSpec((1, tm, D), row),
                  pl.BlockSpec((1, 1, D), lambda b, i: (b, 0, 0)),
                  pl.BlockSpec(w1.shape, const),
                  pl.BlockSpec(w2.shape, const)],
        out_specs=pl.BlockSpec((1, tm, D), row),
        out_shape=jax.ShapeDtypeStruct((B, S, D), F32),
        compiler_params=_params(("parallel", "arbitrary"), V7X_VMEM_LIMIT),
        name="outproj",
    )(y_m, y_a, x, gate1, w1, w2)


def _route_kernel(x_ref, sc_ref, sh_ref, g_ref, whl_ref, br_ref, tri_ref,
                  xin_ref, pos_ref, gate_ref, h_s, pos_s, *, cap):
    S = x_ref.shape[1]
    NE = N_EXPERTS
    e = pl.program_id(1)

    @pl.when(e == 0)
    def _():
        h = _modulated_norm(x_ref[0], g_ref[...], sc_ref[0], sh_ref[0])
        hi, lo = _split_bf16(h)
        h_s[...] = hi
        both = _dot(hi, whl_ref[...])
        logits = both[:, :V7X_LANES] + both[:, V7X_LANES:] + _dot(lo, whl_ref[:, :V7X_LANES])
        lt = logits.T[:NE, :] + br_ref[...]
        ex = jnp.exp(lt - jnp.max(lt, axis=0, keepdims=True))
        aff = ex / jnp.sum(ex, axis=0, keepdims=True)
        gate_ref[0] = aff

        bits = pltpu.bitcast(aff, I32)

        def count_ge(cand):
            return jnp.sum((bits >= cand).astype(F32), axis=1, keepdims=True)

        def search(i, v):
            shift = 27 - 3 * i
            best = v
            for c in range(1, 8):
                cand = v | lax.shift_left(jnp.int32(c), shift)
                best = jnp.where(count_ge(cand) >= cap, cand, best)
            return best

        top = jnp.full((NE, 1), 1 << 30, I32)
        thr = lax.fori_loop(0, 10, search, jnp.where(count_ge(top) >= cap, top, 0))
        gt = (bits > thr).astype(F32)
        eq = (bits == thr).astype(F32)
        need = cap - jnp.sum(gt, axis=1, keepdims=True)

        def prefix_count(mask):
            off = jnp.zeros((NE, 1), F32)
            parts = []
            for j in range(S // V7X_LANES):
                t = mask[:, j * V7X_LANES:(j + 1) * V7X_LANES]
                parts.append(_dot(t.astype(BF16), tri_ref[...]) + off)
                off = off + jnp.sum(t, axis=1, keepdims=True)
            return jnp.concatenate(parts, axis=1)

        sel = jnp.maximum(gt, jnp.where(prefix_count(eq) < need, eq, 0.0))
        pos = jnp.where(sel > 0.0, prefix_count(sel), -1.0).astype(I32)
        pos_s[...] = pos
        pos_ref[0] = pos

    prow = pos_s[pl.ds(e, 1), :]
    slot = lax.broadcasted_iota(I32, (cap, S), 0)
    onehot = jnp.where(prow == slot, 1.0, 0.0).astype(BF16)
    xin_ref[0, 0] = _dot(onehot, h_s[...]).astype(xin_ref.dtype)


def _route(x1, scale, shift, g, w_router, b_router):
    B, S, D = x1.shape
    NE = N_EXPERTS
    cap = (EC_CAPACITY_FACTOR * S) // NE
    wpad = jnp.zeros((D, V7X_LANES), F32).at[:, :NE].set(w_router)
    whl = jnp.concatenate(_split_bf16(wpad), axis=1)
    ti = jnp.arange(V7X_LANES)
    tri = (ti[:, None] < ti[None, :]).astype(BF16)
    vec = lambda b, e: (b, 0, 0)
    const = lambda b, e: (0, 0)
    return pl.pallas_call(
        functools.partial(_route_kernel, cap=cap),
        grid=(B, NE),
        in_specs=[pl.BlockSpec((1, S, D), vec),
                  pl.BlockSpec((1, 1, D), vec),
                  pl.BlockSpec((1, 1, D), vec),
                  pl.BlockSpec((1, D), const),
                  pl.BlockSpec((D, 2 * V7X_LANES), const),
                  pl.BlockSpec((NE, 1), const),
                  pl.BlockSpec((V7X_LANES, V7X_LANES), const)],
        out_specs=[pl.BlockSpec((1, 1, cap, D), lambda b, e: (e, b, 0, 0)),
                   pl.BlockSpec((1, NE, S), vec),
                   pl.BlockSpec((1, NE, S), vec)],
        out_shape=[jax.ShapeDtypeStruct((NE, B, cap, D), BF16),
                   jax.ShapeDtypeStruct((B, NE, S), I32),
                   jax.ShapeDtypeStruct((B, NE, S), F32)],
        scratch_shapes=[pltpu.VMEM((S, D), BF16), pltpu.VMEM((NE, S), I32)],
        compiler_params=_params(("parallel", "arbitrary"), V7X_VMEM_LIMIT),
        name="route",
    )(x1, scale, shift, g, whl, b_router.reshape(NE, 1), tri)


def _expert_kernel(x_ref, wg_ref, wu_ref, wd_ref, o_ref, acc_s, wg_s, wu_s, wd_s, *, row_tile):
    f = pl.program_id(2)
    nf = pl.num_programs(2)
    n_tiles = x_ref.shape[1] // row_tile

    def sweep(first, last):
        wg_s[...] = wg_ref[0].astype(BF16)
        wu_s[...] = wu_ref[0].astype(BF16)
        wd_s[...] = wd_ref[0].astype(BF16)

        def down(r, hid):
            y = _dot(hid, wd_s[...])
            if not first:
                y = y + acc_s[r, :]
            if last:
                o_ref[0, r, :] = y.astype(o_ref.dtype)
            else:
                acc_s[r, :] = y

        pending = None
        for i in range(n_tiles):
            r = slice(i * row_tile, (i + 1) * row_tile)
            xb = x_ref[0, r, :]
            g = _dot(xb, wg_s[...])
            u = _dot(xb, wu_s[...])
            if pending is not None:
                down(*pending)
            pending = (r, (g * _sigmoid(g) * u).astype(BF16))
        down(*pending)

    @pl.when(f == 0)
    def _():
        sweep(True, False)

    @pl.when(jnp.logical_and(f > 0, f < nf - 1))
    def _():
        sweep(False, False)

    @pl.when(f == nf - 1)
    def _():
        sweep(False, True)


def _experts(xin, w_gate, w_up, w_down):
    NE, R, D = xin.shape
    F = w_gate.shape[-1]
    tr = min(R, 2048)
    tf = 512
    row_tile = min(tr, 512)
    return pl.pallas_call(
        functools.partial(_expert_kernel, row_tile=row_tile),
        grid=(NE, R // tr, F // tf),
        in_specs=[pl.BlockSpec((1, tr, D), lambda e, r, f: (e, r, 0)),
                  pl.BlockSpec((1, D, tf), lambda e, r, f: (e, 0, f)),
                  pl.BlockSpec((1, D, tf), lambda e, r, f: (e, 0, f)),
                  pl.BlockSpec((1, tf, D), lambda e, r, f: (e, f, 0))],
        out_specs=pl.BlockSpec((1, tr, D), lambda e, r, f: (e, r, 0)),
        out_shape=jax.ShapeDtypeStruct((NE, R, D), BF16),
        scratch_shapes=[pltpu.VMEM((tr, D), F32), pltpu.VMEM((D, tf), BF16),
                        pltpu.VMEM((D, tf), BF16), pltpu.VMEM((tf, D), BF16)],
        compiler_params=_params(("parallel", "parallel", "arbitrary"), V7X_VMEM_LIMIT),
        name="experts",
    )(xin, w_gate, w_up, w_down)


def _combine_kernel(y_ref, pos_ref, gate_ref, x_ref, g2_ref, o_ref):
    ts = x_ref.shape[1]
    cap = y_ref.shape[2]
    pos = pos_ref[0]
    gate = gate_ref[0]
    slot = lax.broadcasted_iota(I32, (ts, cap), 1)
    acc = jnp.zeros((ts, x_ref.shape[2]), F32)
    for e in range(N_EXPERTS):
        scatter = jnp.where(pos[:, e:e + 1] == slot, gate[:, e:e + 1], 0.0).astype(BF16)
        acc = acc + _dot(scatter, y_ref[e, 0])
    o_ref[0] = x_ref[0] + g2_ref[0] * acc


def _combine(y, pos_t, gate_t, x1, gate2):
    B, S, D = x1.shape
    NE, _, cap, _ = y.shape
    ts = 512
    row = lambda b, i: (b, i, 0)
    return pl.pallas_call(
        _combine_kernel,
        grid=(B, S // ts),
        in_specs=[pl.BlockSpec((NE, 1, cap, D), lambda b, i: (0, b, 0, 0)),
                  pl.BlockSpec((1, ts, NE), row),
                  pl.BlockSpec((1, ts, NE), row),
                  pl.BlockSpec((1, ts, D), row),
                  pl.BlockSpec((1, 1, D), lambda b, i: (b, 0, 0))],
        out_specs=pl.BlockSpec((1, ts, D), row),
        out_shape=jax.ShapeDtypeStruct((B, S, D), F32),
        compiler_params=_params(("parallel", "arbitrary"), V7X_VMEM_LIMIT),
        name="combine",
    )(y, pos_t, gate_t, x1, gate2)


def _inproj_weight(w_in):
    D = w_in.shape[0]
    main = jnp.concatenate([w_in[:, :1024], w_in[:, 1040:2576], w_in[:, 1024:1040],
                            jnp.zeros((D, V7X_LANES - 16), w_in.dtype)], axis=1)
    return main.astype(BF16)


def kernel(x, c, w_ada, b_ada, norm1_g, w_in, conv_w, conv_b, w_q_blk, w_k_blk, w_v_blk, b_igate, b_fgate,
           mlstm_norm_g, mlstm_skip, q_norm_g, k_norm_g, rel_bias, w_out, norm2_g, w_router, b_router,
           w_gate, w_up, w_down):
    B, S, D = x.shape
    depth = w_ada.shape[0]
    diag = _attn_bias_diagonals(rel_bias)
    for l in range(depth):
        mod = _ada(c, w_ada[l], b_ada[l])
        shift1, scale1, gate1, shift2, scale2, gate2 = (
            mod[:, i * D:(i + 1) * D].reshape(B, 1, D) for i in range(N_MOD))

        x_m, o_pre, a_q, a_k, a_v, gates = _inproj(x, scale1, shift1, norm1_g[l].reshape(1, D),
                                                   _inproj_weight(w_in[l]), q_norm_g[l], k_norm_g[l])
        y_m = _mlstm(x_m, o_pre, gates, conv_w[l], conv_b[l], w_q_blk[l], w_k_blk[l], w_v_blk[l],
                     b_igate[l], b_fgate[l], mlstm_norm_g[l], mlstm_skip[l])
        y_a = _attn(a_q, a_k, a_v, diag)
        x1 = _outproj(y_m, y_a, x, gate1, w_out[l])

        xin, pos, aff = _route(x1, scale2, shift2, norm2_g[l].reshape(1, D), w_router[l], b_router[l])
        NE, _, cap, _ = xin.shape
        y = _experts(xin.reshape(NE, B * cap, D), w_gate[l], w_up[l], w_down[l]).reshape(NE, B, cap, D)
        x = _combine(y, pos.transpose(0, 2, 1), aff.transpose(0, 2, 1), x1, gate2)
    return x
```

```python
import functools
import math

import numpy as np
import jax
import jax.numpy as jnp
from jax import lax
from jax.experimental import pallas as pl
from jax.experimental.pallas import tpu as pltpu

F32 = jnp.float32
BF16 = jnp.bfloat16
I32 = jnp.int32

NORM_EPS = 1e-6
N_MOD = 6
MLSTM_HEADS = 4
MLSTM_HEAD_DIM = 128
MLSTM_QKV_BLOCK = 4
MLSTM_CONV = 5
MLSTM_CHUNK = 128
MLSTM_UNROLL = 8
MLSTM_ONES_ROWS = 16
ATTN_HEADS = 8
ATTN_HEAD_DIM = 64
ATTN_WIDTH = ATTN_HEADS * ATTN_HEAD_DIM
DILATIONS = (1, 4, 16)
ATTN_HALF = 64
ATTN_QBLK = 128
ATTN_KBLK = 256
ATTN_DIAG = 512
ATTN_GROUP = 2
LOG2E = math.log2(math.e)
REL_BUCKETS = 32
REL_MAX_DIST = 1024
N_EXPERTS = 16
EC_CAPACITY_FACTOR = 2
MOE_TOKEN_TILE = 256
MOE_WINDOW = 64
NEG_BIG = -1e30

V7X_LANES = 128
V7X_VMEM_LIMIT = 56 * 1024 * 1024


def _sigmoid(x):
    return 1.0 / (1.0 + jnp.exp(-x))


def _dot(a, b):
    return jnp.dot(a, b, preferred_element_type=F32)


def _dot_nt(a, b):
    return lax.dot_general(a, b, (((1,), (1,)), ((), ())), preferred_element_type=F32)


def _dot_tn(a, b):
    return lax.dot_general(a, b, (((0,), (0,)), ((), ())), preferred_element_type=F32)


def _split_bf16(x):
    hi = x.astype(BF16)
    lo = (x - hi.astype(F32)).astype(BF16)
    return hi, lo


def _params(sem, vmem=None):
    return pltpu.CompilerParams(dimension_semantics=sem, vmem_limit_bytes=vmem)


def _ada_kernel(c_ref, w_ref, b_ref, o_ref):
    c = c_ref[...]
    s = c * _sigmoid(c)
    o_ref[...] = jnp.dot(s, w_ref[...], preferred_element_type=F32,
                         precision=lax.Precision.HIGHEST) + b_ref[...]


def _ada(c, w, b):
    B, D = c.shape
    N = w.shape[1]
    tn = 1024
    return pl.pallas_call(
        _ada_kernel,
        grid=(N // tn,),
        in_specs=[pl.BlockSpec((B, D), lambda j: (0, 0)),
                  pl.BlockSpec((D, tn), lambda j: (0, j)),
                  pl.BlockSpec((1, tn), lambda j: (0, j))],
        out_specs=pl.BlockSpec((B, tn), lambda j: (0, j)),
        out_shape=jax.ShapeDtypeStruct((B, N), F32),
        compiler_params=_params(("arbitrary",)),
        name="ada",
    )(c, w, b.reshape(1, N))


def _modulated_norm(x, g, scale, shift):
    ms = jnp.mean(x * x, axis=-1, keepdims=True)
    return x * lax.rsqrt(ms + NORM_EPS) * (g * (1.0 + scale)) + shift


def _inproj_kernel(x_ref, sc_ref, sh_ref, g_ref, w_ref, hm_ref, qg_ref, kg_ref,
                   xm_ref, op_ref, q_ref, k_ref, v_ref, gt_ref):
    h = _modulated_norm(x_ref[0], g_ref[...], sc_ref[0], sh_ref[0]).astype(BF16)
    col = lambda i: _dot(h, w_ref[:, 512 * i:512 * (i + 1)])
    q = col(2)
    k = col(3)
    xm_ref[0] = col(0)
    op_ref[0] = col(1)
    v_ref[0] = col(4)
    gt_ref[0] = _dot(h, w_ref[:, 2560:2688])[:, :16]

    def head_norm(t, g):
        ms = _dot((t * t).astype(BF16), hm_ref[...])
        return t * lax.rsqrt(ms + NORM_EPS) * g

    q_ref[0] = head_norm(q, qg_ref[...])
    k_ref[0] = head_norm(k, kg_ref[...])


def _inproj(x, scale, shift, g, w, q_norm_g, k_norm_g):
    B, S, D = x.shape
    tm = 512
    W = ATTN_WIDTH
    hid = jnp.arange(W) // ATTN_HEAD_DIM
    head_mean = jnp.where(hid[:, None] == hid[None, :], 1.0 / ATTN_HEAD_DIM, 0.0).astype(BF16)
    qg = jnp.tile(q_norm_g, ATTN_HEADS).reshape(1, W) * (LOG2E / math.sqrt(ATTN_HEAD_DIM))
    kg = jnp.tile(k_norm_g, ATTN_HEADS).reshape(1, W)
    row = lambda b, i: (b, i, 0)
    vec = lambda b, i: (b, 0, 0)
    const = lambda b, i: (0, 0)
    outs = ([jax.ShapeDtypeStruct((B, S, 512), F32)] * 2 + [jax.ShapeDtypeStruct((B, S, W), F32)] * 3
            + [jax.ShapeDtypeStruct((B, S, 16), F32)])
    return pl.pallas_call(
        _inproj_kernel,
        grid=(B, S // tm),
        in_specs=[pl.BlockSpec((1, tm, D), row),
                  pl.BlockSpec((1, 1, D), vec),
                  pl.BlockSpec((1, 1, D), vec),
                  pl.BlockSpec((1, D), const),
                  pl.BlockSpec(w.shape, const),
                  pl.BlockSpec((W, W), const),
                  pl.BlockSpec((1, W), const),
                  pl.BlockSpec((1, W), const)],
        out_specs=[pl.BlockSpec((1, tm, 512), row)] * 5 + [pl.BlockSpec((1, tm, 16), row)],
        out_shape=outs,
        compiler_params=_params(("parallel", "arbitrary"), V7X_VMEM_LIMIT),
        name="inproj",
    )(x, scale, shift, g, w, head_mean, qg, kg)


def _chunk_scan(x, op, reverse):
    n = x.shape[1]
    idx = lax.broadcasted_iota(I32, x.shape, 1) & (MLSTM_CHUNK - 1)
    s = 1
    while s < MLSTM_CHUNK:
        if reverse:
            r = pltpu.roll(x, n - s, 1)
            x = jnp.where(idx < MLSTM_CHUNK - s, op(x, r), x)
        else:
            r = pltpu.roll(x, s, 1)
            x = jnp.where(idx >= s, op(x, r), x)
        s *= 2
    return x


def _log_sigmoid(x):
    return jnp.minimum(x, 0.0) - jnp.log(1.0 + jnp.exp(-jnp.abs(x)))


def _mlstm_kernel(xm_ref, op_ref, grow_ref, brow_ref, cw_ref, cb_ref,
                  wqt_ref, wk_ref, wvt_ref, ng_ref, sk_ref, y_ref,
                  qt_s, k_s, vat_s, xc_s, hf_s, hb_s, row_s, col_s):
    S = xm_ref.shape[1]
    L = MLSTM_CHUNK
    NC = S // L
    E = MLSTM_HEAD_DIM
    A = E + MLSTM_ONES_ROWS
    xm = xm_ref[0]

    rows = lax.broadcasted_iota(I32, xm.shape, 0)
    half = MLSTM_CONV // 2
    conv = xm * cw_ref[half:half + 1, :]
    for j in range(MLSTM_CONV):
        off = j - half
        if off == 0:
            continue
        shifted = pltpu.roll(xm, (-off) % S, 0)
        valid = rows >= -off if off < 0 else rows < S - off
        conv = conv + jnp.where(valid, shifted, 0.0) * cw_ref[j:j + 1, :]
    conv = conv + cb_ref[...]
    xc = conv * _sigmoid(conv)
    xc_s[...] = xc

    xcb = xc.astype(BF16)
    qt_s[...] = _dot_nt(wqt_ref[0], xcb).astype(BF16)
    k_s[...] = (_dot(xcb, wk_ref[0]) * (1.0 / math.sqrt(E))).astype(BF16)
    vat_s[:E, :] = _dot_nt(wvt_ref[0], xm.astype(BF16))
    vat_s[E:, :] = jnp.ones((MLSTM_ONES_ROWS, S), F32)

    gr = grow_ref[0, 0] + brow_ref[0]
    kind = lax.broadcasted_iota(I32, gr.shape, 0)
    gr = jnp.where((kind & 1) == 1, _log_sigmoid(gr), gr)
    b_f = _chunk_scan(gr[1:2], jnp.add, False)
    b_b = _chunk_scan(gr[3:4], jnp.add, True)
    u_f = gr[0:1] - b_f
    u_b = gr[2:3] - b_b
    m_f = _chunk_scan(u_f, jnp.maximum, False)
    m_b = _chunk_scan(u_b, jnp.maximum, True)
    row_s[...] = jnp.concatenate([b_f, m_f, u_f, gr[1:2], b_b, m_b, u_b, gr[3:4]], axis=0)
    for k, stat in enumerate((u_f, u_b)):
        for c in range(NC):
            col_s[k, c * L:(c + 1) * L, :] = jnp.broadcast_to(stat[:, c * L:(c + 1) * L], (L, L)).T

    kj = lax.broadcasted_iota(I32, (L, L), 0)
    qi = lax.broadcasted_iota(I32, (L, L), 1)
    causal = (kj <= qi, kj >= qi)

    def local_part(c, dirn):
        r0 = pl.multiple_of(c * L, L)
        qt = qt_s[:, pl.ds(r0, L)]
        kc = k_s[pl.ds(r0, L), :]
        vat = vat_s[:, pl.ds(r0, L)]
        rr = row_s[:, pl.ds(r0, L)]
        b_r, m_r, u_r = (rr[4 * dirn + i:4 * dirn + i + 1, :] for i in range(3))
        g = jnp.sum(rr[4 * dirn + 3:4 * dirn + 4, :], axis=1, keepdims=True)
        m_loc = g + jnp.max(u_r, axis=1, keepdims=True)
        return dict(
            r0=r0, dirn=dirn, qt=qt, vat=vat.astype(BF16), b_r=b_r, m_r=m_r, g=g, m_loc=m_loc,
            u_c=col_s[dirn, pl.ds(r0, L), :],
            s=_dot(kc, qt),
            d_state=_dot((vat * jnp.exp(g + u_r - m_loc)).astype(BF16), kc))

    def intra_part(t):
        p = jnp.exp(jnp.where(causal[t["dirn"]], t["u_c"] - t["m_r"], NEG_BIG)) * t.pop("s")
        t["y_loc"] = _dot(t["vat"], p.astype(BF16))

    def state_part(t, state):
        Ct, m = state
        x_st = _dot(Ct.astype(BF16), t["qt"])
        mm = jnp.maximum(m, t["m_r"])
        z = jnp.exp(m - mm) * x_st + jnp.exp(t["m_r"] - mm) * t["y_loc"]
        h = z[:E, :] / jnp.maximum(jnp.abs(z[E:E + 1, :]), jnp.exp(-t["b_r"] - mm))
        m_new = jnp.maximum(t["g"] + m, t["m_loc"])
        Ct_new = jnp.exp(t["g"] + m - m_new) * Ct + jnp.exp(t["m_loc"] - m_new) * t["d_state"]
        return h, (Ct_new, m_new)

    def body(i, carry):
        states = list(carry)
        out_s = (hf_s, hb_s)
        work = []
        for k in range(MLSTM_UNROLL):
            c = i * MLSTM_UNROLL + k
            work += [local_part(c, 0), local_part(NC - 1 - c, 1)]
        for t in work:
            intra_part(t)
        for t in work:
            h, states[t["dirn"]] = state_part(t, states[t["dirn"]])
            out_s[t["dirn"]][:, pl.ds(t["r0"], L)] = h
        return tuple(states)

    init = (jnp.zeros((A, E), F32), jnp.zeros((1, 1), F32))
    lax.fori_loop(0, NC // MLSTM_UNROLL, body, (init, init))

    h = (hf_s[...] + hb_s[...]).T
    hn = h * lax.rsqrt(jnp.mean(h * h, axis=-1, keepdims=True) + NORM_EPS) * ng_ref[...]
    y = (hn + sk_ref[...] * xc_s[...]) * _sigmoid(op_ref[0])
    y_ref[0] = y.astype(y_ref.dtype)


def _blockdiag_dense(w_blk):
    nblk = w_blk.shape[0]
    per_head = nblk // MLSTM_HEADS
    w = w_blk.reshape(MLSTM_HEADS, per_head, MLSTM_QKV_BLOCK, MLSTM_QKV_BLOCK)
    eye = jnp.eye(per_head, dtype=w.dtype)
    dense = jnp.einsum('hgij,gk->hgikj', w, eye)
    return dense.reshape(MLSTM_HEADS, MLSTM_HEAD_DIM, MLSTM_HEAD_DIM)


def _mlstm(x_m, o_pre, gates, conv_w, conv_b, w_q_blk, w_k_blk, w_v_blk, b_igate, b_fgate, norm_g, skip):
    B, S, W = x_m.shape
    H, E = MLSTM_HEADS, MLSTM_HEAD_DIM
    assert S % (MLSTM_CHUNK * MLSTM_UNROLL) == 0
    grow = gates.reshape(B, S, 4, H).transpose(0, 3, 2, 1)
    bk = jnp.stack([b_igate[0], b_fgate[0], b_igate[1], b_fgate[1]], axis=0)
    brow = bk.T.reshape(H, 4, 1)
    wqt = _blockdiag_dense(w_q_blk).transpose(0, 2, 1).astype(BF16)
    wk = _blockdiag_dense(w_k_blk).astype(BF16)
    wvt = _blockdiag_dense(w_v_blk).transpose(0, 2, 1).astype(BF16)
    headcol = lambda b, h: (b, 0, h)
    perhead = lambda b, h: (h, 0, 0)
    lanes = lambda b, h: (0, h)
    return pl.pallas_call(
        _mlstm_kernel,
        grid=(B, H),
        in_specs=[pl.BlockSpec((1, S, E), headcol),
                  pl.BlockSpec((1, S, E), headcol),
                  pl.BlockSpec((1, 1, 4, S), lambda b, h: (b, h, 0, 0)),
                  pl.BlockSpec((1, 4, 1), perhead),
                  pl.BlockSpec((MLSTM_CONV, E), lanes),
                  pl.BlockSpec((1, E), lanes),
                  pl.BlockSpec((1, E, E), perhead),
                  pl.BlockSpec((1, E, E), perhead),
                  pl.BlockSpec((1, E, E), perhead),
                  pl.BlockSpec((1, E), lanes),
                  pl.BlockSpec((1, E), lanes)],
        out_specs=pl.BlockSpec((1, S, E), headcol),
        out_shape=jax.ShapeDtypeStruct((B, S, W), BF16),
        scratch_shapes=[pltpu.VMEM((E, S), BF16), pltpu.VMEM((S, E), BF16),
                        pltpu.VMEM((E + MLSTM_ONES_ROWS, S), F32),
                        pltpu.VMEM((S, E), F32), pltpu.VMEM((E, S), F32), pltpu.VMEM((E, S), F32),
                        pltpu.VMEM((8, S), F32), pltpu.VMEM((2, S, E), F32)],
        compiler_params=_params(("parallel", "arbitrary"), V7X_VMEM_LIMIT),
        name="mlstm",
    )(x_m, o_pre, grow, brow, conv_w, conv_b.reshape(1, W), wqt, wk, wvt,
      norm_g.reshape(1, W), skip.reshape(1, W))


def _t5_bucket_static(rel):
    half = REL_BUCKETS // 2
    exact = half // 2
    n = np.abs(rel)
    log_ratio = (np.log(np.maximum(n, 1).astype(np.float32) / np.float32(exact))
                 / np.float32(math.log(REL_MAX_DIST / exact)))
    large = np.minimum(exact + (log_ratio * np.float32(half - exact)).astype(np.int32), half - 1)
    return np.where(rel > 0, half, 0) + np.where(n < exact, n, large)


def _attn_bias_diagonals(rel_bias):
    n = ATTN_DIAG
    nv = 3 * len(DILATIONS)
    x = np.arange(n)
    offset = np.where(x <= ATTN_KBLK, x, x - n)
    rel = offset[None, :] - ATTN_HALF * np.arange(3)[:, None]
    valid = np.tile(np.abs(rel) <= ATTN_HALF, (len(DILATIONS), 1))
    bucket = np.concatenate([_t5_bucket_static(rel * d) for d in DILATIONS], axis=0)
    onehot = (bucket[..., None] == np.arange(REL_BUCKETS)) & valid[..., None]
    w = jnp.einsum('vnb,bh->vhn', jnp.asarray(onehot, F32), rel_bias.astype(F32) * LOG2E,
                   precision=lax.Precision.HIGHEST)
    w = jnp.where(jnp.asarray(valid)[:, None, :], w, NEG_BIG)
    w = w.reshape(nv, ATTN_HEADS // 2, 2, n).transpose(1, 0, 2, 3)
    return w.reshape(ATTN_HEADS // 2, 2 * nv, n)


def _attn_kernel(q_ref, k_ref, v_ref, diag_ref, y_ref, q0_s, q1_s, k_s, v0_s, v1_s, o_s, l_s, bias_s):
    S = q_ref.shape[1]
    QB, KB = ATTN_QBLK, ATTN_KBLK
    lane = lax.broadcasted_iota(I32, (1, V7X_LANES), 1)
    in_head = (lane < ATTN_HEAD_DIM, lane >= ATTN_HEAD_DIM)
    q_s = (q0_s, q1_s)
    v_s = (v0_s, v1_s)

    pair = pl.program_id(1)

    @pl.when(pl.program_id(0) == 0)
    def _():
        for i in range(bias_s.shape[1]):
            rows_i = jnp.broadcast_to(diag_ref[0, i:i + 1, :], (QB, ATTN_DIAG))
            bias_s[pair, i] = pltpu.roll(rows_i, 0, 1, stride=1, stride_axis=0)[:, :KB]

    def strided(start, size, d):
        return pl.ds(start, size) if d == 1 else pl.ds(start, size, stride=d)

    for p, d in enumerate(DILATIONS):
        L = S // d
        for r in range(d):
            src = strided(r, L, d)
            dst = slice(r * L, (r + 1) * L)
            q = q_ref[0, src, :]
            v = v_ref[0, src, :]
            k_s[p, dst, :] = k_ref[0, src, :].astype(BF16)
            for a in range(2):
                q_s[a][p, dst, :] = jnp.where(in_head[a], q, 0.0).astype(BF16)
                v_s[a][p, dst, :] = jnp.where(in_head[a], v, 1.0).astype(BF16)

    def logits(p, d, r, qb):
        L = S // d
        nqb = L // QB
        nk = min(L, KB)
        if nqb == 1 or qb == 0:
            k0, variant = 0, 0
        elif qb == nqb - 1:
            k0, variant = L - nk, 2
        else:
            k0, variant = qb * QB - ATTN_HALF, 1
        qrows = slice(r * L + qb * QB, r * L + (qb + 1) * QB)
        krows = slice(r * L + k0, r * L + k0 + nk)
        kt = k_s[p, krows, :]
        s = [_dot_nt(q_s[a][p, qrows, :], kt) + bias_s[pair, p * 6 + variant * 2 + a][:, :nk] for a in range(2)]
        return dict(p=p, krows=krows, out_rows=strided(r + d * qb * QB, QB, d), s=s)

    def softmax(t):
        t["m"] = [jnp.max(s, axis=1, keepdims=True) for s in t["s"]]
        t["e"] = [jnp.exp2(s - m).astype(BF16) for s, m in zip(t.pop("s"), t["m"])]

    def outputs(t):
        p = t["p"]
        acc = [_dot(t["e"][a], v_s[a][p, t["krows"], :]) for a in range(2)]
        num = jnp.where(in_head[0], acc[0], acc[1])
        den = pltpu.roll(jnp.where(in_head[0], acc[1], acc[0]), ATTN_HEAD_DIM, 1)
        o_s[p, t["out_rows"], :] = num / den
        l_s[p, t["out_rows"], :] = jnp.where(in_head[0], t["m"][0], t["m"][1]) + jnp.log2(den)

    units = [(p, d, r, qb) for p, d in enumerate(DILATIONS) for r in range(d) for qb in range(S // d // QB)]
    prev = []
    for i in range(0, len(units), ATTN_GROUP):
        cur = [logits(*u) for u in units[i:i + ATTN_GROUP]]
        for t in prev:
            outputs(t)
        for t in cur:
            softmax(t)
        prev = cur
    for t in prev:
        outputs(t)

    mx = jnp.maximum(jnp.maximum(l_s[0], l_s[1]), l_s[2])
    num = jnp.zeros((S, V7X_LANES), F32)
    den = jnp.zeros((S, V7X_LANES), F32)
    for p in range(len(DILATIONS)):
        w = jnp.exp2(l_s[p] - mx)
        num = num + w * o_s[p]
        den = den + w
    y_ref[0] = (num / den).astype(y_ref.dtype)


def _attn(a_q, a_k, a_v, diag):
    B, S, W = a_q.shape
    P = ATTN_HEADS // 2
    NP = len(DILATIONS)
    pair = lambda b, p: (b, 0, p)
    blk = pl.BlockSpec((1, S, V7X_LANES), pair)
    return pl.pallas_call(
        _attn_kernel,
        grid=(B, P),
        in_specs=[blk, blk, blk, pl.BlockSpec((1,) + diag.shape[1:], lambda b, p: (p, 0, 0))],
        out_specs=blk,
        out_shape=jax.ShapeDtypeStruct((B, S, W), BF16),
        scratch_shapes=[pltpu.VMEM((NP, S, V7X_LANES), BF16)] * 5 + [pltpu.VMEM((NP, S, V7X_LANES), F32)] * 2
                       + [pltpu.VMEM((P, diag.shape[1], ATTN_QBLK, ATTN_KBLK), F32)],
        compiler_params=_params(("arbitrary", "arbitrary"), V7X_VMEM_LIMIT),
        name="attn",
    )(a_q, a_k, a_v, diag)


def _outproj_kernel(ym_ref, ya_ref, x_ref, g1_ref, w1_ref, w2_ref, o_ref):
    mix = _dot(ym_ref[0], w1_ref[...]) + _dot(ya_ref[0], w2_ref[...])
    o_ref[0] = x_ref[0] + g1_ref[0] * mix


def _outproj(y_m, y_a, x, gate1, w_out):
    B, S, D = x.shape
    Wm = y_m.shape[-1]
    tm = 512
    w1 = w_out[:Wm].astype(BF16)
    w2 = w_out[Wm:].astype(BF16)
    row = lambda b, i: (b, i, 0)
    const = lambda b, i: (0, 0)
    return pl.pallas_call(
        _outproj_kernel,
        grid=(B, S // tm),
        in_specs=[pl.BlockSpec((1, tm, Wm), row),
                  pl.BlockSpec((1, tm, y_a.shape[-1]), row),
                  pl.BlockSpec((1, tm, D), row),
                  pl.BlockSpec((1, 1, D), lambda b, i: (b, 0, 0)),
                  pl.BlockSpec(w1.shape, const),
                  pl.BlockSpec(w2.shape, const)],
        out_specs=pl.BlockSpec((1, tm, D), row),
        out_shape=jax.ShapeDtypeStruct((B, S, D), F32),
        compiler_params=_params(("parallel", "arbitrary"), V7X_VMEM_LIMIT),
        name="outproj",
    )(y_m, y_a, x, gate1, w1, w2)


def _select_kernel(x_ref, sc_ref, sh_ref, g_ref, whl_ref, br_ref, tri_ref,
                   h_ref, pos_ref, gate_ref, off_ref, *, cap):
    S = x_ref.shape[1]
    NE = N_EXPERTS
    h = _modulated_norm(x_ref[0], g_ref[...], sc_ref[0], sh_ref[0])
    hi, lo = _split_bf16(h)
    h_ref[0] = hi
    both = _dot(hi, whl_ref[...])
    logits = both[:, :V7X_LANES] + both[:, V7X_LANES:] + _dot(lo, whl_ref[:, :V7X_LANES])
    lt = logits.T[:NE, :] + br_ref[...]
    ex = jnp.exp(lt - jnp.max(lt, axis=0, keepdims=True))
    aff = ex / jnp.sum(ex, axis=0, keepdims=True)
    gate_ref[0] = aff

    bits = pltpu.bitcast(aff, I32)

    def count_ge(cand):
        return jnp.sum((bits >= cand).astype(F32), axis=1, keepdims=True)

    def search(i, v):
        shift = 27 - 3 * i
        best = v
        for c in range(1, 8):
            cand = v | lax.shift_left(jnp.int32(c), shift)
            best = jnp.where(count_ge(cand) >= cap, cand, best)
        return best

    top = jnp.full((NE, 1), 1 << 30, I32)
    thr = lax.fori_loop(0, 10, search, jnp.where(count_ge(top) >= cap, top, 0))
    gt = (bits > thr).astype(F32)
    eq = (bits == thr).astype(F32)
    need = cap - jnp.sum(gt, axis=1, keepdims=True)

    def prefix_count(mask):
        off = jnp.zeros((NE, 1), F32)
        parts, starts = [], []
        for j in range(S // V7X_LANES):
            t = mask[:, j * V7X_LANES:(j + 1) * V7X_LANES]
            starts.append(off)
            parts.append(_dot(t.astype(BF16), tri_ref[...]) + off)
            off = off + jnp.sum(t, axis=1, keepdims=True)
        return jnp.concatenate(parts, axis=1), starts + [off]

    sel = jnp.maximum(gt, jnp.where(prefix_count(eq)[0] < need, eq, 0.0))
    slot, starts = prefix_count(sel)
    pos_ref[0] = jnp.where(sel > 0.0, slot, -1.0).astype(I32)
    step = MOE_TOKEN_TILE // V7X_LANES
    off_ref[0] = jnp.concatenate(starts[::step], axis=1).astype(I32)


def _select(x1, scale, shift, g, w_router, b_router):
    B, S, D = x1.shape
    NE = N_EXPERTS
    cap = (EC_CAPACITY_FACTOR * S) // NE
    nt = S // MOE_TOKEN_TILE
    wpad = jnp.zeros((D, V7X_LANES), F32).at[:, :NE].set(w_router)
    whl = jnp.concatenate(_split_bf16(wpad), axis=1)
    ti = jnp.arange(V7X_LANES)
    tri = (ti[:, None] < ti[None, :]).astype(BF16)
    vec = lambda b: (b, 0, 0)
    const = lambda b: (0, 0)
    return pl.pallas_call(
        functools.partial(_select_kernel, cap=cap),
        grid=(B,),
        in_specs=[pl.BlockSpec((1, S, D), vec),
                  pl.BlockSpec((1, 1, D), vec),
                  pl.BlockSpec((1, 1, D), vec),
                  pl.BlockSpec((1, D), const),
                  pl.BlockSpec((D, 2 * V7X_LANES), const),
                  pl.BlockSpec((NE, 1), const),
                  pl.BlockSpec((V7X_LANES, V7X_LANES), const)],
        out_specs=[pl.BlockSpec((1, S, D), vec),
                   pl.BlockSpec((1, NE, S), vec),
                   pl.BlockSpec((1, NE, S), vec),
                   pl.BlockSpec((1, NE, nt + 1), vec)],
        out_shape=[jax.ShapeDtypeStruct((B, S, D), BF16),
                   jax.ShapeDtypeStruct((B, NE, S), I32),
                   jax.ShapeDtypeStruct((B, NE, S), F32),
                   jax.ShapeDtypeStruct((B, NE, nt + 1), I32)],
        compiler_params=_params(("parallel",), V7X_VMEM_LIMIT),
        name="select",
    )(x1, scale, shift, g, whl, b_router.reshape(NE, 1), tri)


def _window_start(off, cap, rows):
    return pl.multiple_of(jnp.minimum((off // 16) * 16, cap - rows), 16)


def _gather_kernel(off_ref, h_ref, pos_ref, xin_ref):
    NE, _, cap, D = xin_ref.shape
    TT = h_ref.shape[1]
    W = MOE_WINDOW
    b, j = pl.program_id(0), pl.program_id(1)
    nt = pl.num_programs(1)

    @pl.when(j == 0)
    def _():
        xin_ref[...] = jnp.zeros(xin_ref.shape, xin_ref.dtype)

    base = (b * NE) * (nt + 1) + j
    offs = [off_ref[base + e * (nt + 1)] for e in range(NE)]
    ends = [off_ref[base + e * (nt + 1) + 1] for e in range(NE)]
    h = h_ref[0]

    def onehot(e, start, rows):
        slot = start + lax.broadcasted_iota(I32, (rows, TT), 0)
        return jnp.where(pos_ref[0, e:e + 1, :] == slot, 1.0, 0.0).astype(BF16)

    def place(e, start, rows, new):
        slot = start + lax.broadcasted_iota(I32, (rows, 1), 0)
        old = xin_ref[e, 0, pl.ds(start, rows), :].astype(F32)
        xin_ref[e, 0, pl.ds(start, rows), :] = jnp.where(slot >= offs[e], new, old).astype(xin_ref.dtype)

    starts = [_window_start(offs[e], cap, W) for e in range(NE)]
    res = _dot(jnp.concatenate([onehot(e, starts[e], W) for e in range(NE)], axis=0), h)
    for e in range(NE):
        place(e, starts[e], W, res[e * W:(e + 1) * W])

    n_extra = [(jnp.maximum(ends[e] - starts[e] - W, 0) + W - 1) // W for e in range(NE)]

    @pl.when(functools.reduce(jnp.maximum, n_extra) > 0)
    def _():
        for e in range(NE):
            def extra(c, carry, e=e):
                start = _window_start(starts[e] + W + c * W, cap, W)
                place(e, start, W, _dot(onehot(e, start, W), h_ref[0]))
                return carry

            lax.fori_loop(0, n_extra[e], extra, 0)


def _gather(h2, pos, offs):
    B, S, D = h2.shape
    NE = N_EXPERTS
    cap = (EC_CAPACITY_FACTOR * S) // NE
    TT = MOE_TOKEN_TILE
    grid_spec = pltpu.PrefetchScalarGridSpec(
        num_scalar_prefetch=1,
        grid=(B, S // TT),
        in_specs=[pl.BlockSpec((1, TT, D), lambda b, j, o: (b, j, 0)),
                  pl.BlockSpec((1, NE, TT), lambda b, j, o: (b, 0, j))],
        out_specs=pl.BlockSpec((NE, 1, cap, D), lambda b, j, o: (0, b, 0, 0)),
    )
    return pl.pallas_call(
        _gather_kernel,
        grid_spec=grid_spec,
        out_shape=jax.ShapeDtypeStruct((NE, B, cap, D), BF16),
        compiler_params=_params(("parallel", "arbitrary"), V7X_VMEM_LIMIT),
        name="gather",
    )(offs.reshape(-1), h2, pos)


def _expert_kernel(x_ref, wg_ref, wu_ref, wd_ref, o_ref, acc_s, wg_s, wu_s, wd_s, *, row_tile):
    f = pl.program_id(2)
    nf = pl.num_programs(2)
    n_tiles = x_ref.shape[1] // row_tile

    def sweep(first, last):
        wg_s[...] = wg_ref[0].astype(BF16)
        wu_s[...] = wu_ref[0].astype(BF16)
        wd_s[...] = wd_ref[0].astype(BF16)

        def down(r, hid):
            y = _dot(hid, wd_s[...])
            if not first:
                y = y + acc_s[r, :]
            if last:
                o_ref[0, r, :] = y.astype(o_ref.dtype)
            else:
                acc_s[r, :] = y

        pending = None
        for i in range(n_tiles):
            r = slice(i * row_tile, (i + 1) * row_tile)
            xb = x_ref[0, r, :]
            g = _dot(xb, wg_s[...])
            u = _dot(xb, wu_s[...])
            if pending is not None:
                down(*pending)
            pending = (r, (g * _sigmoid(g) * u).astype(BF16))
        down(*pending)

    @pl.when(f == 0)
    def _():
        sweep(True, False)

    @pl.when(jnp.logical_and(f > 0, f < nf - 1))
    def _():
        sweep(False, False)

    @pl.when(f == nf - 1)
    def _():
        sweep(False, True)


def _experts(xin, w_gate, w_up, w_down):
    NE, R, D = xin.shape
    F = w_gate.shape[-1]
    tr = min(R, 2048)
    tf = 512
    row_tile = min(tr, 512)
    return pl.pallas_call(
        functools.partial(_expert_kernel, row_tile=row_tile),
        grid=(NE, R // tr, F // tf),
        in_specs=[pl.BlockSpec((1, tr, D), lambda e, r, f: (e, r, 0)),
                  pl.BlockSpec((1, D, tf), lambda e, r, f: (e, 0, f)),
                  pl.BlockSpec((1, D, tf), lambda e, r, f: (e, 0, f)),
                  pl.BlockSpec((1, tf, D), lambda e, r, f: (e, f, 0))],
        out_specs=pl.BlockSpec((1, tr, D), lambda e, r, f: (e, r, 0)),
        out_shape=jax.ShapeDtypeStruct((NE, R, D), BF16),
        scratch_shapes=[pltpu.VMEM((tr, D), F32), pltpu.VMEM((D, tf), BF16),
                        pltpu.VMEM((D, tf), BF16), pltpu.VMEM((tf, D), BF16)],
        compiler_params=_params(("parallel", "parallel", "arbitrary"), V7X_VMEM_LIMIT),
        name="experts",
    )(xin, w_gate, w_up, w_down)


def _combine_kernel(off_ref, y_ref, pos_ref, gate_ref, x_ref, g2_ref, o_ref):
    NE, _, cap, D = y_ref.shape
    TT = x_ref.shape[1]
    W = MOE_WINDOW
    b, j = pl.program_id(0), pl.program_id(1)
    nt = pl.num_programs(1)
    base = (b * NE) * (nt + 1) + j
    offs = [off_ref[base + e * (nt + 1)] for e in range(NE)]
    ends = [off_ref[base + e * (nt + 1) + 1] for e in range(NE)]
    starts = [_window_start(offs[e], cap, W) for e in range(NE)]

    def weighted_onehot(e, start, rows, lo):
        slot = start + lax.broadcasted_iota(I32, (rows, TT), 0)
        hit = (pos_ref[0, e:e + 1, :] == slot) & (slot >= lo)
        return jnp.where(hit, gate_ref[0, e:e + 1, :], 0.0).astype(BF16)

    scatter = jnp.concatenate([weighted_onehot(e, starts[e], W, 0) for e in range(NE)], axis=0)
    ystack = jnp.concatenate([y_ref[e, 0, pl.ds(starts[e], W), :] for e in range(NE)], axis=0)
    o_ref[0] = x_ref[0] + g2_ref[0] * _dot_tn(scatter, ystack)

    n_extra = [(jnp.maximum(ends[e] - starts[e] - W, 0) + W - 1) // W for e in range(NE)]

    @pl.when(functools.reduce(jnp.maximum, n_extra) > 0)
    def _():
        for e in range(NE):
            def extra(c, carry, e=e):
                lo = starts[e] + W + c * W
                start = _window_start(lo, cap, W)
                part = _dot_tn(weighted_onehot(e, start, W, lo), y_ref[e, 0, pl.ds(start, W), :])
                o_ref[0] = o_ref[0] + g2_ref[0] * part
                return carry

            lax.fori_loop(0, n_extra[e], extra, 0)


def _combine(y, pos, gate, offs, x1, gate2):
    B, S, D = x1.shape
    NE, _, cap, _ = y.shape
    TT = MOE_TOKEN_TILE
    row = lambda b, j, o: (b, j, 0)
    grid_spec = pltpu.PrefetchScalarGridSpec(
        num_scalar_prefetch=1,
        grid=(B, S // TT),
        in_specs=[pl.BlockSpec((NE, 1, cap, D), lambda b, j, o: (0, b, 0, 0)),
                  pl.BlockSpec((1, NE, TT), lambda b, j, o: (b, 0, j)),
                  pl.BlockSpec((1, NE, TT), lambda b, j, o: (b, 0, j)),
                  pl.BlockSpec((1, TT, D), row),
                  pl.BlockSpec((1, 1, D), lambda b, j, o: (b, 0, 0))],
        out_specs=pl.BlockSpec((1, TT, D), row),
    )
    return pl.pallas_call(
        _combine_kernel,
        grid_spec=grid_spec,
        out_shape=jax.ShapeDtypeStruct((B, S, D), F32),
        compiler_params=_params(("parallel", "arbitrary"), V7X_VMEM_LIMIT),
        name="combine",
    )(offs.reshape(-1), y, pos, gate, x1, gate2)


def _inproj_weight(w_in):
    D = w_in.shape[0]
    main = jnp.concatenate([w_in[:, :1024], w_in[:, 1040:2576], w_in[:, 1024:1040],
                            jnp.zeros((D, V7X_LANES - 16), w_in.dtype)], axis=1)
    return main.astype(BF16)


def kernel(x, c, w_ada, b_ada, norm1_g, w_in, conv_w, conv_b, w_q_blk, w_k_blk, w_v_blk, b_igate, b_fgate,
           mlstm_norm_g, mlstm_skip, q_norm_g, k_norm_g, rel_bias, w_out, norm2_g, w_router, b_router,
           w_gate, w_up, w_down):
    B, S, D = x.shape
    depth = w_ada.shape[0]
    diag = _attn_bias_diagonals(rel_bias)
    for l in range(depth):
        mod = _ada(c, w_ada[l], b_ada[l])
        shift1, scale1, gate1, shift2, scale2, gate2 = (
            mod[:, i * D:(i + 1) * D].reshape(B, 1, D) for i in range(N_MOD))

        x_m, o_pre, a_q, a_k, a_v, gates = _inproj(x, scale1, shift1, norm1_g[l].reshape(1, D),
                                                   _inproj_weight(w_in[l]), q_norm_g[l], k_norm_g[l])
        y_m = _mlstm(x_m, o_pre, gates, conv_w[l], conv_b[l], w_q_blk[l], w_k_blk[l], w_v_blk[l],
                     b_igate[l], b_fgate[l], mlstm_norm_g[l], mlstm_skip[l])
        y_a = _attn(a_q, a_k, a_v, diag)
        x1 = _outproj(y_m, y_a, x, gate1, w_out[l])

        h2, pos, aff, offs = _select(x1, scale2, shift2, norm2_g[l].reshape(1, D), w_router[l], b_router[l])
        xin = _gather(h2, pos, offs)
        NE, _, cap, _ = xin.shape
        y = _experts(xin.reshape(NE, B * cap, D), w_gate[l], w_up[l], w_down[l]).reshape(NE, B, cap, D)
        x = _combine(y, pos, aff, offs, x1, gate2)
    return x
```

```python
import functools
import math

import numpy as np
import jax
import jax.numpy as jnp
from jax import lax
from jax.experimental import pallas as pl
from jax.experimental.pallas import tpu as pltpu

F32 = jnp.float32
BF16 = jnp.bfloat16
I32 = jnp.int32

NORM_EPS = 1e-6
N_MOD = 6
MLSTM_HEADS = 4
MLSTM_HEAD_DIM = 128
MLSTM_QKV_BLOCK = 4
MLSTM_CONV = 5
MLSTM_CHUNK = 128
MLSTM_UNROLL = 8
MLSTM_ONES_ROWS = 16
ATTN_HEADS = 8
ATTN_HEAD_DIM = 64
ATTN_WIDTH = ATTN_HEADS * ATTN_HEAD_DIM
DILATIONS = (1, 4, 16)
ATTN_HALF = 64
ATTN_QBLK = 128
ATTN_KBLK = 256
ATTN_DIAG = 512
ATTN_GROUP = 2
LOG2E = math.log2(math.e)
REL_BUCKETS = 32
REL_MAX_DIST = 1024
N_EXPERTS = 16
EC_CAPACITY_FACTOR = 2
MOE_TOKEN_TILE = 256
MOE_STEP_TILES = 2
MOE_WINDOW = 64
NEG_BIG = -1e30

V7X_LANES = 128
V7X_VMEM_LIMIT = 56 * 1024 * 1024


def _sigmoid(x):
    return 1.0 / (1.0 + jnp.exp(-x))


def _dot(a, b):
    return jnp.dot(a, b, preferred_element_type=F32)


def _dot_nt(a, b):
    return lax.dot_general(a, b, (((1,), (1,)), ((), ())), preferred_element_type=F32)


def _dot_tn(a, b):
    return lax.dot_general(a, b, (((0,), (0,)), ((), ())), preferred_element_type=F32)


def _split_bf16(x):
    hi = x.astype(BF16)
    lo = (x - hi.astype(F32)).astype(BF16)
    return hi, lo


def _params(sem, vmem=None):
    return pltpu.CompilerParams(dimension_semantics=sem, vmem_limit_bytes=vmem)


def _ada_kernel(c_ref, w_ref, b_ref, o_ref):
    c = c_ref[...]
    s = c * _sigmoid(c)
    o_ref[...] = jnp.dot(s, w_ref[...], preferred_element_type=F32,
                         precision=lax.Precision.HIGHEST) + b_ref[...]


def _ada(c, w, b):
    B, D = c.shape
    N = w.shape[1]
    tn = 1024
    return pl.pallas_call(
        _ada_kernel,
        grid=(N // tn,),
        in_specs=[pl.BlockSpec((B, D), lambda j: (0, 0)),
                  pl.BlockSpec((D, tn), lambda j: (0, j)),
                  pl.BlockSpec((1, tn), lambda j: (0, j))],
        out_specs=pl.BlockSpec((B, tn), lambda j: (0, j)),
        out_shape=jax.ShapeDtypeStruct((B, N), F32),
        compiler_params=_params(("arbitrary",)),
        name="ada",
    )(c, w, b.reshape(1, N))


def _modulated_norm(x, g, scale, shift):
    ms = jnp.mean(x * x, axis=-1, keepdims=True)
    return x * lax.rsqrt(ms + NORM_EPS) * (g * (1.0 + scale)) + shift


def _inproj_kernel(x_ref, sc_ref, sh_ref, g_ref, w_ref, hm_ref, qg_ref, kg_ref,
                   xm_ref, op_ref, q_ref, k_ref, v_ref, gt_ref):
    h = _modulated_norm(x_ref[0], g_ref[...], sc_ref[0], sh_ref[0]).astype(BF16)
    col = lambda i: _dot(h, w_ref[:, 512 * i:512 * (i + 1)])
    q = col(2)
    k = col(3)
    xm_ref[0] = col(0)
    op_ref[0] = col(1)
    v_ref[0] = col(4)
    gt_ref[0] = _dot(h, w_ref[:, 2560:2688]).T[:16, :]

    def head_norm(t, g):
        ms = _dot((t * t).astype(BF16), hm_ref[...])
        return t * lax.rsqrt(ms + NORM_EPS) * g

    q_ref[0] = head_norm(q, qg_ref[...])
    k_ref[0] = head_norm(k, kg_ref[...])


def _inproj(x, scale, shift, g, w, q_norm_g, k_norm_g):
    B, S, D = x.shape
    tm = 512
    W = ATTN_WIDTH
    hid = jnp.arange(W) // ATTN_HEAD_DIM
    head_mean = jnp.where(hid[:, None] == hid[None, :], 1.0 / ATTN_HEAD_DIM, 0.0).astype(BF16)
    qg = jnp.tile(q_norm_g, ATTN_HEADS).reshape(1, W) * (LOG2E / math.sqrt(ATTN_HEAD_DIM))
    kg = jnp.tile(k_norm_g, ATTN_HEADS).reshape(1, W)
    row = lambda b, i: (b, i, 0)
    vec = lambda b, i: (b, 0, 0)
    const = lambda b, i: (0, 0)
    outs = ([jax.ShapeDtypeStruct((B, S, 512), F32)] * 2 + [jax.ShapeDtypeStruct((B, S, W), F32)] * 3
            + [jax.ShapeDtypeStruct((B, 16, S), F32)])
    return pl.pallas_call(
        _inproj_kernel,
        grid=(B, S // tm),
        in_specs=[pl.BlockSpec((1, tm, D), row),
                  pl.BlockSpec((1, 1, D), vec),
                  pl.BlockSpec((1, 1, D), vec),
                  pl.BlockSpec((1, D), const),
                  pl.BlockSpec(w.shape, const),
                  pl.BlockSpec((W, W), const),
                  pl.BlockSpec((1, W), const),
                  pl.BlockSpec((1, W), const)],
        out_specs=[pl.BlockSpec((1, tm, 512), row)] * 5 + [pl.BlockSpec((1, 16, tm), lambda b, i: (b, 0, i))],
        out_shape=outs,
        compiler_params=_params(("parallel", "arbitrary"), V7X_VMEM_LIMIT),
        name="inproj",
    )(x, scale, shift, g, w, head_mean, qg, kg)


def _chunk_scan(x, op, reverse):
    n = x.shape[1]
    idx = lax.broadcasted_iota(I32, x.shape, 1) & (MLSTM_CHUNK - 1)
    s = 1
    while s < MLSTM_CHUNK:
        if reverse:
            r = pltpu.roll(x, n - s, 1)
            x = jnp.where(idx < MLSTM_CHUNK - s, op(x, r), x)
        else:
            r = pltpu.roll(x, s, 1)
            x = jnp.where(idx >= s, op(x, r), x)
        s *= 2
    return x


def _log_sigmoid(x):
    return jnp.minimum(x, 0.0) - jnp.log(1.0 + jnp.exp(-jnp.abs(x)))


def _mlstm_kernel(xm_ref, op_ref, grow_ref, brow_ref, cw_ref, cb_ref,
                  wqt_ref, wk_ref, wvt_ref, ng_ref, sk_ref, y_ref,
                  qt_s, k_s, vat_s, xc_s, hf_s, hb_s, row_s, col_s):
    S = xm_ref.shape[1]
    L = MLSTM_CHUNK
    NC = S // L
    E = MLSTM_HEAD_DIM
    A = E + MLSTM_ONES_ROWS
    xm = xm_ref[0]

    rows = lax.broadcasted_iota(I32, xm.shape, 0)
    half = MLSTM_CONV // 2
    conv = xm * cw_ref[half:half + 1, :]
    for j in range(MLSTM_CONV):
        off = j - half
        if off == 0:
            continue
        shifted = pltpu.roll(xm, (-off) % S, 0)
        valid = rows >= -off if off < 0 else rows < S - off
        conv = conv + jnp.where(valid, shifted, 0.0) * cw_ref[j:j + 1, :]
    conv = conv + cb_ref[...]
    xc = conv * _sigmoid(conv)
    xc_s[...] = xc

    xcb = xc.astype(BF16)
    qt_s[...] = _dot_nt(wqt_ref[0], xcb).astype(BF16)
    k_s[...] = (_dot(xcb, wk_ref[0]) * (1.0 / math.sqrt(E))).astype(BF16)
    vat_s[:E, :] = _dot_nt(wvt_ref[0], xm.astype(BF16))
    vat_s[E:, :] = jnp.ones((MLSTM_ONES_ROWS, S), F32)

    gr = grow_ref[0, 0] + brow_ref[0]
    kind = lax.broadcasted_iota(I32, gr.shape, 0)
    gr = jnp.where((kind & 1) == 1, _log_sigmoid(gr), gr)
    b_f = _chunk_scan(gr[1:2], jnp.add, False)
    b_b = _chunk_scan(gr[3:4], jnp.add, True)
    u_f = gr[0:1] - b_f
    u_b = gr[2:3] - b_b
    m_f = _chunk_scan(u_f, jnp.maximum, False)
    m_b = _chunk_scan(u_b, jnp.maximum, True)
    row_s[...] = jnp.concatenate([b_f, m_f, u_f, gr[1:2], b_b, m_b, u_b, gr[3:4]], axis=0)
    for k, stat in enumerate((u_f, u_b)):
        for c in range(NC):
            col_s[k, c * L:(c + 1) * L, :] = jnp.broadcast_to(stat[:, c * L:(c + 1) * L], (L, L)).T

    kj = lax.broadcasted_iota(I32, (L, L), 0)
    qi = lax.broadcasted_iota(I32, (L, L), 1)
    causal = (kj <= qi, kj >= qi)

    def local_part(c, dirn):
        r0 = pl.multiple_of(c * L, L)
        qt = qt_s[:, pl.ds(r0, L)]
        kc = k_s[pl.ds(r0, L), :]
        vat = vat_s[:, pl.ds(r0, L)]
        rr = row_s[:, pl.ds(r0, L)]
        b_r, m_r, u_r = (rr[4 * dirn + i:4 * dirn + i + 1, :] for i in range(3))
        g = jnp.sum(rr[4 * dirn + 3:4 * dirn + 4, :], axis=1, keepdims=True)
        m_loc = g + jnp.max(u_r, axis=1, keepdims=True)
        return dict(
            r0=r0, dirn=dirn, qt=qt, vat=vat.astype(BF16), b_r=b_r, m_r=m_r, g=g, m_loc=m_loc,
            u_c=col_s[dirn, pl.ds(r0, L), :],
            s=_dot(kc, qt),
            d_state=_dot((vat * jnp.exp(g + u_r - m_loc)).astype(BF16), kc))

    def intra_part(t):
        p = jnp.exp(jnp.where(causal[t["dirn"]], t["u_c"] - t["m_r"], NEG_BIG)) * t.pop("s")
        t["y_loc"] = _dot(t["vat"], p.astype(BF16))

    def state_part(t, state):
        Ct, m = state
        x_st = _dot(Ct.astype(BF16), t["qt"])
        mm = jnp.maximum(m, t["m_r"])
        z = jnp.exp(m - mm) * x_st + jnp.exp(t["m_r"] - mm) * t["y_loc"]
        h = z[:E, :] / jnp.maximum(jnp.abs(z[E:E + 1, :]), jnp.exp(-t["b_r"] - mm))
        m_new = jnp.maximum(t["g"] + m, t["m_loc"])
        Ct_new = jnp.exp(t["g"] + m - m_new) * Ct + jnp.exp(t["m_loc"] - m_new) * t["d_state"]
        return h, (Ct_new, m_new)

    def body(i, carry):
        states = list(carry)
        out_s = (hf_s, hb_s)
        work = []
        for k in range(MLSTM_UNROLL):
            c = i * MLSTM_UNROLL + k
            work += [local_part(c, 0), local_part(NC - 1 - c, 1)]
        for t in work:
            intra_part(t)
        for t in work:
            h, states[t["dirn"]] = state_part(t, states[t["dirn"]])
            out_s[t["dirn"]][:, pl.ds(t["r0"], L)] = h
        return tuple(states)

    init = (jnp.zeros((A, E), F32), jnp.zeros((1, 1), F32))
    lax.fori_loop(0, NC // MLSTM_UNROLL, body, (init, init))

    h = (hf_s[...] + hb_s[...]).T
    hn = h * lax.rsqrt(jnp.mean(h * h, axis=-1, keepdims=True) + NORM_EPS) * ng_ref[...]
    y = (hn + sk_ref[...] * xc_s[...]) * _sigmoid(op_ref[0])
    y_ref[0] = y.astype(y_ref.dtype)


def _blockdiag_dense(w_blk):
    nblk = w_blk.shape[0]
    per_head = nblk // MLSTM_HEADS
    w = w_blk.reshape(MLSTM_HEADS, per_head, MLSTM_QKV_BLOCK, MLSTM_QKV_BLOCK)
    eye = jnp.eye(per_head, dtype=w.dtype)
    dense = jnp.einsum('hgij,gk->hgikj', w, eye)
    return dense.reshape(MLSTM_HEADS, MLSTM_HEAD_DIM, MLSTM_HEAD_DIM)


def _mlstm(x_m, o_pre, gates, conv_w, conv_b, w_q_blk, w_k_blk, w_v_blk, b_igate, b_fgate, norm_g, skip):
    B, S, W = x_m.shape
    H, E = MLSTM_HEADS, MLSTM_HEAD_DIM
    assert S % (MLSTM_CHUNK * MLSTM_UNROLL) == 0
    grow = gates.reshape(B, H, 4, S)
    bk = jnp.stack([b_igate[0], b_fgate[0], b_igate[1], b_fgate[1]], axis=0)
    brow = bk.T.reshape(H, 4, 1)
    wqt = _blockdiag_dense(w_q_blk).transpose(0, 2, 1).astype(BF16)
    wk = _blockdiag_dense(w_k_blk).astype(BF16)
    wvt = _blockdiag_dense(w_v_blk).transpose(0, 2, 1).astype(BF16)
    headcol = lambda b, h: (b, 0, h)
    perhead = lambda b, h: (h, 0, 0)
    lanes = lambda b, h: (0, h)
    return pl.pallas_call(
        _mlstm_kernel,
        grid=(B, H),
        in_specs=[pl.BlockSpec((1, S, E), headcol),
                  pl.BlockSpec((1, S, E), headcol),
                  pl.BlockSpec((1, 1, 4, S), lambda b, h: (b, h, 0, 0)),
                  pl.BlockSpec((1, 4, 1), perhead),
                  pl.BlockSpec((MLSTM_CONV, E), lanes),
                  pl.BlockSpec((1, E), lanes),
                  pl.BlockSpec((1, E, E), perhead),
                  pl.BlockSpec((1, E, E), perhead),
                  pl.BlockSpec((1, E, E), perhead),
                  pl.BlockSpec((1, E), lanes),
                  pl.BlockSpec((1, E), lanes)],
        out_specs=pl.BlockSpec((1, S, E), headcol),
        out_shape=jax.ShapeDtypeStruct((B, S, W), BF16),
        scratch_shapes=[pltpu.VMEM((E, S), BF16), pltpu.VMEM((S, E), BF16),
                        pltpu.VMEM((E + MLSTM_ONES_ROWS, S), F32),
                        pltpu.VMEM((S, E), F32), pltpu.VMEM((E, S), F32), pltpu.VMEM((E, S), F32),
                        pltpu.VMEM((8, S), F32), pltpu.VMEM((2, S, E), F32)],
        compiler_params=_params(("parallel", "arbitrary"), V7X_VMEM_LIMIT),
        name="mlstm",
    )(x_m, o_pre, grow, brow, conv_w, conv_b.reshape(1, W), wqt, wk, wvt,
      norm_g.reshape(1, W), skip.reshape(1, W))


def _t5_bucket_static(rel):
    half = REL_BUCKETS // 2
    exact = half // 2
    n = np.abs(rel)
    log_ratio = (np.log(np.maximum(n, 1).astype(np.float32) / np.float32(exact))
                 / np.float32(math.log(REL_MAX_DIST / exact)))
    large = np.minimum(exact + (log_ratio * np.float32(half - exact)).astype(np.int32), half - 1)
    return np.where(rel > 0, half, 0) + np.where(n < exact, n, large)


def _attn_bias_diagonals(rel_bias):
    n = ATTN_DIAG
    nv = 3 * len(DILATIONS)
    x = np.arange(n)
    offset = np.where(x <= ATTN_KBLK, x, x - n)
    rel = offset[None, :] - ATTN_HALF * np.arange(3)[:, None]
    valid = np.tile(np.abs(rel) <= ATTN_HALF, (len(DILATIONS), 1))
    bucket = np.concatenate([_t5_bucket_static(rel * d) for d in DILATIONS], axis=0)
    onehot = (bucket[..., None] == np.arange(REL_BUCKETS)) & valid[..., None]
    w = jnp.einsum('vnb,bh->vhn', jnp.asarray(onehot, F32), rel_bias.astype(F32) * LOG2E,
                   precision=lax.Precision.HIGHEST)
    w = jnp.where(jnp.asarray(valid)[:, None, :], w, NEG_BIG)
    w = w.reshape(nv, ATTN_HEADS // 2, 2, n).transpose(1, 0, 2, 3)
    return w.reshape(ATTN_HEADS // 2, 2 * nv, n)


def _attn_kernel(q_ref, k_ref, v_ref, diag_ref, y_ref, q0_s, q1_s, k_s, v0_s, v1_s, o_s, l_s, bias_s):
    S = q_ref.shape[1]
    QB, KB = ATTN_QBLK, ATTN_KBLK
    lane = lax.broadcasted_iota(I32, (1, V7X_LANES), 1)
    in_head = (lane < ATTN_HEAD_DIM, lane >= ATTN_HEAD_DIM)
    q_s = (q0_s, q1_s)
    v_s = (v0_s, v1_s)

    pair = pl.program_id(1)

    @pl.when(pl.program_id(0) == 0)
    def _():
        for i in range(bias_s.shape[1]):
            rows_i = jnp.broadcast_to(diag_ref[0, i:i + 1, :], (QB, ATTN_DIAG))
            bias_s[pair, i] = pltpu.roll(rows_i, 0, 1, stride=1, stride_axis=0)[:, :KB]

    def strided(start, size, d):
        return pl.ds(start, size) if d == 1 else pl.ds(start, size, stride=d)

    for p, d in enumerate(DILATIONS):
        L = S // d
        for r in range(d):
            src = strided(r, L, d)
            dst = slice(r * L, (r + 1) * L)
            q = q_ref[0, src, :]
            v = v_ref[0, src, :]
            k_s[p, dst, :] = k_ref[0, src, :].astype(BF16)
            for a in range(2):
                q_s[a][p, dst, :] = jnp.where(in_head[a], q, 0.0).astype(BF16)
                v_s[a][p, dst, :] = jnp.where(in_head[a], v, 1.0).astype(BF16)

    def logits(p, d, r, qb):
        L = S // d
        nqb = L // QB
        nk = min(L, KB)
        if nqb == 1 or qb == 0:
            k0, variant = 0, 0
        elif qb == nqb - 1:
            k0, variant = L - nk, 2
        else:
            k0, variant = qb * QB - ATTN_HALF, 1
        qrows = slice(r * L + qb * QB, r * L + (qb + 1) * QB)
        krows = slice(r * L + k0, r * L + k0 + nk)
        kt = k_s[p, krows, :]
        s = [_dot_nt(q_s[a][p, qrows, :], kt) + bias_s[pair, p * 6 + variant * 2 + a][:, :nk] for a in range(2)]
        return dict(p=p, krows=krows, out_rows=strided(r + d * qb * QB, QB, d), s=s)

    def softmax(t):
        t["m"] = [jnp.max(s, axis=1, keepdims=True) for s in t["s"]]
        t["e"] = [jnp.exp2(s - m).astype(BF16) for s, m in zip(t.pop("s"), t["m"])]

    def outputs(t):
        p = t["p"]
        acc = [_dot(t["e"][a], v_s[a][p, t["krows"], :]) for a in range(2)]
        num = jnp.where(in_head[0], acc[0], acc[1])
        den = pltpu.roll(jnp.where(in_head[0], acc[1], acc[0]), ATTN_HEAD_DIM, 1)
        o_s[p, t["out_rows"], :] = num / den
        l_s[p, t["out_rows"], :] = jnp.where(in_head[0], t["m"][0], t["m"][1]) + jnp.log2(den)

    units = [(p, d, r, qb) for p, d in enumerate(DILATIONS) for r in range(d) for qb in range(S // d // QB)]
    prev = []
    for i in range(0, len(units), ATTN_GROUP):
        cur = [logits(*u) for u in units[i:i + ATTN_GROUP]]
        for t in prev:
            outputs(t)
        for t in cur:
            softmax(t)
        prev = cur
    for t in prev:
        outputs(t)

    mx = jnp.maximum(jnp.maximum(l_s[0], l_s[1]), l_s[2])
    num = jnp.zeros((S, V7X_LANES), F32)
    den = jnp.zeros((S, V7X_LANES), F32)
    for p in range(len(DILATIONS)):
        w = jnp.exp2(l_s[p] - mx)
        num = num + w * o_s[p]
        den = den + w
    y_ref[0] = (num / den).astype(y_ref.dtype)


def _attn(a_q, a_k, a_v, diag):
    B, S, W = a_q.shape
    P = ATTN_HEADS // 2
    NP = len(DILATIONS)
    pair = lambda b, p: (b, 0, p)
    blk = pl.BlockSpec((1, S, V7X_LANES), pair)
    return pl.pallas_call(
        _attn_kernel,
        grid=(B, P),
        in_specs=[blk, blk, blk, pl.BlockSpec((1,) + diag.shape[1:], lambda b, p: (p, 0, 0))],
        out_specs=blk,
        out_shape=jax.ShapeDtypeStruct((B, S, W), BF16),
        scratch_shapes=[pltpu.VMEM((NP, S, V7X_LANES), BF16)] * 5 + [pltpu.VMEM((NP, S, V7X_LANES), F32)] * 2
                       + [pltpu.VMEM((P, diag.shape[1], ATTN_QBLK, ATTN_KBLK), F32)],
        compiler_params=_params(("arbitrary", "arbitrary"), V7X_VMEM_LIMIT),
        name="attn",
    )(a_q, a_k, a_v, diag)


def _outproj_kernel(ym_ref, ya_ref, x_ref, g1_ref, w1_ref, w2_ref, o_ref):
    mix = _dot(ym_ref[0], w1_ref[...]) + _dot(ya_ref[0], w2_ref[...])
    o_ref[0] = x_ref[0] + g1_ref[0] * mix


def _outproj(y_m, y_a, x, gate1, w_out):
    B, S, D = x.shape
    Wm = y_m.shape[-1]
    tm = 512
    w1 = w_out[:Wm].astype(BF16)
    w2 = w_out[Wm:].astype(BF16)
    row = lambda b, i: (b, i, 0)
    const = lambda b, i: (0, 0)
    return pl.pallas_call(
        _outproj_kernel,
        grid=(B, S // tm),
        in_specs=[pl.BlockSpec((1, tm, Wm), row),
                  pl.BlockSpec((1, tm, y_a.shape[-1]), row),
                  pl.BlockSpec((1, tm, D), row),
                  pl.BlockSpec((1, 1, D), lambda b, i: (b, 0, 0)),
                  pl.BlockSpec(w1.shape, const),
                  pl.BlockSpec(w2.shape, const)],
        out_specs=pl.BlockSpec((1, tm, D), row),
        out_shape=jax.ShapeDtypeStruct((B, S, D), F32),
        compiler_params=_params(("parallel", "arbitrary"), V7X_VMEM_LIMIT),
        name="outproj",
    )(y_m, y_a, x, gate1, w1, w2)


def _select_kernel(x_ref, sc_ref, sh_ref, g_ref, whl_ref, br_ref, tri_ref,
                   h_ref, pos_ref, gate_ref, off_ref, *, cap):
    S = x_ref.shape[1]
    NE = N_EXPERTS
    h = _modulated_norm(x_ref[0], g_ref[...], sc_ref[0], sh_ref[0])
    hi, lo = _split_bf16(h)
    h_ref[0] = hi
    both = _dot(hi, whl_ref[...])
    logits = both[:, :V7X_LANES] + both[:, V7X_LANES:] + _dot(lo, whl_ref[:, :V7X_LANES])
    lt = logits.T[:NE, :] + br_ref[...]
    ex = jnp.exp(lt - jnp.max(lt, axis=0, keepdims=True))
    aff = ex / jnp.sum(ex, axis=0, keepdims=True)
    gate_ref[0] = aff

    bits = pltpu.bitcast(aff, I32)

    def count_ge(cand):
        return jnp.sum((bits >= cand).astype(F32), axis=1, keepdims=True)

    def search(i, v):
        shift = 27 - 3 * i
        best = v
        for c in range(1, 8):
            cand = v | lax.shift_left(jnp.int32(c), shift)
            best = jnp.where(count_ge(cand) >= cap, cand, best)
        return best

    top = jnp.full((NE, 1), 1 << 30, I32)
    thr = lax.fori_loop(0, 10, search, jnp.where(count_ge(top) >= cap, top, 0))
    gt = (bits > thr).astype(F32)
    eq = (bits == thr).astype(F32)
    need = cap - jnp.sum(gt, axis=1, keepdims=True)

    def prefix_count(mask):
        off = jnp.zeros((NE, 1), F32)
        parts, starts = [], []
        for j in range(S // V7X_LANES):
            t = mask[:, j * V7X_LANES:(j + 1) * V7X_LANES]
            starts.append(off)
            parts.append(_dot(t.astype(BF16), tri_ref[...]) + off)
            off = off + jnp.sum(t, axis=1, keepdims=True)
        return jnp.concatenate(parts, axis=1), starts + [off]

    sel = jnp.maximum(gt, jnp.where(prefix_count(eq)[0] < need, eq, 0.0))
    slot, starts = prefix_count(sel)
    pos_ref[0] = jnp.where(sel > 0.0, slot, -1.0).astype(I32)
    step = MOE_TOKEN_TILE // V7X_LANES
    off_ref[0] = jnp.concatenate(starts[::step], axis=1).astype(I32)


def _select(x1, scale, shift, g, w_router, b_router):
    B, S, D = x1.shape
    NE = N_EXPERTS
    cap = (EC_CAPACITY_FACTOR * S) // NE
    nt = S // MOE_TOKEN_TILE
    wpad = jnp.zeros((D, V7X_LANES), F32).at[:, :NE].set(w_router)
    whl = jnp.concatenate(_split_bf16(wpad), axis=1)
    ti = jnp.arange(V7X_LANES)
    tri = (ti[:, None] < ti[None, :]).astype(BF16)
    vec = lambda b: (b, 0, 0)
    const = lambda b: (0, 0)
    return pl.pallas_call(
        functools.partial(_select_kernel, cap=cap),
        grid=(B,),
        in_specs=[pl.BlockSpec((1, S, D), vec),
                  pl.BlockSpec((1, 1, D), vec),
                  pl.BlockSpec((1, 1, D), vec),
                  pl.BlockSpec((1, D), const),
                  pl.BlockSpec((D, 2 * V7X_LANES), const),
                  pl.BlockSpec((NE, 1), const),
                  pl.BlockSpec((V7X_LANES, V7X_LANES), const)],
        out_specs=[pl.BlockSpec((1, S, D), vec),
                   pl.BlockSpec((1, NE, S), vec),
                   pl.BlockSpec((1, NE, S), vec),
                   pl.BlockSpec((1, NE, nt + 1), vec)],
        out_shape=[jax.ShapeDtypeStruct((B, S, D), BF16),
                   jax.ShapeDtypeStruct((B, NE, S), I32),
                   jax.ShapeDtypeStruct((B, NE, S), F32),
                   jax.ShapeDtypeStruct((B, NE, nt + 1), I32)],
        compiler_params=_params(("parallel",), V7X_VMEM_LIMIT),
        name="select",
    )(x1, scale, shift, g, whl, b_router.reshape(NE, 1), tri)


def _window_start(off, cap, rows):
    return pl.multiple_of(jnp.minimum((off // 16) * 16, cap - rows), 16)


def _gather_kernel(off_ref, h_ref, pos_ref, xin_ref):
    NE, _, cap, D = xin_ref.shape
    TT, W = MOE_TOKEN_TILE, MOE_WINDOW
    b, j = pl.program_id(0), pl.program_id(1)
    nt = pl.num_programs(1) * MOE_STEP_TILES

    @pl.when(j == 0)
    def _():
        xin_ref[...] = jnp.zeros(xin_ref.shape, xin_ref.dtype)

    for t in range(MOE_STEP_TILES):
        tok = slice(t * TT, (t + 1) * TT)
        base = (b * NE) * (nt + 1) + j * MOE_STEP_TILES + t
        offs = [off_ref[base + e * (nt + 1)] for e in range(NE)]
        ends = [off_ref[base + e * (nt + 1) + 1] for e in range(NE)]

        def onehot(e, start, tok=tok):
            slot = start + lax.broadcasted_iota(I32, (W, TT), 0)
            return jnp.where(pos_ref[0, e:e + 1, tok] == slot, 1.0, 0.0).astype(BF16)

        def place(e, start, new, offs=offs):
            slot = start + lax.broadcasted_iota(I32, (W, 1), 0)
            old = xin_ref[e, 0, pl.ds(start, W), :].astype(F32)
            xin_ref[e, 0, pl.ds(start, W), :] = jnp.where(slot >= offs[e], new, old).astype(xin_ref.dtype)

        starts = [_window_start(offs[e], cap, W) for e in range(NE)]
        res = _dot(jnp.concatenate([onehot(e, starts[e]) for e in range(NE)], axis=0), h_ref[0, tok, :])
        for e in range(NE):
            place(e, starts[e], res[e * W:(e + 1) * W])

        n_extra = [(jnp.maximum(ends[e] - starts[e] - W, 0) + W - 1) // W for e in range(NE)]

        @pl.when(functools.reduce(jnp.maximum, n_extra) > 0)
        def _(tok=tok, starts=starts, n_extra=n_extra, onehot=onehot, place=place):
            for e in range(NE):
                def extra(c, carry, e=e):
                    start = _window_start(starts[e] + W + c * W, cap, W)
                    place(e, start, _dot(onehot(e, start), h_ref[0, tok, :]))
                    return carry

                lax.fori_loop(0, n_extra[e], extra, 0)


def _gather(h2, pos, offs):
    B, S, D = h2.shape
    NE = N_EXPERTS
    cap = (EC_CAPACITY_FACTOR * S) // NE
    ts = MOE_TOKEN_TILE * MOE_STEP_TILES
    grid_spec = pltpu.PrefetchScalarGridSpec(
        num_scalar_prefetch=1,
        grid=(B, S // ts),
        in_specs=[pl.BlockSpec((1, ts, D), lambda b, j, o: (b, j, 0)),
                  pl.BlockSpec((1, NE, ts), lambda b, j, o: (b, 0, j))],
        out_specs=pl.BlockSpec((NE, 1, cap, D), lambda b, j, o: (0, b, 0, 0)),
    )
    return pl.pallas_call(
        _gather_kernel,
        grid_spec=grid_spec,
        out_shape=jax.ShapeDtypeStruct((NE, B, cap, D), BF16),
        compiler_params=_params(("parallel", "arbitrary"), V7X_VMEM_LIMIT),
        name="gather",
    )(offs.reshape(-1), h2, pos)


def _expert_kernel(x_ref, wg_ref, wu_ref, wd_ref, o_ref, acc_s, wg_s, wu_s, wd_s, *, row_tile):
    f = pl.program_id(2)
    nf = pl.num_programs(2)
    n_tiles = x_ref.shape[1] // row_tile

    def sweep(first, last):
        wg_s[...] = wg_ref[0].astype(BF16)
        wu_s[...] = wu_ref[0].astype(BF16)
        wd_s[...] = wd_ref[0].astype(BF16)

        def down(r, hid):
            y = _dot(hid, wd_s[...])
            if not first:
                y = y + acc_s[r, :]
            if last:
                o_ref[0, r, :] = y.astype(o_ref.dtype)
            else:
                acc_s[r, :] = y

        pending = None
        for i in range(n_tiles):
            r = slice(i * row_tile, (i + 1) * row_tile)
            xb = x_ref[0, r, :]
            g = _dot(xb, wg_s[...])
            u = _dot(xb, wu_s[...])
            if pending is not None:
                down(*pending)
            pending = (r, (g * _sigmoid(g) * u).astype(BF16))
        down(*pending)

    @pl.when(f == 0)
    def _():
        sweep(True, False)

    @pl.when(jnp.logical_and(f > 0, f < nf - 1))
    def _():
        sweep(False, False)

    @pl.when(f == nf - 1)
    def _():
        sweep(False, True)


def _experts(xin, w_gate, w_up, w_down):
    NE, R, D = xin.shape
    F = w_gate.shape[-1]
    tr = min(R, 2048)
    tf = 512
    row_tile = min(tr, 512)
    return pl.pallas_call(
        functools.partial(_expert_kernel, row_tile=row_tile),
        grid=(NE, R // tr, F // tf),
        in_specs=[pl.BlockSpec((1, tr, D), lambda e, r, f: (e, r, 0)),
                  pl.BlockSpec((1, D, tf), lambda e, r, f: (e, 0, f)),
                  pl.BlockSpec((1, D, tf), lambda e, r, f: (e, 0, f)),
                  pl.BlockSpec((1, tf, D), lambda e, r, f: (e, f, 0))],
        out_specs=pl.BlockSpec((1, tr, D), lambda e, r, f: (e, r, 0)),
        out_shape=jax.ShapeDtypeStruct((NE, R, D), BF16),
        scratch_shapes=[pltpu.VMEM((tr, D), F32), pltpu.VMEM((D, tf), BF16),
                        pltpu.VMEM((D, tf), BF16), pltpu.VMEM((tf, D), BF16)],
        compiler_params=_params(("parallel", "parallel", "arbitrary"), V7X_VMEM_LIMIT),
        name="experts",
    )(xin, w_gate, w_up, w_down)


def _combine_kernel(off_ref, y_ref, pos_ref, gate_ref, x_ref, g2_ref, o_ref):
    NE, _, cap, D = y_ref.shape
    TT, W = MOE_TOKEN_TILE, MOE_WINDOW
    b, j = pl.program_id(0), pl.program_id(1)
    nt = pl.num_programs(1) * MOE_STEP_TILES

    for t in range(MOE_STEP_TILES):
        tok = slice(t * TT, (t + 1) * TT)
        base = (b * NE) * (nt + 1) + j * MOE_STEP_TILES + t
        offs = [off_ref[base + e * (nt + 1)] for e in range(NE)]
        ends = [off_ref[base + e * (nt + 1) + 1] for e in range(NE)]
        starts = [_window_start(offs[e], cap, W) for e in range(NE)]

        def weighted_onehot(e, start, lo, tok=tok):
            slot = start + lax.broadcasted_iota(I32, (W, TT), 0)
            hit = (pos_ref[0, e:e + 1, tok] == slot) & (slot >= lo)
            return jnp.where(hit, gate_ref[0, e:e + 1, tok], 0.0).astype(BF16)

        scatter = jnp.concatenate([weighted_onehot(e, starts[e], 0) for e in range(NE)], axis=0)
        ystack = jnp.concatenate([y_ref[e, 0, pl.ds(starts[e], W), :] for e in range(NE)], axis=0)
        o_ref[0, tok, :] = x_ref[0, tok, :] + g2_ref[0] * _dot_tn(scatter, ystack)

        n_extra = [(jnp.maximum(ends[e] - starts[e] - W, 0) + W - 1) // W for e in range(NE)]

        @pl.when(functools.reduce(jnp.maximum, n_extra) > 0)
        def _(tok=tok, starts=starts, n_extra=n_extra, weighted_onehot=weighted_onehot):
            for e in range(NE):
                def extra(c, carry, e=e):
                    lo = starts[e] + W + c * W
                    start = _window_start(lo, cap, W)
                    part = _dot_tn(weighted_onehot(e, start, lo), y_ref[e, 0, pl.ds(start, W), :])
                    o_ref[0, tok, :] = o_ref[0, tok, :] + g2_ref[0] * part
                    return carry

                lax.fori_loop(0, n_extra[e], extra, 0)


def _combine(y, pos, gate, offs, x1, gate2):
    B, S, D = x1.shape
    NE, _, cap, _ = y.shape
    ts = MOE_TOKEN_TILE * MOE_STEP_TILES
    row = lambda b, j, o: (b, j, 0)
    grid_spec = pltpu.PrefetchScalarGridSpec(
        num_scalar_prefetch=1,
        grid=(B, S // ts),
        in_specs=[pl.BlockSpec((NE, 1, cap, D), lambda b, j, o: (0, b, 0, 0)),
                  pl.BlockSpec((1, NE, ts), lambda b, j, o: (b, 0, j)),
                  pl.BlockSpec((1, NE, ts), lambda b, j, o: (b, 0, j)),
                  pl.BlockSpec((1, ts, D), row),
                  pl.BlockSpec((1, 1, D), lambda b, j, o: (b, 0, 0))],
        out_specs=pl.BlockSpec((1, ts, D), row),
    )
    return pl.pallas_call(
        _combine_kernel,
        grid_spec=grid_spec,
        out_shape=jax.ShapeDtypeStruct((B, S, D), F32),
        compiler_params=_params(("parallel", "arbitrary"), V7X_VMEM_LIMIT),
        name="combine",
    )(offs.reshape(-1), y, pos, gate, x1, gate2)


def _inproj_weight(w_in):
    D = w_in.shape[0]
    H = MLSTM_HEADS
    gates = w_in[:, 1024:1040].reshape(D, 4, H).transpose(0, 2, 1).reshape(D, 4 * H)
    main = jnp.concatenate([w_in[:, :1024], w_in[:, 1040:2576], gates,
                            jnp.zeros((D, V7X_LANES - 16), w_in.dtype)], axis=1)
    return main.astype(BF16)


def kernel(x, c, w_ada, b_ada, norm1_g, w_in, conv_w, conv_b, w_q_blk, w_k_blk, w_v_blk, b_igate, b_fgate,
           mlstm_norm_g, mlstm_skip, q_norm_g, k_norm_g, rel_bias, w_out, norm2_g, w_router, b_router,
           w_gate, w_up, w_down):
    B, S, D = x.shape
    depth = w_ada.shape[0]
    diag = _attn_bias_diagonals(rel_bias)
    for l in range(depth):
        mod = _ada(c, w_ada[l], b_ada[l])
        shift1, scale1, gate1, shift2, scale2, gate2 = (
            mod[:, i * D:(i + 1) * D].reshape(B, 1, D) for i in range(N_MOD))

        x_m, o_pre, a_q, a_k, a_v, gates = _inproj(x, scale1, shift1, norm1_g[l].reshape(1, D),
                                                   _inproj_weight(w_in[l]), q_norm_g[l], k_norm_g[l])
        y_m = _mlstm(x_m, o_pre, gates, conv_w[l], conv_b[l], w_q_blk[l], w_k_blk[l], w_v_blk[l],
                     b_igate[l], b_fgate[l], mlstm_norm_g[l], mlstm_skip[l])
        y_a = _attn(a_q, a_k, a_v, diag)
        x1 = _outproj(y_m, y_a, x, gate1, w_out[l])

        h2, pos, aff, offs = _select(x1, scale2, shift2, norm2_g[l].reshape(1, D), w_router[l], b_router[l])
        xin = _gather(h2, pos, offs)
        NE, _, cap, _ = xin.shape
        y = _experts(xin.reshape(NE, B * cap, D), w_gate[l], w_up[l], w_down[l]).reshape(NE, B, cap, D)
        x = _combine(y, pos, aff, offs, x1, gate2)
    return x
```

```python
import functools
import math

import numpy as np
import jax
import jax.numpy as jnp
from jax import lax
from jax.experimental import pallas as pl
from jax.experimental.pallas import tpu as pltpu

F32 = jnp.float32
BF16 = jnp.bfloat16
I32 = jnp.int32

NORM_EPS = 1e-6
N_MOD = 6
MLSTM_HEADS = 4
MLSTM_HEAD_DIM = 128
MLSTM_QKV_BLOCK = 4
MLSTM_CONV = 5
MLSTM_CHUNK = 128
MLSTM_UNROLL = 8
MLSTM_ONES_ROWS = 16
ATTN_HEADS = 8
ATTN_HEAD_DIM = 64
ATTN_WIDTH = ATTN_HEADS * ATTN_HEAD_DIM
DILATIONS = (1, 4, 16)
ATTN_HALF = 64
ATTN_QBLK = 128
ATTN_KBLK = 256
ATTN_DIAG = 512
ATTN_GROUP = 2
LOG2E = math.log2(math.e)
REL_BUCKETS = 32
REL_MAX_DIST = 1024
N_EXPERTS = 16
EC_CAPACITY_FACTOR = 2
MOE_TOKEN_TILE = 256
MOE_STEP_TILES = 4
MOE_WINDOW = 64
NEG_BIG = -1e30

V7X_LANES = 128
V7X_VMEM_LIMIT = 56 * 1024 * 1024


def _sigmoid(x):
    return 1.0 / (1.0 + jnp.exp(-x))


def _dot(a, b):
    return jnp.dot(a, b, preferred_element_type=F32)


def _dot_nt(a, b):
    return lax.dot_general(a, b, (((1,), (1,)), ((), ())), preferred_element_type=F32)


def _dot_tn(a, b):
    return lax.dot_general(a, b, (((0,), (0,)), ((), ())), preferred_element_type=F32)


def _split_bf16(x):
    hi = x.astype(BF16)
    lo = (x - hi.astype(F32)).astype(BF16)
    return hi, lo


def _params(sem, vmem=None):
    return pltpu.CompilerParams(dimension_semantics=sem, vmem_limit_bytes=vmem)


def _ada_kernel(c_ref, w_ref, b_ref, o_ref):
    c = c_ref[...]
    s = c * _sigmoid(c)
    o_ref[...] = jnp.dot(s, w_ref[...], preferred_element_type=F32,
                         precision=lax.Precision.HIGHEST) + b_ref[...]


def _ada(c, w, b):
    B, D = c.shape
    N = w.shape[1]
    tn = 1024
    return pl.pallas_call(
        _ada_kernel,
        grid=(N // tn,),
        in_specs=[pl.BlockSpec((B, D), lambda j: (0, 0)),
                  pl.BlockSpec((D, tn), lambda j: (0, j)),
                  pl.BlockSpec((1, tn), lambda j: (0, j))],
        out_specs=pl.BlockSpec((B, tn), lambda j: (0, j)),
        out_shape=jax.ShapeDtypeStruct((B, N), F32),
        compiler_params=_params(("arbitrary",)),
        name="ada",
    )(c, w, b.reshape(1, N))


def _modulated_norm(x, g, scale, shift):
    ms = jnp.mean(x * x, axis=-1, keepdims=True)
    return x * lax.rsqrt(ms + NORM_EPS) * (g * (1.0 + scale)) + shift


def _inproj_kernel(x_ref, sc_ref, sh_ref, g_ref, w_ref, hm_ref, qg_ref, kg_ref,
                   xm_ref, op_ref, q_ref, k_ref, v_ref, gt_ref):
    h = _modulated_norm(x_ref[0], g_ref[...], sc_ref[0], sh_ref[0]).astype(BF16)
    col = lambda i: _dot(h, w_ref[:, 512 * i:512 * (i + 1)])
    q = col(2)
    k = col(3)
    xm_ref[0] = col(0)
    op_ref[0] = col(1)
    v_ref[0] = col(4)
    gt_ref[0] = _dot(h, w_ref[:, 2560:2688]).T[:16, :]

    def head_norm(t, g):
        ms = _dot((t * t).astype(BF16), hm_ref[...])
        return t * lax.rsqrt(ms + NORM_EPS) * g

    q_ref[0] = head_norm(q, qg_ref[...])
    k_ref[0] = head_norm(k, kg_ref[...])


def _inproj(x, scale, shift, g, w, q_norm_g, k_norm_g):
    B, S, D = x.shape
    tm = 512
    W = ATTN_WIDTH
    hid = jnp.arange(W) // ATTN_HEAD_DIM
    head_mean = jnp.where(hid[:, None] == hid[None, :], 1.0 / ATTN_HEAD_DIM, 0.0).astype(BF16)
    qg = jnp.tile(q_norm_g, ATTN_HEADS).reshape(1, W) * (LOG2E / math.sqrt(ATTN_HEAD_DIM))
    kg = jnp.tile(k_norm_g, ATTN_HEADS).reshape(1, W)
    row = lambda b, i: (b, i, 0)
    vec = lambda b, i: (b, 0, 0)
    const = lambda b, i: (0, 0)
    outs = ([jax.ShapeDtypeStruct((B, S, 512), F32)] * 2 + [jax.ShapeDtypeStruct((B, S, W), F32)] * 3
            + [jax.ShapeDtypeStruct((B, 16, S), F32)])
    return pl.pallas_call(
        _inproj_kernel,
        grid=(B, S // tm),
        in_specs=[pl.BlockSpec((1, tm, D), row),
                  pl.BlockSpec((1, 1, D), vec),
                  pl.BlockSpec((1, 1, D), vec),
                  pl.BlockSpec((1, D), const),
                  pl.BlockSpec(w.shape, const),
                  pl.BlockSpec((W, W), const),
                  pl.BlockSpec((1, W), const),
                  pl.BlockSpec((1, W), const)],
        out_specs=[pl.BlockSpec((1, tm, 512), row)] * 5 + [pl.BlockSpec((1, 16, tm), lambda b, i: (b, 0, i))],
        out_shape=outs,
        compiler_params=_params(("parallel", "arbitrary"), V7X_VMEM_LIMIT),
        name="inproj",
    )(x, scale, shift, g, w, head_mean, qg, kg)


def _chunk_scan(x, op, reverse):
    n = x.shape[1]
    idx = lax.broadcasted_iota(I32, x.shape, 1) & (MLSTM_CHUNK - 1)
    s = 1
    while s < MLSTM_CHUNK:
        if reverse:
            r = pltpu.roll(x, n - s, 1)
            x = jnp.where(idx < MLSTM_CHUNK - s, op(x, r), x)
        else:
            r = pltpu.roll(x, s, 1)
            x = jnp.where(idx >= s, op(x, r), x)
        s *= 2
    return x


def _log_sigmoid(x):
    return jnp.minimum(x, 0.0) - jnp.log(1.0 + jnp.exp(-jnp.abs(x)))


def _mlstm_kernel(xm_ref, op_ref, grow_ref, brow_ref, cw_ref, cb_ref,
                  wqt_ref, wk_ref, wvt_ref, ng_ref, sk_ref, y_ref,
                  qt_s, k_s, vat_s, xc_s, hf_s, hb_s, row_s, col_s):
    S = xm_ref.shape[1]
    L = MLSTM_CHUNK
    NC = S // L
    E = MLSTM_HEAD_DIM
    A = E + MLSTM_ONES_ROWS
    xm = xm_ref[0]

    rows = lax.broadcasted_iota(I32, xm.shape, 0)
    half = MLSTM_CONV // 2
    conv = xm * cw_ref[half:half + 1, :]
    for j in range(MLSTM_CONV):
        off = j - half
        if off == 0:
            continue
        shifted = pltpu.roll(xm, (-off) % S, 0)
        valid = rows >= -off if off < 0 else rows < S - off
        conv = conv + jnp.where(valid, shifted, 0.0) * cw_ref[j:j + 1, :]
    conv = conv + cb_ref[...]
    xc = conv * _sigmoid(conv)
    xc_s[...] = xc

    xcb = xc.astype(BF16)
    qt_s[...] = _dot_nt(wqt_ref[0], xcb).astype(BF16)
    k_s[...] = (_dot(xcb, wk_ref[0]) * (1.0 / math.sqrt(E))).astype(BF16)
    vat_s[:E, :] = _dot_nt(wvt_ref[0], xm.astype(BF16))
    vat_s[E:, :] = jnp.ones((MLSTM_ONES_ROWS, S), F32)

    pair_rows = grow_ref[0]
    own = jnp.where(pl.program_id(1) % 2 == 0, pair_rows[:4], pair_rows[4:])
    gr = own + brow_ref[0]
    kind = lax.broadcasted_iota(I32, gr.shape, 0)
    gr = jnp.where((kind & 1) == 1, _log_sigmoid(gr), gr)
    b_f = _chunk_scan(gr[1:2], jnp.add, False)
    b_b = _chunk_scan(gr[3:4], jnp.add, True)
    u_f = gr[0:1] - b_f
    u_b = gr[2:3] - b_b
    m_f = _chunk_scan(u_f, jnp.maximum, False)
    m_b = _chunk_scan(u_b, jnp.maximum, True)
    row_s[...] = jnp.concatenate([b_f, m_f, u_f, gr[1:2], b_b, m_b, u_b, gr[3:4]], axis=0)
    for k, stat in enumerate((u_f, u_b)):
        for c in range(NC):
            col_s[k, c * L:(c + 1) * L, :] = jnp.broadcast_to(stat[:, c * L:(c + 1) * L], (L, L)).T

    kj = lax.broadcasted_iota(I32, (L, L), 0)
    qi = lax.broadcasted_iota(I32, (L, L), 1)
    causal = (kj <= qi, kj >= qi)

    def local_part(c, dirn):
        r0 = pl.multiple_of(c * L, L)
        qt = qt_s[:, pl.ds(r0, L)]
        kc = k_s[pl.ds(r0, L), :]
        vat = vat_s[:, pl.ds(r0, L)]
        rr = row_s[:, pl.ds(r0, L)]
        b_r, m_r, u_r = (rr[4 * dirn + i:4 * dirn + i + 1, :] for i in range(3))
        g = jnp.sum(rr[4 * dirn + 3:4 * dirn + 4, :], axis=1, keepdims=True)
        m_loc = g + jnp.max(u_r, axis=1, keepdims=True)
        return dict(
            r0=r0, dirn=dirn, qt=qt, vat=vat.astype(BF16), b_r=b_r, m_r=m_r, g=g, m_loc=m_loc,
            u_c=col_s[dirn, pl.ds(r0, L), :],
            s=_dot(kc, qt),
            d_state=_dot((vat * jnp.exp(g + u_r - m_loc)).astype(BF16), kc))

    def intra_part(t):
        p = jnp.exp(jnp.where(causal[t["dirn"]], t["u_c"] - t["m_r"], NEG_BIG)) * t.pop("s")
        t["y_loc"] = _dot(t["vat"], p.astype(BF16))

    def state_part(t, state):
        Ct, m = state
        x_st = _dot(Ct.astype(BF16), t["qt"])
        mm = jnp.maximum(m, t["m_r"])
        z = jnp.exp(m - mm) * x_st + jnp.exp(t["m_r"] - mm) * t["y_loc"]
        h = z[:E, :] / jnp.maximum(jnp.abs(z[E:E + 1, :]), jnp.exp(-t["b_r"] - mm))
        m_new = jnp.maximum(t["g"] + m, t["m_loc"])
        Ct_new = jnp.exp(t["g"] + m - m_new) * Ct + jnp.exp(t["m_loc"] - m_new) * t["d_state"]
        return h, (Ct_new, m_new)

    def body(i, carry):
        states = list(carry)
        out_s = (hf_s, hb_s)
        work = []
        for k in range(MLSTM_UNROLL):
            c = i * MLSTM_UNROLL + k
            work += [local_part(c, 0), local_part(NC - 1 - c, 1)]
        for t in work:
            intra_part(t)
        for t in work:
            h, states[t["dirn"]] = state_part(t, states[t["dirn"]])
            out_s[t["dirn"]][:, pl.ds(t["r0"], L)] = h
        return tuple(states)

    init = (jnp.zeros((A, E), F32), jnp.zeros((1, 1), F32))
    lax.fori_loop(0, NC // MLSTM_UNROLL, body, (init, init))

    h = (hf_s[...] + hb_s[...]).T
    hn = h * lax.rsqrt(jnp.mean(h * h, axis=-1, keepdims=True) + NORM_EPS) * ng_ref[...]
    y = (hn + sk_ref[...] * xc_s[...]) * _sigmoid(op_ref[0])
    y_ref[0] = y.astype(y_ref.dtype)


def _blockdiag_dense(w_blk, transposed=False):
    E, Q = MLSTM_HEAD_DIM, MLSTM_QKV_BLOCK
    rows = w_blk.reshape(MLSTM_HEADS, E, Q)
    idx = np.arange(E)
    spread = jnp.asarray(idx[None, :] % Q == np.arange(Q)[:, None], w_blk.dtype)
    same_block = jnp.asarray(idx[:, None] // Q == idx[None, :] // Q, w_blk.dtype)
    out = 'hcr' if transposed else 'hrc'
    return jnp.einsum(f'hrj,jc->{out}', rows, spread, precision=lax.Precision.HIGHEST) * same_block


def _mlstm(x_m, o_pre, gates, conv_w, conv_b, w_q_blk, w_k_blk, w_v_blk, b_igate, b_fgate, norm_g, skip):
    B, S, W = x_m.shape
    H, E = MLSTM_HEADS, MLSTM_HEAD_DIM
    assert S % (MLSTM_CHUNK * MLSTM_UNROLL) == 0
    bk = jnp.stack([b_igate[0], b_fgate[0], b_igate[1], b_fgate[1]], axis=0)
    brow = bk.T.reshape(H, 4, 1)
    wqt = _blockdiag_dense(w_q_blk, transposed=True).astype(BF16)
    wk = _blockdiag_dense(w_k_blk).astype(BF16)
    wvt = _blockdiag_dense(w_v_blk, transposed=True).astype(BF16)
    headcol = lambda b, h: (b, 0, h)
    perhead = lambda b, h: (h, 0, 0)
    lanes = lambda b, h: (0, h)
    return pl.pallas_call(
        _mlstm_kernel,
        grid=(B, H),
        in_specs=[pl.BlockSpec((1, S, E), headcol),
                  pl.BlockSpec((1, S, E), headcol),
                  pl.BlockSpec((1, 8, S), lambda b, h: (b, h // 2, 0)),
                  pl.BlockSpec((1, 4, 1), perhead),
                  pl.BlockSpec((MLSTM_CONV, E), lanes),
                  pl.BlockSpec((1, E), lanes),
                  pl.BlockSpec((1, E, E), perhead),
                  pl.BlockSpec((1, E, E), perhead),
                  pl.BlockSpec((1, E, E), perhead),
                  pl.BlockSpec((1, E), lanes),
                  pl.BlockSpec((1, E), lanes)],
        out_specs=pl.BlockSpec((1, S, E), headcol),
        out_shape=jax.ShapeDtypeStruct((B, S, W), BF16),
        scratch_shapes=[pltpu.VMEM((E, S), BF16), pltpu.VMEM((S, E), BF16),
                        pltpu.VMEM((E + MLSTM_ONES_ROWS, S), F32),
                        pltpu.VMEM((S, E), F32), pltpu.VMEM((E, S), F32), pltpu.VMEM((E, S), F32),
                        pltpu.VMEM((8, S), F32), pltpu.VMEM((2, S, E), F32)],
        compiler_params=_params(("parallel", "arbitrary"), V7X_VMEM_LIMIT),
        name="mlstm",
    )(x_m, o_pre, gates, brow, conv_w, conv_b.reshape(1, W), wqt, wk, wvt,
      norm_g.reshape(1, W), skip.reshape(1, W))


def _t5_bucket_static(rel):
    half = REL_BUCKETS // 2
    exact = half // 2
    n = np.abs(rel)
    log_ratio = (np.log(np.maximum(n, 1).astype(np.float32) / np.float32(exact))
                 / np.float32(math.log(REL_MAX_DIST / exact)))
    large = np.minimum(exact + (log_ratio * np.float32(half - exact)).astype(np.int32), half - 1)
    return np.where(rel > 0, half, 0) + np.where(n < exact, n, large)


def _attn_bias_diagonals(rel_bias):
    n = ATTN_DIAG
    nv = 3 * len(DILATIONS)
    x = np.arange(n)
    offset = np.where(x <= ATTN_KBLK, x, x - n)
    rel = offset[None, :] - ATTN_HALF * np.arange(3)[:, None]
    valid = np.tile(np.abs(rel) <= ATTN_HALF, (len(DILATIONS), 1))
    bucket = np.concatenate([_t5_bucket_static(rel * d) for d in DILATIONS], axis=0)
    onehot = (bucket[..., None] == np.arange(REL_BUCKETS)) & valid[..., None]
    w = jnp.einsum('vnb,bh->vhn', jnp.asarray(onehot, F32), rel_bias.astype(F32) * LOG2E,
                   precision=lax.Precision.HIGHEST)
    w = jnp.where(jnp.asarray(valid)[:, None, :], w, NEG_BIG)
    w = w.reshape(nv, ATTN_HEADS // 2, 2, n).transpose(1, 0, 2, 3)
    return w.reshape(ATTN_HEADS // 2, 2 * nv, n)


def _attn_kernel(q_ref, k_ref, v_ref, diag_ref, y_ref, q0_s, q1_s, k_s, v0_s, v1_s, o_s, l_s, bias_s):
    S = q_ref.shape[1]
    QB, KB = ATTN_QBLK, ATTN_KBLK
    lane = lax.broadcasted_iota(I32, (1, V7X_LANES), 1)
    in_head = (lane < ATTN_HEAD_DIM, lane >= ATTN_HEAD_DIM)
    q_s = (q0_s, q1_s)
    v_s = (v0_s, v1_s)

    pair = pl.program_id(1)

    @pl.when(pl.program_id(0) == 0)
    def _():
        for i in range(bias_s.shape[1]):
            rows_i = jnp.broadcast_to(diag_ref[0, i:i + 1, :], (QB, ATTN_DIAG))
            bias_s[pair, i] = pltpu.roll(rows_i, 0, 1, stride=1, stride_axis=0)[:, :KB]

    def strided(start, size, d):
        return pl.ds(start, size) if d == 1 else pl.ds(start, size, stride=d)

    for p, d in enumerate(DILATIONS):
        L = S // d
        for r in range(d):
            src = strided(r, L, d)
            dst = slice(r * L, (r + 1) * L)
            q = q_ref[0, src, :]
            v = v_ref[0, src, :]
            k_s[p, dst, :] = k_ref[0, src, :].astype(BF16)
            for a in range(2):
                q_s[a][p, dst, :] = jnp.where(in_head[a], q, 0.0).astype(BF16)
                v_s[a][p, dst, :] = jnp.where(in_head[a], v, 1.0).astype(BF16)

    def logits(p, d, r, qb):
        L = S // d
        nqb = L // QB
        nk = min(L, KB)
        if nqb == 1 or qb == 0:
            k0, variant = 0, 0
        elif qb == nqb - 1:
            k0, variant = L - nk, 2
        else:
            k0, variant = qb * QB - ATTN_HALF, 1
        qrows = slice(r * L + qb * QB, r * L + (qb + 1) * QB)
        krows = slice(r * L + k0, r * L + k0 + nk)
        kt = k_s[p, krows, :]
        s = [_dot_nt(q_s[a][p, qrows, :], kt) + bias_s[pair, p * 6 + variant * 2 + a][:, :nk] for a in range(2)]
        return dict(p=p, krows=krows, out_rows=strided(r + d * qb * QB, QB, d), s=s)

    def softmax(t):
        t["m"] = [jnp.max(s, axis=1, keepdims=True) for s in t["s"]]
        t["e"] = [jnp.exp2(s - m).astype(BF16) for s, m in zip(t.pop("s"), t["m"])]

    def outputs(t):
        p = t["p"]
        acc = [_dot(t["e"][a], v_s[a][p, t["krows"], :]) for a in range(2)]
        num = jnp.where(in_head[0], acc[0], acc[1])
        den = pltpu.roll(jnp.where(in_head[0], acc[1], acc[0]), ATTN_HEAD_DIM, 1)
        o_s[p, t["out_rows"], :] = num / den
        l_s[p, t["out_rows"], :] = jnp.where(in_head[0], t["m"][0], t["m"][1]) + jnp.log2(den)

    units = [(p, d, r, qb) for p, d in enumerate(DILATIONS) for r in range(d) for qb in range(S // d // QB)]
    prev = []
    for i in range(0, len(units), ATTN_GROUP):
        cur = [logits(*u) for u in units[i:i + ATTN_GROUP]]
        for t in prev:
            outputs(t)
        for t in cur:
            softmax(t)
        prev = cur
    for t in prev:
        outputs(t)

    mx = jnp.maximum(jnp.maximum(l_s[0], l_s[1]), l_s[2])
    num = jnp.zeros((S, V7X_LANES), F32)
    den = jnp.zeros((S, V7X_LANES), F32)
    for p in range(len(DILATIONS)):
        w = jnp.exp2(l_s[p] - mx)
        num = num + w * o_s[p]
        den = den + w
    y_ref[0] = (num / den).astype(y_ref.dtype)


def _attn(a_q, a_k, a_v, diag):
    B, S, W = a_q.shape
    P = ATTN_HEADS // 2
    NP = len(DILATIONS)
    pair = lambda b, p: (b, 0, p)
    blk = pl.BlockSpec((1, S, V7X_LANES), pair)
    return pl.pallas_call(
        _attn_kernel,
        grid=(B, P),
        in_specs=[blk, blk, blk, pl.BlockSpec((1,) + diag.shape[1:], lambda b, p: (p, 0, 0))],
        out_specs=blk,
        out_shape=jax.ShapeDtypeStruct((B, S, W), BF16),
        scratch_shapes=[pltpu.VMEM((NP, S, V7X_LANES), BF16)] * 5 + [pltpu.VMEM((NP, S, V7X_LANES), F32)] * 2
                       + [pltpu.VMEM((P, diag.shape[1], ATTN_QBLK, ATTN_KBLK), F32)],
        compiler_params=_params(("arbitrary", "arbitrary"), V7X_VMEM_LIMIT),
        name="attn",
    )(a_q, a_k, a_v, diag)


def _outproj_kernel(ym_ref, ya_ref, x_ref, g1_ref, w1_ref, w2_ref, o_ref):
    mix = _dot(ym_ref[0], w1_ref[...]) + _dot(ya_ref[0], w2_ref[...])
    o_ref[0] = x_ref[0] + g1_ref[0] * mix


def _outproj(y_m, y_a, x, gate1, w_out):
    B, S, D = x.shape
    Wm = y_m.shape[-1]
    tm = 512
    w1 = w_out[:Wm].astype(BF16)
    w2 = w_out[Wm:].astype(BF16)
    row = lambda b, i: (b, i, 0)
    const = lambda b, i: (0, 0)
    return pl.pallas_call(
        _outproj_kernel,
        grid=(B, S // tm),
        in_specs=[pl.BlockSpec((1, tm, Wm), row),
                  pl.BlockSpec((1, tm, y_a.shape[-1]), row),
                  pl.BlockSpec((1, tm, D), row),
                  pl.BlockSpec((1, 1, D), lambda b, i: (b, 0, 0)),
                  pl.BlockSpec(w1.shape, const),
                  pl.BlockSpec(w2.shape, const)],
        out_specs=pl.BlockSpec((1, tm, D), row),
        out_shape=jax.ShapeDtypeStruct((B, S, D), F32),
        compiler_params=_params(("parallel", "arbitrary"), V7X_VMEM_LIMIT),
        name="outproj",
    )(y_m, y_a, x, gate1, w1, w2)


def _select_kernel(x_ref, sc_ref, sh_ref, g_ref, whl_ref, br_ref, tri_ref,
                   h_ref, pos_ref, gate_ref, off_ref, *, cap):
    S = x_ref.shape[1]
    NE = N_EXPERTS
    h = _modulated_norm(x_ref[0], g_ref[...], sc_ref[0], sh_ref[0])
    hi, lo = _split_bf16(h)
    h_ref[0] = hi
    both = _dot(hi, whl_ref[...])
    logits = both[:, :V7X_LANES] + both[:, V7X_LANES:] + _dot(lo, whl_ref[:, :V7X_LANES])
    lt = logits.T[:NE, :] + br_ref[...]
    ex = jnp.exp(lt - jnp.max(lt, axis=0, keepdims=True))
    aff = ex / jnp.sum(ex, axis=0, keepdims=True)
    gate_ref[0] = aff

    bits = pltpu.bitcast(aff, I32)

    def count_ge(cand):
        return jnp.sum((bits >= cand).astype(F32), axis=1, keepdims=True)

    def search(i, v):
        shift = 27 - 3 * i
        best = v
        for c in range(1, 8):
            cand = v | lax.shift_left(jnp.int32(c), shift)
            best = jnp.where(count_ge(cand) >= cap, cand, best)
        return best

    top = jnp.full((NE, 1), 1 << 30, I32)
    thr = lax.fori_loop(0, 10, search, jnp.where(count_ge(top) >= cap, top, 0))
    gt = (bits > thr).astype(F32)
    eq = (bits == thr).astype(F32)
    need = cap - jnp.sum(gt, axis=1, keepdims=True)

    def prefix_count(mask):
        off = jnp.zeros((NE, 1), F32)
        parts, starts = [], []
        for j in range(S // V7X_LANES):
            t = mask[:, j * V7X_LANES:(j + 1) * V7X_LANES]
            starts.append(off)
            parts.append(_dot(t.astype(BF16), tri_ref[...]) + off)
            off = off + jnp.sum(t, axis=1, keepdims=True)
        return jnp.concatenate(parts, axis=1), starts + [off]

    sel = jnp.maximum(gt, jnp.where(prefix_count(eq)[0] < need, eq, 0.0))
    slot, starts = prefix_count(sel)
    pos_ref[0] = jnp.where(sel > 0.0, slot, -1.0).astype(I32)
    step = MOE_TOKEN_TILE // V7X_LANES
    off_ref[0] = jnp.concatenate(starts[::step], axis=1).astype(I32)


def _select(x1, scale, shift, g, w_router, b_router):
    B, S, D = x1.shape
    NE = N_EXPERTS
    cap = (EC_CAPACITY_FACTOR * S) // NE
    nt = S // MOE_TOKEN_TILE
    wpad = jnp.zeros((D, V7X_LANES), F32).at[:, :NE].set(w_router)
    whl = jnp.concatenate(_split_bf16(wpad), axis=1)
    ti = jnp.arange(V7X_LANES)
    tri = (ti[:, None] < ti[None, :]).astype(BF16)
    vec = lambda b: (b, 0, 0)
    const = lambda b: (0, 0)
    return pl.pallas_call(
        functools.partial(_select_kernel, cap=cap),
        grid=(B,),
        in_specs=[pl.BlockSpec((1, S, D), vec),
                  pl.BlockSpec((1, 1, D), vec),
                  pl.BlockSpec((1, 1, D), vec),
                  pl.BlockSpec((1, D), const),
                  pl.BlockSpec((D, 2 * V7X_LANES), const),
                  pl.BlockSpec((NE, 1), const),
                  pl.BlockSpec((V7X_LANES, V7X_LANES), const)],
        out_specs=[pl.BlockSpec((1, S, D), vec),
                   pl.BlockSpec((1, NE, S), vec),
                   pl.BlockSpec((1, NE, S), vec),
                   pl.BlockSpec((1, NE, nt + 1), vec)],
        out_shape=[jax.ShapeDtypeStruct((B, S, D), BF16),
                   jax.ShapeDtypeStruct((B, NE, S), I32),
                   jax.ShapeDtypeStruct((B, NE, S), F32),
                   jax.ShapeDtypeStruct((B, NE, nt + 1), I32)],
        compiler_params=_params(("parallel",), V7X_VMEM_LIMIT),
        name="select",
    )(x1, scale, shift, g, whl, b_router.reshape(NE, 1), tri)


def _window_start(off, cap, rows):
    return pl.multiple_of(jnp.minimum((off // 16) * 16, cap - rows), 16)


def _gather_kernel(off_ref, h_ref, pos_ref, xin_ref):
    NE, _, cap, D = xin_ref.shape
    TT, W = MOE_TOKEN_TILE, MOE_WINDOW
    b, j = pl.program_id(0), pl.program_id(1)
    nt = pl.num_programs(1) * MOE_STEP_TILES

    @pl.when(j == 0)
    def _():
        xin_ref[...] = jnp.zeros(xin_ref.shape, xin_ref.dtype)

    for t in range(MOE_STEP_TILES):
        tok = slice(t * TT, (t + 1) * TT)
        base = (b * NE) * (nt + 1) + j * MOE_STEP_TILES + t
        offs = [off_ref[base + e * (nt + 1)] for e in range(NE)]
        ends = [off_ref[base + e * (nt + 1) + 1] for e in range(NE)]

        def onehot(e, start, tok=tok):
            slot = start + lax.broadcasted_iota(I32, (W, TT), 0)
            return jnp.where(pos_ref[0, e:e + 1, tok] == slot, 1.0, 0.0).astype(BF16)

        def place(e, start, new, offs=offs):
            slot = start + lax.broadcasted_iota(I32, (W, 1), 0)
            old = xin_ref[e, 0, pl.ds(start, W), :].astype(F32)
            xin_ref[e, 0, pl.ds(start, W), :] = jnp.where(slot >= offs[e], new, old).astype(xin_ref.dtype)

        starts = [_window_start(offs[e], cap, W) for e in range(NE)]
        res = _dot(jnp.concatenate([onehot(e, starts[e]) for e in range(NE)], axis=0), h_ref[0, tok, :])
        for e in range(NE):
            place(e, starts[e], res[e * W:(e + 1) * W])

        n_extra = [(jnp.maximum(ends[e] - starts[e] - W, 0) + W - 1) // W for e in range(NE)]

        @pl.when(functools.reduce(jnp.maximum, n_extra) > 0)
        def _(tok=tok, starts=starts, n_extra=n_extra, onehot=onehot, place=place):
            for e in range(NE):
                def extra(c, carry, e=e):
                    start = _window_start(starts[e] + W + c * W, cap, W)
                    place(e, start, _dot(onehot(e, start), h_ref[0, tok, :]))
                    return carry

                lax.fori_loop(0, n_extra[e], extra, 0)


def _gather(h2, pos, offs):
    B, S, D = h2.shape
    NE = N_EXPERTS
    cap = (EC_CAPACITY_FACTOR * S) // NE
    ts = MOE_TOKEN_TILE * MOE_STEP_TILES
    grid_spec = pltpu.PrefetchScalarGridSpec(
        num_scalar_prefetch=1,
        grid=(B, S // ts),
        in_specs=[pl.BlockSpec((1, ts, D), lambda b, j, o: (b, j, 0)),
                  pl.BlockSpec((1, NE, ts), lambda b, j, o: (b, 0, j))],
        out_specs=pl.BlockSpec((NE, 1, cap, D), lambda b, j, o: (0, b, 0, 0)),
    )
    return pl.pallas_call(
        _gather_kernel,
        grid_spec=grid_spec,
        out_shape=jax.ShapeDtypeStruct((NE, B, cap, D), BF16),
        compiler_params=_params(("parallel", "arbitrary"), V7X_VMEM_LIMIT),
        name="gather",
    )(offs.reshape(-1), h2, pos)


def _expert_kernel(x_ref, wg_ref, wu_ref, wd_ref, o_ref, acc_s, wg_s, wu_s, wd_s, *, row_tile):
    f = pl.program_id(2)
    nf = pl.num_programs(2)
    n_tiles = x_ref.shape[1] // row_tile

    def sweep(first, last):
        wg_s[...] = wg_ref[0].astype(BF16)
        wu_s[...] = wu_ref[0].astype(BF16)
        wd_s[...] = wd_ref[0].astype(BF16)

        def down(r, hid):
            y = _dot(hid, wd_s[...])
            if not first:
                y = y + acc_s[r, :]
            if last:
                o_ref[0, r, :] = y.astype(o_ref.dtype)
            else:
                acc_s[r, :] = y

        pending = None
        for i in range(n_tiles):
            r = slice(i * row_tile, (i + 1) * row_tile)
            xb = x_ref[0, r, :]
            g = _dot(xb, wg_s[...])
            u = _dot(xb, wu_s[...])
            if pending is not None:
                down(*pending)
            pending = (r, (g * _sigmoid(g) * u).astype(BF16))
        down(*pending)

    @pl.when(f == 0)
    def _():
        sweep(True, False)

    @pl.when(jnp.logical_and(f > 0, f < nf - 1))
    def _():
        sweep(False, False)

    @pl.when(f == nf - 1)
    def _():
        sweep(False, True)


def _experts(xin, w_gate, w_up, w_down):
    NE, R, D = xin.shape
    F = w_gate.shape[-1]
    tr = min(R, 2048)
    tf = 512
    row_tile = min(tr, 512)
    return pl.pallas_call(
        functools.partial(_expert_kernel, row_tile=row_tile),
        grid=(NE, R // tr, F // tf),
        in_specs=[pl.BlockSpec((1, tr, D), lambda e, r, f: (e, r, 0)),
                  pl.BlockSpec((1, D, tf), lambda e, r, f: (e, 0, f)),
                  pl.BlockSpec((1, D, tf), lambda e, r, f: (e, 0, f)),
                  pl.BlockSpec((1, tf, D), lambda e, r, f: (e, f, 0))],
        out_specs=pl.BlockSpec((1, tr, D), lambda e, r, f: (e, r, 0)),
        out_shape=jax.ShapeDtypeStruct((NE, R, D), BF16),
        scratch_shapes=[pltpu.VMEM((tr, D), F32), pltpu.VMEM((D, tf), BF16),
                        pltpu.VMEM((D, tf), BF16), pltpu.VMEM((tf, D), BF16)],
        compiler_params=_params(("parallel", "parallel", "arbitrary"), V7X_VMEM_LIMIT),
        name="experts",
    )(xin, w_gate, w_up, w_down)


def _combine_kernel(off_ref, y_ref, pos_ref, gate_ref, x_ref, g2_ref, o_ref):
    NE, _, cap, D = y_ref.shape
    TT, W = MOE_TOKEN_TILE, MOE_WINDOW
    b, j = pl.program_id(0), pl.program_id(1)
    nt = pl.num_programs(1) * MOE_STEP_TILES

    for t in range(MOE_STEP_TILES):
        tok = slice(t * TT, (t + 1) * TT)
        base = (b * NE) * (nt + 1) + j * MOE_STEP_TILES + t
        offs = [off_ref[base + e * (nt + 1)] for e in range(NE)]
        ends = [off_ref[base + e * (nt + 1) + 1] for e in range(NE)]
        starts = [_window_start(offs[e], cap, W) for e in range(NE)]

        def weighted_onehot(e, start, lo, tok=tok):
            slot = start + lax.broadcasted_iota(I32, (W, TT), 0)
            hit = (pos_ref[0, e:e + 1, tok] == slot) & (slot >= lo)
            return jnp.where(hit, gate_ref[0, e:e + 1, tok], 0.0).astype(BF16)

        scatter = jnp.concatenate([weighted_onehot(e, starts[e], 0) for e in range(NE)], axis=0)
        ystack = jnp.concatenate([y_ref[e, 0, pl.ds(starts[e], W), :] for e in range(NE)], axis=0)
        o_ref[0, tok, :] = x_ref[0, tok, :] + g2_ref[0] * _dot_tn(scatter, ystack)

        n_extra = [(jnp.maximum(ends[e] - starts[e] - W, 0) + W - 1) // W for e in range(NE)]

        @pl.when(functools.reduce(jnp.maximum, n_extra) > 0)
        def _(tok=tok, starts=starts, n_extra=n_extra, weighted_onehot=weighted_onehot):
            for e in range(NE):
                def extra(c, carry, e=e):
                    lo = starts[e] + W + c * W
                    start = _window_start(lo, cap, W)
                    part = _dot_tn(weighted_onehot(e, start, lo), y_ref[e, 0, pl.ds(start, W), :])
                    o_ref[0, tok, :] = o_ref[0, tok, :] + g2_ref[0] * part
                    return carry

                lax.fori_loop(0, n_extra[e], extra, 0)


def _combine(y, pos, gate, offs, x1, gate2):
    B, S, D = x1.shape
    NE, _, cap, _ = y.shape
    ts = MOE_TOKEN_TILE * MOE_STEP_TILES
    row = lambda b, j, o: (b, j, 0)
    grid_spec = pltpu.PrefetchScalarGridSpec(
        num_scalar_prefetch=1,
        grid=(B, S // ts),
        in_specs=[pl.BlockSpec((NE, 1, cap, D), lambda b, j, o: (0, b, 0, 0)),
                  pl.BlockSpec((1, NE, ts), lambda b, j, o: (b, 0, j)),
                  pl.BlockSpec((1, NE, ts), lambda b, j, o: (b, 0, j)),
                  pl.BlockSpec((1, ts, D), row),
                  pl.BlockSpec((1, 1, D), lambda b, j, o: (b, 0, 0))],
        out_specs=pl.BlockSpec((1, ts, D), row),
    )
    return pl.pallas_call(
        _combine_kernel,
        grid_spec=grid_spec,
        out_shape=jax.ShapeDtypeStruct((B, S, D), F32),
        compiler_params=_params(("parallel", "arbitrary"), V7X_VMEM_LIMIT),
        name="combine",
    )(offs.reshape(-1), y, pos, gate, x1, gate2)


def _inproj_weight(w_in):
    D = w_in.shape[0]
    H = MLSTM_HEADS
    gates = w_in[:, 1024:1040].reshape(D, 4, H).transpose(0, 2, 1).reshape(D, 4 * H)
    main = jnp.concatenate([w_in[:, :1024], w_in[:, 1040:2576], gates,
                            jnp.zeros((D, V7X_LANES - 16), w_in.dtype)], axis=1)
    return main.astype(BF16)


def kernel(x, c, w_ada, b_ada, norm1_g, w_in, conv_w, conv_b, w_q_blk, w_k_blk, w_v_blk, b_igate, b_fgate,
           mlstm_norm_g, mlstm_skip, q_norm_g, k_norm_g, rel_bias, w_out, norm2_g, w_router, b_router,
           w_gate, w_up, w_down):
    B, S, D = x.shape
    depth = w_ada.shape[0]
    diag = _attn_bias_diagonals(rel_bias)
    for l in range(depth):
        mod = _ada(c, w_ada[l], b_ada[l])
        shift1, scale1, gate1, shift2, scale2, gate2 = (
            mod[:, i * D:(i + 1) * D].reshape(B, 1, D) for i in range(N_MOD))

        x_m, o_pre, a_q, a_k, a_v, gates = _inproj(x, scale1, shift1, norm1_g[l].reshape(1, D),
                                                   _inproj_weight(w_in[l]), q_norm_g[l], k_norm_g[l])
        y_m = _mlstm(x_m, o_pre, gates, conv_w[l], conv_b[l], w_q_blk[l], w_k_blk[l], w_v_blk[l],
                     b_igate[l], b_fgate[l], mlstm_norm_g[l], mlstm_skip[l])
        y_a = _attn(a_q, a_k, a_v, diag)
        x1 = _outproj(y_m, y_a, x, gate1, w_out[l])

        h2, pos, aff, offs = _select(x1, scale2, shift2, norm2_g[l].reshape(1, D), w_router[l], b_router[l])
        xin = _gather(h2, pos, offs)
        NE, _, cap, _ = xin.shape
        y = _experts(xin.reshape(NE, B * cap, D), w_gate[l], w_up[l], w_down[l]).reshape(NE, B, cap, D)
        x = _combine(y, pos, aff, offs, x1, gate2)
    return x
```

```python
import functools
import math

import numpy as np
import jax
import jax.numpy as jnp
from jax import lax
from jax.experimental import pallas as pl
from jax.experimental.pallas import tpu as pltpu

F32 = jnp.float32
BF16 = jnp.bfloat16
I32 = jnp.int32

NORM_EPS = 1e-6
N_MOD = 6
MLSTM_HEADS = 4
MLSTM_HEAD_DIM = 128
MLSTM_QKV_BLOCK = 4
MLSTM_CONV = 5
MLSTM_CHUNK = 128
MLSTM_UNROLL = 16
MLSTM_ONES_ROWS = 16
ATTN_HEADS = 8
ATTN_HEAD_DIM = 64
ATTN_WIDTH = ATTN_HEADS * ATTN_HEAD_DIM
DILATIONS = (1, 4, 16)
ATTN_HALF = 64
ATTN_QBLK = 128
ATTN_KBLK = 256
ATTN_DIAG = 512
ATTN_GROUP = 2
LOG2E = math.log2(math.e)
REL_BUCKETS = 32
REL_MAX_DIST = 1024
N_EXPERTS = 16
EC_CAPACITY_FACTOR = 2
MOE_TOKEN_TILE = 256
MOE_STEP_TILES = 4
MOE_WINDOW = 64
NEG_BIG = -1e30

V7X_LANES = 128
V7X_VMEM_LIMIT = 56 * 1024 * 1024


def _sigmoid(x):
    return 1.0 / (1.0 + jnp.exp(-x))


def _dot(a, b):
    return jnp.dot(a, b, preferred_element_type=F32)


def _dot_nt(a, b):
    return lax.dot_general(a, b, (((1,), (1,)), ((), ())), preferred_element_type=F32)


def _dot_tn(a, b):
    return lax.dot_general(a, b, (((0,), (0,)), ((), ())), preferred_element_type=F32)


def _split_bf16(x):
    hi = x.astype(BF16)
    lo = (x - hi.astype(F32)).astype(BF16)
    return hi, lo


def _params(sem, vmem=None):
    return pltpu.CompilerParams(dimension_semantics=sem, vmem_limit_bytes=vmem)


def _ada_kernel(c_ref, w_ref, b_ref, o_ref):
    c = c_ref[...]
    s = c * _sigmoid(c)
    o_ref[...] = jnp.dot(s, w_ref[...], preferred_element_type=F32,
                         precision=lax.Precision.HIGHEST) + b_ref[...]


def _ada(c, w, b):
    B, D = c.shape
    N = w.shape[1]
    tn = 1024
    return pl.pallas_call(
        _ada_kernel,
        grid=(N // tn,),
        in_specs=[pl.BlockSpec((B, D), lambda j: (0, 0)),
                  pl.BlockSpec((D, tn), lambda j: (0, j)),
                  pl.BlockSpec((1, tn), lambda j: (0, j))],
        out_specs=pl.BlockSpec((B, tn), lambda j: (0, j)),
        out_shape=jax.ShapeDtypeStruct((B, N), F32),
        compiler_params=_params(("arbitrary",)),
        name="ada",
    )(c, w, b.reshape(1, N))


def _modulated_norm(x, g, scale, shift):
    ms = jnp.mean(x * x, axis=-1, keepdims=True)
    return x * lax.rsqrt(ms + NORM_EPS) * (g * (1.0 + scale)) + shift


def _inproj_kernel(x_ref, sc_ref, sh_ref, g_ref, w_ref, hm_ref, qg_ref, kg_ref,
                   xm_ref, op_ref, q_ref, k_ref, v_ref, gt_ref):
    h = _modulated_norm(x_ref[0], g_ref[...], sc_ref[0], sh_ref[0]).astype(BF16)
    col = lambda i: _dot(h, w_ref[:, 512 * i:512 * (i + 1)])
    q = col(2)
    k = col(3)
    xm_ref[0] = col(0)
    op_ref[0] = col(1)
    v_ref[0] = col(4)
    gt_ref[0] = _dot(h, w_ref[:, 2560:2688]).T[:16, :]

    def head_norm(t, g):
        ms = _dot((t * t).astype(BF16), hm_ref[...])
        return t * lax.rsqrt(ms + NORM_EPS) * g

    q_ref[0] = head_norm(q, qg_ref[...])
    k_ref[0] = head_norm(k, kg_ref[...])


def _inproj(x, scale, shift, g, w, q_norm_g, k_norm_g):
    B, S, D = x.shape
    tm = 512
    W = ATTN_WIDTH
    hid = jnp.arange(W) // ATTN_HEAD_DIM
    head_mean = jnp.where(hid[:, None] == hid[None, :], 1.0 / ATTN_HEAD_DIM, 0.0).astype(BF16)
    qg = jnp.tile(q_norm_g, ATTN_HEADS).reshape(1, W) * (LOG2E / math.sqrt(ATTN_HEAD_DIM))
    kg = jnp.tile(k_norm_g, ATTN_HEADS).reshape(1, W)
    row = lambda b, i: (b, i, 0)
    vec = lambda b, i: (b, 0, 0)
    const = lambda b, i: (0, 0)
    outs = ([jax.ShapeDtypeStruct((B, S, 512), F32)] * 2 + [jax.ShapeDtypeStruct((B, S, W), F32)] * 3
            + [jax.ShapeDtypeStruct((B, 16, S), F32)])
    return pl.pallas_call(
        _inproj_kernel,
        grid=(B, S // tm),
        in_specs=[pl.BlockSpec((1, tm, D), row),
                  pl.BlockSpec((1, 1, D), vec),
                  pl.BlockSpec((1, 1, D), vec),
                  pl.BlockSpec((1, D), const),
                  pl.BlockSpec(w.shape, const),
                  pl.BlockSpec((W, W), const),
                  pl.BlockSpec((1, W), const),
                  pl.BlockSpec((1, W), const)],
        out_specs=[pl.BlockSpec((1, tm, 512), row)] * 5 + [pl.BlockSpec((1, 16, tm), lambda b, i: (b, 0, i))],
        out_shape=outs,
        compiler_params=_params(("parallel", "arbitrary"), V7X_VMEM_LIMIT),
        name="inproj",
    )(x, scale, shift, g, w, head_mean, qg, kg)


def _chunk_scan(x, op, reverse):
    n = x.shape[1]
    idx = lax.broadcasted_iota(I32, x.shape, 1) & (MLSTM_CHUNK - 1)
    s = 1
    while s < MLSTM_CHUNK:
        if reverse:
            r = pltpu.roll(x, n - s, 1)
            x = jnp.where(idx < MLSTM_CHUNK - s, op(x, r), x)
        else:
            r = pltpu.roll(x, s, 1)
            x = jnp.where(idx >= s, op(x, r), x)
        s *= 2
    return x


def _log_sigmoid(x):
    return jnp.minimum(x, 0.0) - jnp.log(1.0 + jnp.exp(-jnp.abs(x)))


def _mlstm_kernel(xm_ref, op_ref, grow_ref, brow_ref, cw_ref, cb_ref,
                  wqt_ref, wk_ref, wvt_ref, ng_ref, sk_ref, y_ref,
                  qt_s, k_s, vat_s, xc_s, hf_s, hb_s, row_s, col_s):
    S = xm_ref.shape[1]
    L = MLSTM_CHUNK
    NC = S // L
    E = MLSTM_HEAD_DIM
    A = E + MLSTM_ONES_ROWS
    xm = xm_ref[0]

    rows = lax.broadcasted_iota(I32, xm.shape, 0)
    half = MLSTM_CONV // 2
    conv = xm * cw_ref[half:half + 1, :]
    for j in range(MLSTM_CONV):
        off = j - half
        if off == 0:
            continue
        shifted = pltpu.roll(xm, (-off) % S, 0)
        valid = rows >= -off if off < 0 else rows < S - off
        conv = conv + jnp.where(valid, shifted, 0.0) * cw_ref[j:j + 1, :]
    conv = conv + cb_ref[...]
    xc = conv * _sigmoid(conv)
    xc_s[...] = xc

    xcb = xc.astype(BF16)
    qt_s[...] = _dot_nt(wqt_ref[0], xcb).astype(BF16)
    k_s[...] = (_dot(xcb, wk_ref[0]) * (1.0 / math.sqrt(E))).astype(BF16)
    vat_s[:E, :] = _dot_nt(wvt_ref[0], xm.astype(BF16))
    vat_s[E:, :] = jnp.ones((MLSTM_ONES_ROWS, S), F32)

    pair_rows = grow_ref[0]
    own = jnp.where(pl.program_id(1) % 2 == 0, pair_rows[:4], pair_rows[4:])
    gr = own + brow_ref[0]
    kind = lax.broadcasted_iota(I32, gr.shape, 0)
    gr = jnp.where((kind & 1) == 1, _log_sigmoid(gr), gr)
    b_f = _chunk_scan(gr[1:2], jnp.add, False)
    b_b = _chunk_scan(gr[3:4], jnp.add, True)
    u_f = gr[0:1] - b_f
    u_b = gr[2:3] - b_b
    m_f = _chunk_scan(u_f, jnp.maximum, False)
    m_b = _chunk_scan(u_b, jnp.maximum, True)
    row_s[...] = jnp.concatenate([b_f, m_f, u_f, gr[1:2], b_b, m_b, u_b, gr[3:4]], axis=0)
    for k, stat in enumerate((u_f, u_b)):
        for c in range(NC):
            col_s[k, c * L:(c + 1) * L, :] = jnp.broadcast_to(stat[:, c * L:(c + 1) * L], (L, L)).T

    kj = lax.broadcasted_iota(I32, (L, L), 0)
    qi = lax.broadcasted_iota(I32, (L, L), 1)
    causal = (kj <= qi, kj >= qi)

    def local_part(c, dirn):
        r0 = pl.multiple_of(c * L, L)
        qt = qt_s[:, pl.ds(r0, L)]
        kc = k_s[pl.ds(r0, L), :]
        vat = vat_s[:, pl.ds(r0, L)]
        rr = row_s[:, pl.ds(r0, L)]
        b_r, m_r, u_r = (rr[4 * dirn + i:4 * dirn + i + 1, :] for i in range(3))
        g = jnp.sum(rr[4 * dirn + 3:4 * dirn + 4, :], axis=1, keepdims=True)
        m_loc = g + jnp.max(u_r, axis=1, keepdims=True)
        return dict(
            r0=r0, dirn=dirn, qt=qt, vat=vat.astype(BF16), b_r=b_r, m_r=m_r, g=g, m_loc=m_loc,
            u_c=col_s[dirn, pl.ds(r0, L), :],
            s=_dot(kc, qt),
            d_state=_dot((vat * jnp.exp(g + u_r - m_loc)).astype(BF16), kc))

    def intra_part(t):
        p = jnp.exp(jnp.where(causal[t["dirn"]], t["u_c"] - t["m_r"], NEG_BIG)) * t.pop("s")
        t["y_loc"] = _dot(t["vat"], p.astype(BF16))

    def state_part(t, state):
        Ct, m = state
        x_st = _dot(Ct.astype(BF16), t["qt"])
        mm = jnp.maximum(m, t["m_r"])
        z = jnp.exp(m - mm) * x_st + jnp.exp(t["m_r"] - mm) * t["y_loc"]
        h = z[:E, :] / jnp.maximum(jnp.abs(z[E:E + 1, :]), jnp.exp(-t["b_r"] - mm))
        m_new = jnp.maximum(t["g"] + m, t["m_loc"])
        Ct_new = jnp.exp(t["g"] + m - m_new) * Ct + jnp.exp(t["m_loc"] - m_new) * t["d_state"]
        return h, (Ct_new, m_new)

    def body(i, carry):
        states = list(carry)
        out_s = (hf_s, hb_s)
        work = []
        for k in range(MLSTM_UNROLL):
            c = i * MLSTM_UNROLL + k
            work += [local_part(c, 0), local_part(NC - 1 - c, 1)]
        for t in work:
            intra_part(t)
        for t in work:
            h, states[t["dirn"]] = state_part(t, states[t["dirn"]])
            out_s[t["dirn"]][:, pl.ds(t["r0"], L)] = h
        return tuple(states)

    init = (jnp.zeros((A, E), F32), jnp.zeros((1, 1), F32))
    lax.fori_loop(0, NC // MLSTM_UNROLL, body, (init, init))

    h = (hf_s[...] + hb_s[...]).T
    hn = h * lax.rsqrt(jnp.mean(h * h, axis=-1, keepdims=True) + NORM_EPS) * ng_ref[...]
    y = (hn + sk_ref[...] * xc_s[...]) * _sigmoid(op_ref[0])
    y_ref[0] = y.astype(y_ref.dtype)


def _blockdiag_dense(w_blk, transposed=False):
    E, Q = MLSTM_HEAD_DIM, MLSTM_QKV_BLOCK
    rows = w_blk.reshape(MLSTM_HEADS, E, Q)
    idx = np.arange(E)
    spread = jnp.asarray(idx[None, :] % Q == np.arange(Q)[:, None], w_blk.dtype)
    same_block = jnp.asarray(idx[:, None] // Q == idx[None, :] // Q, w_blk.dtype)
    out = 'hcr' if transposed else 'hrc'
    return jnp.einsum(f'hrj,jc->{out}', rows, spread, precision=lax.Precision.HIGHEST) * same_block


def _mlstm(x_m, o_pre, gates, conv_w, conv_b, w_q_blk, w_k_blk, w_v_blk, b_igate, b_fgate, norm_g, skip):
    B, S, W = x_m.shape
    H, E = MLSTM_HEADS, MLSTM_HEAD_DIM
    assert S % (MLSTM_CHUNK * MLSTM_UNROLL) == 0
    bk = jnp.stack([b_igate[0], b_fgate[0], b_igate[1], b_fgate[1]], axis=0)
    brow = bk.T.reshape(H, 4, 1)
    wqt = _blockdiag_dense(w_q_blk, transposed=True).astype(BF16)
    wk = _blockdiag_dense(w_k_blk).astype(BF16)
    wvt = _blockdiag_dense(w_v_blk, transposed=True).astype(BF16)
    headcol = lambda b, h: (b, 0, h)
    perhead = lambda b, h: (h, 0, 0)
    lanes = lambda b, h: (0, h)
    return pl.pallas_call(
        _mlstm_kernel,
        grid=(B, H),
        in_specs=[pl.BlockSpec((1, S, E), headcol),
                  pl.BlockSpec((1, S, E), headcol),
                  pl.BlockSpec((1, 8, S), lambda b, h: (b, h // 2, 0)),
                  pl.BlockSpec((1, 4, 1), perhead),
                  pl.BlockSpec((MLSTM_CONV, E), lanes),
                  pl.BlockSpec((1, E), lanes),
                  pl.BlockSpec((1, E, E), perhead),
                  pl.BlockSpec((1, E, E), perhead),
                  pl.BlockSpec((1, E, E), perhead),
                  pl.BlockSpec((1, E), lanes),
                  pl.BlockSpec((1, E), lanes)],
        out_specs=pl.BlockSpec((1, S, E), headcol),
        out_shape=jax.ShapeDtypeStruct((B, S, W), BF16),
        scratch_shapes=[pltpu.VMEM((E, S), BF16), pltpu.VMEM((S, E), BF16),
                        pltpu.VMEM((E + MLSTM_ONES_ROWS, S), F32),
                        pltpu.VMEM((S, E), F32), pltpu.VMEM((E, S), F32), pltpu.VMEM((E, S), F32),
                        pltpu.VMEM((8, S), F32), pltpu.VMEM((2, S, E), F32)],
        compiler_params=_params(("parallel", "arbitrary"), V7X_VMEM_LIMIT),
        name="mlstm",
    )(x_m, o_pre, gates, brow, conv_w, conv_b.reshape(1, W), wqt, wk, wvt,
      norm_g.reshape(1, W), skip.reshape(1, W))


def _t5_bucket_static(rel):
    half = REL_BUCKETS // 2
    exact = half // 2
    n = np.abs(rel)
    log_ratio = (np.log(np.maximum(n, 1).astype(np.float32) / np.float32(exact))
                 / np.float32(math.log(REL_MAX_DIST / exact)))
    large = np.minimum(exact + (log_ratio * np.float32(half - exact)).astype(np.int32), half - 1)
    return np.where(rel > 0, half, 0) + np.where(n < exact, n, large)


def _attn_bias_diagonals(rel_bias):
    n = ATTN_DIAG
    nv = 3 * len(DILATIONS)
    x = np.arange(n)
    offset = np.where(x <= ATTN_KBLK, x, x - n)
    rel = offset[None, :] - ATTN_HALF * np.arange(3)[:, None]
    valid = np.tile(np.abs(rel) <= ATTN_HALF, (len(DILATIONS), 1))
    bucket = np.concatenate([_t5_bucket_static(rel * d) for d in DILATIONS], axis=0)
    onehot = (bucket[..., None] == np.arange(REL_BUCKETS)) & valid[..., None]
    w = jnp.einsum('vnb,bh->vhn', jnp.asarray(onehot, F32), rel_bias.astype(F32) * LOG2E,
                   precision=lax.Precision.HIGHEST)
    w = jnp.where(jnp.asarray(valid)[:, None, :], w, NEG_BIG)
    w = w.reshape(nv, ATTN_HEADS // 2, 2, n).transpose(1, 0, 2, 3)
    return w.reshape(ATTN_HEADS // 2, 2 * nv, n)


def _attn_kernel(q_ref, k_ref, v_ref, diag_ref, y_ref, q0_s, q1_s, k_s, v0_s, v1_s, o_s, l_s, bias_s):
    S = q_ref.shape[1]
    QB, KB = ATTN_QBLK, ATTN_KBLK
    lane = lax.broadcasted_iota(I32, (1, V7X_LANES), 1)
    in_head = (lane < ATTN_HEAD_DIM, lane >= ATTN_HEAD_DIM)
    q_s = (q0_s, q1_s)
    v_s = (v0_s, v1_s)

    pair = pl.program_id(1)

    @pl.when(pl.program_id(0) == 0)
    def _():
        for i in range(bias_s.shape[1]):
            rows_i = jnp.broadcast_to(diag_ref[0, i:i + 1, :], (QB, ATTN_DIAG))
            bias_s[pair, i] = pltpu.roll(rows_i, 0, 1, stride=1, stride_axis=0)[:, :KB]

    def strided(start, size, d):
        return pl.ds(start, size) if d == 1 else pl.ds(start, size, stride=d)

    for p, d in enumerate(DILATIONS):
        L = S // d
        for r in range(d):
            src = strided(r, L, d)
            dst = slice(r * L, (r + 1) * L)
            q = q_ref[0, src, :]
            v = v_ref[0, src, :]
            k_s[p, dst, :] = k_ref[0, src, :].astype(BF16)
            for a in range(2):
                q_s[a][p, dst, :] = jnp.where(in_head[a], q, 0.0).astype(BF16)
                v_s[a][p, dst, :] = jnp.where(in_head[a], v, 1.0).astype(BF16)

    def logits(p, d, r, qb):
        L = S // d
        nqb = L // QB
        nk = min(L, KB)
        if nqb == 1 or qb == 0:
            k0, variant = 0, 0
        elif qb == nqb - 1:
            k0, variant = L - nk, 2
        else:
            k0, variant = qb * QB - ATTN_HALF, 1
        qrows = slice(r * L + qb * QB, r * L + (qb + 1) * QB)
        krows = slice(r * L + k0, r * L + k0 + nk)
        kt = k_s[p, krows, :]
        s = [_dot_nt(q_s[a][p, qrows, :], kt) + bias_s[pair, p * 6 + variant * 2 + a][:, :nk] for a in range(2)]
        return dict(p=p, krows=krows, out_rows=strided(r + d * qb * QB, QB, d), s=s)

    def softmax(t):
        t["m"] = [jnp.max(s, axis=1, keepdims=True) for s in t["s"]]
        t["e"] = [jnp.exp2(s - m).astype(BF16) for s, m in zip(t.pop("s"), t["m"])]

    def outputs(t):
        p = t["p"]
        acc = [_dot(t["e"][a], v_s[a][p, t["krows"], :]) for a in range(2)]
        num = jnp.where(in_head[0], acc[0], acc[1])
        den = pltpu.roll(jnp.where(in_head[0], acc[1], acc[0]), ATTN_HEAD_DIM, 1)
        o_s[p, t["out_rows"], :] = num / den
        l_s[p, t["out_rows"], :] = jnp.where(in_head[0], t["m"][0], t["m"][1]) + jnp.log2(den)

    units = [(p, d, r, qb) for p, d in enumerate(DILATIONS) for r in range(d) for qb in range(S // d // QB)]
    prev = []
    for i in range(0, len(units), ATTN_GROUP):
        cur = [logits(*u) for u in units[i:i + ATTN_GROUP]]
        for t in prev:
            outputs(t)
        for t in cur:
            softmax(t)
        prev = cur
    for t in prev:
        outputs(t)

    mx = jnp.maximum(jnp.maximum(l_s[0], l_s[1]), l_s[2])
    num = jnp.zeros((S, V7X_LANES), F32)
    den = jnp.zeros((S, V7X_LANES), F32)
    for p in range(len(DILATIONS)):
        w = jnp.exp2(l_s[p] - mx)
        num = num + w * o_s[p]
        den = den + w
    y_ref[0] = (num / den).astype(y_ref.dtype)


def _attn(a_q, a_k, a_v, diag):
    B, S, W = a_q.shape
    P = ATTN_HEADS // 2
    NP = len(DILATIONS)
    pair = lambda b, p: (b, 0, p)
    blk = pl.BlockSpec((1, S, V7X_LANES), pair)
    return pl.pallas_call(
        _attn_kernel,
        grid=(B, P),
        in_specs=[blk, blk, blk, pl.BlockSpec((1,) + diag.shape[1:], lambda b, p: (p, 0, 0))],
        out_specs=blk,
        out_shape=jax.ShapeDtypeStruct((B, S, W), BF16),
        scratch_shapes=[pltpu.VMEM((NP, S, V7X_LANES), BF16)] * 5 + [pltpu.VMEM((NP, S, V7X_LANES), F32)] * 2
                       + [pltpu.VMEM((P, diag.shape[1], ATTN_QBLK, ATTN_KBLK), F32)],
        compiler_params=_params(("arbitrary", "arbitrary"), V7X_VMEM_LIMIT),
        name="attn",
    )(a_q, a_k, a_v, diag)


def _outproj_kernel(ym_ref, ya_ref, x_ref, g1_ref, w1_ref, w2_ref, o_ref):
    mix = _dot(ym_ref[0], w1_ref[...]) + _dot(ya_ref[0], w2_ref[...])
    o_ref[0] = x_ref[0] + g1_ref[0] * mix


def _outproj(y_m, y_a, x, gate1, w_out):
    B, S, D = x.shape
    Wm = y_m.shape[-1]
    tm = 512
    w1 = w_out[:Wm].astype(BF16)
    w2 = w_out[Wm:].astype(BF16)
    row = lambda b, i: (b, i, 0)
    const = lambda b, i: (0, 0)
    return pl.pallas_call(
        _outproj_kernel,
        grid=(B, S // tm),
        in_specs=[pl.BlockSpec((1, tm, Wm), row),
                  pl.BlockSpec((1, tm, y_a.shape[-1]), row),
                  pl.BlockSpec((1, tm, D), row),
                  pl.BlockSpec((1, 1, D), lambda b, i: (b, 0, 0)),
                  pl.BlockSpec(w1.shape, const),
                  pl.BlockSpec(w2.shape, const)],
        out_specs=pl.BlockSpec((1, tm, D), row),
        out_shape=jax.ShapeDtypeStruct((B, S, D), F32),
        compiler_params=_params(("parallel", "arbitrary"), V7X_VMEM_LIMIT),
        name="outproj",
    )(y_m, y_a, x, gate1, w1, w2)


def _select_kernel(x_ref, sc_ref, sh_ref, g_ref, whl_ref, br_ref, tri_ref,
                   h_ref, pos_ref, gate_ref, off_ref, *, cap):
    S = x_ref.shape[1]
    NE = N_EXPERTS
    h = _modulated_norm(x_ref[0], g_ref[...], sc_ref[0], sh_ref[0])
    hi, lo = _split_bf16(h)
    h_ref[0] = hi
    both = _dot(hi, whl_ref[...])
    logits = both[:, :V7X_LANES] + both[:, V7X_LANES:] + _dot(lo, whl_ref[:, :V7X_LANES])
    lt = logits.T[:NE, :] + br_ref[...]
    ex = jnp.exp(lt - jnp.max(lt, axis=0, keepdims=True))
    aff = ex / jnp.sum(ex, axis=0, keepdims=True)
    gate_ref[0] = aff

    bits = pltpu.bitcast(aff, I32)

    def count_ge(cand):
        return jnp.sum((bits >= cand).astype(F32), axis=1, keepdims=True)

    def search(i, v):
        shift = 27 - 3 * i
        best = v
        for c in range(1, 8):
            cand = v | lax.shift_left(jnp.int32(c), shift)
            best = jnp.where(count_ge(cand) >= cap, cand, best)
        return best

    top = jnp.full((NE, 1), 1 << 30, I32)
    thr = lax.fori_loop(0, 10, search, jnp.where(count_ge(top) >= cap, top, 0))
    gt = (bits > thr).astype(F32)
    eq = (bits == thr).astype(F32)
    need = cap - jnp.sum(gt, axis=1, keepdims=True)

    def prefix_count(mask):
        off = jnp.zeros((NE, 1), F32)
        parts, starts = [], []
        for j in range(S // V7X_LANES):
            t = mask[:, j * V7X_LANES:(j + 1) * V7X_LANES]
            starts.append(off)
            parts.append(_dot(t.astype(BF16), tri_ref[...]) + off)
            off = off + jnp.sum(t, axis=1, keepdims=True)
        return jnp.concatenate(parts, axis=1), starts + [off]

    sel = jnp.maximum(gt, jnp.where(prefix_count(eq)[0] < need, eq, 0.0))
    slot, starts = prefix_count(sel)
    pos_ref[0] = jnp.where(sel > 0.0, slot, -1.0).astype(I32)
    step = MOE_TOKEN_TILE // V7X_LANES
    off_ref[0] = jnp.concatenate(starts[::step], axis=1).astype(I32)


def _select(x1, scale, shift, g, w_router, b_router):
    B, S, D = x1.shape
    NE = N_EXPERTS
    cap = (EC_CAPACITY_FACTOR * S) // NE
    nt = S // MOE_TOKEN_TILE
    wpad = jnp.zeros((D, V7X_LANES), F32).at[:, :NE].set(w_router)
    whl = jnp.concatenate(_split_bf16(wpad), axis=1)
    ti = jnp.arange(V7X_LANES)
    tri = (ti[:, None] < ti[None, :]).astype(BF16)
    vec = lambda b: (b, 0, 0)
    const = lambda b: (0, 0)
    return pl.pallas_call(
        functools.partial(_select_kernel, cap=cap),
        grid=(B,),
        in_specs=[pl.BlockSpec((1, S, D), vec),
                  pl.BlockSpec((1, 1, D), vec),
                  pl.BlockSpec((1, 1, D), vec),
                  pl.BlockSpec((1, D), const),
                  pl.BlockSpec((D, 2 * V7X_LANES), const),
                  pl.BlockSpec((NE, 1), const),
                  pl.BlockSpec((V7X_LANES, V7X_LANES), const)],
        out_specs=[pl.BlockSpec((1, S, D), vec),
                   pl.BlockSpec((1, NE, S), vec),
                   pl.BlockSpec((1, NE, S), vec),
                   pl.BlockSpec((1, NE, nt + 1), vec)],
        out_shape=[jax.ShapeDtypeStruct((B, S, D), BF16),
                   jax.ShapeDtypeStruct((B, NE, S), I32),
                   jax.ShapeDtypeStruct((B, NE, S), F32),
                   jax.ShapeDtypeStruct((B, NE, nt + 1), I32)],
        compiler_params=_params(("parallel",), V7X_VMEM_LIMIT),
        name="select",
    )(x1, scale, shift, g, whl, b_router.reshape(NE, 1), tri)


def _window_start(off, cap, rows):
    return pl.multiple_of(jnp.minimum((off // 16) * 16, cap - rows), 16)


def _gather_kernel(off_ref, h_ref, pos_ref, xin_ref):
    NE, _, cap, D = xin_ref.shape
    TT, W = MOE_TOKEN_TILE, MOE_WINDOW
    b, j = pl.program_id(0), pl.program_id(1)
    nt = pl.num_programs(1) * MOE_STEP_TILES

    @pl.when(j == 0)
    def _():
        xin_ref[...] = jnp.zeros(xin_ref.shape, xin_ref.dtype)

    for t in range(MOE_STEP_TILES):
        tok = slice(t * TT, (t + 1) * TT)
        base = (b * NE) * (nt + 1) + j * MOE_STEP_TILES + t
        offs = [off_ref[base + e * (nt + 1)] for e in range(NE)]
        ends = [off_ref[base + e * (nt + 1) + 1] for e in range(NE)]

        def onehot(e, start, tok=tok):
            slot = start + lax.broadcasted_iota(I32, (W, TT), 0)
            return jnp.where(pos_ref[0, e:e + 1, tok] == slot, 1.0, 0.0).astype(BF16)

        def place(e, start, new, offs=offs):
            slot = start + lax.broadcasted_iota(I32, (W, 1), 0)
            old = xin_ref[e, 0, pl.ds(start, W), :].astype(F32)
            xin_ref[e, 0, pl.ds(start, W), :] = jnp.where(slot >= offs[e], new, old).astype(xin_ref.dtype)

        starts = [_window_start(offs[e], cap, W) for e in range(NE)]
        res = _dot(jnp.concatenate([onehot(e, starts[e]) for e in range(NE)], axis=0), h_ref[0, tok, :])
        for e in range(NE):
            place(e, starts[e], res[e * W:(e + 1) * W])

        n_extra = [(jnp.maximum(ends[e] - starts[e] - W, 0) + W - 1) // W for e in range(NE)]

        @pl.when(functools.reduce(jnp.maximum, n_extra) > 0)
        def _(tok=tok, starts=starts, n_extra=n_extra, onehot=onehot, place=place):
            for e in range(NE):
                def extra(c, carry, e=e):
                    start = _window_start(starts[e] + W + c * W, cap, W)
                    place(e, start, _dot(onehot(e, start), h_ref[0, tok, :]))
                    return carry

                lax.fori_loop(0, n_extra[e], extra, 0)


def _gather(h2, pos, offs):
    B, S, D = h2.shape
    NE = N_EXPERTS
    cap = (EC_CAPACITY_FACTOR * S) // NE
    ts = MOE_TOKEN_TILE * MOE_STEP_TILES
    grid_spec = pltpu.PrefetchScalarGridSpec(
        num_scalar_prefetch=1,
        grid=(B, S // ts),
        in_specs=[pl.BlockSpec((1, ts, D), lambda b, j, o: (b, j, 0)),
                  pl.BlockSpec((1, NE, ts), lambda b, j, o: (b, 0, j))],
        out_specs=pl.BlockSpec((NE, 1, cap, D), lambda b, j, o: (0, b, 0, 0)),
    )
    return pl.pallas_call(
        _gather_kernel,
        grid_spec=grid_spec,
        out_shape=jax.ShapeDtypeStruct((NE, B, cap, D), BF16),
        compiler_params=_params(("parallel", "arbitrary"), V7X_VMEM_LIMIT),
        name="gather",
    )(offs.reshape(-1), h2, pos)


def _expert_kernel(x_ref, wg_ref, wu_ref, wd_ref, o_ref, acc_s, wg_s, wu_s, wd_s, *, row_tile):
    f = pl.program_id(2)
    nf = pl.num_programs(2)
    n_tiles = x_ref.shape[1] // row_tile

    def sweep(first, last):
        wg_s[...] = wg_ref[0].astype(BF16)
        wu_s[...] = wu_ref[0].astype(BF16)
        wd_s[...] = wd_ref[0].astype(BF16)

        def down(r, hid):
            y = _dot(hid, wd_s[...])
            if not first:
                y = y + acc_s[r, :]
            if last:
                o_ref[0, r, :] = y.astype(o_ref.dtype)
            else:
                acc_s[r, :] = y

        pending = None
        for i in range(n_tiles):
            r = slice(i * row_tile, (i + 1) * row_tile)
            xb = x_ref[0, r, :]
            g = _dot(xb, wg_s[...])
            u = _dot(xb, wu_s[...])
            if pending is not None:
                down(*pending)
            pending = (r, (g * _sigmoid(g) * u).astype(BF16))
        down(*pending)

    @pl.when(f == 0)
    def _():
        sweep(True, False)

    @pl.when(jnp.logical_and(f > 0, f < nf - 1))
    def _():
        sweep(False, False)

    @pl.when(f == nf - 1)
    def _():
        sweep(False, True)


def _experts(xin, w_gate, w_up, w_down):
    NE, R, D = xin.shape
    F = w_gate.shape[-1]
    tr = min(R, 2048)
    tf = 512
    row_tile = min(tr, 512)
    return pl.pallas_call(
        functools.partial(_expert_kernel, row_tile=row_tile),
        grid=(NE, R // tr, F // tf),
        in_specs=[pl.BlockSpec((1, tr, D), lambda e, r, f: (e, r, 0)),
                  pl.BlockSpec((1, D, tf), lambda e, r, f: (e, 0, f)),
                  pl.BlockSpec((1, D, tf), lambda e, r, f: (e, 0, f)),
                  pl.BlockSpec((1, tf, D), lambda e, r, f: (e, f, 0))],
        out_specs=pl.BlockSpec((1, tr, D), lambda e, r, f: (e, r, 0)),
        out_shape=jax.ShapeDtypeStruct((NE, R, D), BF16),
        scratch_shapes=[pltpu.VMEM((tr, D), F32), pltpu.VMEM((D, tf), BF16),
                        pltpu.VMEM((D, tf), BF16), pltpu.VMEM((tf, D), BF16)],
        compiler_params=_params(("parallel", "parallel", "arbitrary"), V7X_VMEM_LIMIT),
        name="experts",
    )(xin, w_gate, w_up, w_down)


def _combine_kernel(off_ref, y_ref, pos_ref, gate_ref, x_ref, g2_ref, o_ref):
    NE, _, cap, D = y_ref.shape
    TT, W = MOE_TOKEN_TILE, MOE_WINDOW
    b, j = pl.program_id(0), pl.program_id(1)
    nt = pl.num_programs(1) * MOE_STEP_TILES

    for t in range(MOE_STEP_TILES):
        tok = slice(t * TT, (t + 1) * TT)
        base = (b * NE) * (nt + 1) + j * MOE_STEP_TILES + t
        offs = [off_ref[base + e * (nt + 1)] for e in range(NE)]
        ends = [off_ref[base + e * (nt + 1) + 1] for e in range(NE)]
        starts = [_window_start(offs[e], cap, W) for e in range(NE)]

        def weighted_onehot(e, start, lo, tok=tok):
            slot = start + lax.broadcasted_iota(I32, (W, TT), 0)
            hit = (pos_ref[0, e:e + 1, tok] == slot) & (slot >= lo)
            return jnp.where(hit, gate_ref[0, e:e + 1, tok], 0.0).astype(BF16)

        scatter = jnp.concatenate([weighted_onehot(e, starts[e], 0) for e in range(NE)], axis=0)
        ystack = jnp.concatenate([y_ref[e, 0, pl.ds(starts[e], W), :] for e in range(NE)], axis=0)
        o_ref[0, tok, :] = x_ref[0, tok, :] + g2_ref[0] * _dot_tn(scatter, ystack)

        n_extra = [(jnp.maximum(ends[e] - starts[e] - W, 0) + W - 1) // W for e in range(NE)]

        @pl.when(functools.reduce(jnp.maximum, n_extra) > 0)
        def _(tok=tok, starts=starts, n_extra=n_extra, weighted_onehot=weighted_onehot):
            for e in range(NE):
                def extra(c, carry, e=e):
                    lo = starts[e] + W + c * W
                    start = _window_start(lo, cap, W)
                    part = _dot_tn(weighted_onehot(e, start, lo), y_ref[e, 0, pl.ds(start, W), :])
                    o_ref[0, tok, :] = o_ref[0, tok, :] + g2_ref[0] * part
                    return carry

                lax.fori_loop(0, n_extra[e], extra, 0)


def _combine(y, pos, gate, offs, x1, gate2):
    B, S, D = x1.shape
    NE, _, cap, _ = y.shape
    ts = MOE_TOKEN_TILE * MOE_STEP_TILES
    row = lambda b, j, o: (b, j, 0)
    grid_spec = pltpu.PrefetchScalarGridSpec(
        num_scalar_prefetch=1,
        grid=(B, S // ts),
        in_specs=[pl.BlockSpec((NE, 1, cap, D), lambda b, j, o: (0, b, 0, 0)),
                  pl.BlockSpec((1, NE, ts), lambda b, j, o: (b, 0, j)),
                  pl.BlockSpec((1, NE, ts), lambda b, j, o: (b, 0, j)),
                  pl.BlockSpec((1, ts, D), row),
                  pl.BlockSpec((1, 1, D), lambda b, j, o: (b, 0, 0))],
        out_specs=pl.BlockSpec((1, ts, D), row),
    )
    return pl.pallas_call(
        _combine_kernel,
        grid_spec=grid_spec,
        out_shape=jax.ShapeDtypeStruct((B, S, D), F32),
        compiler_params=_params(("parallel", "arbitrary"), V7X_VMEM_LIMIT),
        name="combine",
    )(offs.reshape(-1), y, pos, gate, x1, gate2)


def _inproj_weight(w_in):
    D = w_in.shape[0]
    H = MLSTM_HEADS
    gates = w_in[:, 1024:1040].reshape(D, 4, H).transpose(0, 2, 1).reshape(D, 4 * H)
    main = jnp.concatenate([w_in[:, :1024], w_in[:, 1040:2576], gates,
                            jnp.zeros((D, V7X_LANES - 16), w_in.dtype)], axis=1)
    return main.astype(BF16)


def kernel(x, c, w_ada, b_ada, norm1_g, w_in, conv_w, conv_b, w_q_blk, w_k_blk, w_v_blk, b_igate, b_fgate,
           mlstm_norm_g, mlstm_skip, q_norm_g, k_norm_g, rel_bias, w_out, norm2_g, w_router, b_router,
           w_gate, w_up, w_down):
    B, S, D = x.shape
    depth = w_ada.shape[0]
    diag = _attn_bias_diagonals(rel_bias)
    for l in range(depth):
        mod = _ada(c, w_ada[l], b_ada[l])
        shift1, scale1, gate1, shift2, scale2, gate2 = (
            mod[:, i * D:(i + 1) * D].reshape(B, 1, D) for i in range(N_MOD))

        x_m, o_pre, a_q, a_k, a_v, gates = _inproj(x, scale1, shift1, norm1_g[l].reshape(1, D),
                                                   _inproj_weight(w_in[l]), q_norm_g[l], k_norm_g[l])
        y_m = _mlstm(x_m, o_pre, gates, conv_w[l], conv_b[l], w_q_blk[l], w_k_blk[l], w_v_blk[l],
                     b_igate[l], b_fgate[l], mlstm_norm_g[l], mlstm_skip[l])
        y_a = _attn(a_q, a_k, a_v, diag)
        x1 = _outproj(y_m, y_a, x, gate1, w_out[l])

        h2, pos, aff, offs = _select(x1, scale2, shift2, norm2_g[l].reshape(1, D), w_router[l], b_router[l])
        xin = _gather(h2, pos, offs)
        NE, _, cap, _ = xin.shape
        y = _experts(xin.reshape(NE, B * cap, D), w_gate[l], w_up[l], w_down[l]).reshape(NE, B, cap, D)
        x = _combine(y, pos, aff, offs, x1, gate2)
    return x
```

```python
import functools
import math

import numpy as np
import jax
import jax.numpy as jnp
from jax import lax
from jax.experimental import pallas as pl
from jax.experimental.pallas import tpu as pltpu

F32 = jnp.float32
BF16 = jnp.bfloat16
I32 = jnp.int32

NORM_EPS = 1e-6
N_MOD = 6
MLSTM_HEADS = 4
MLSTM_HEAD_DIM = 128
MLSTM_QKV_BLOCK = 4
MLSTM_CONV = 5
MLSTM_CHUNK = 128
MLSTM_STEP_HEADS = 2
MLSTM_ONES_ROWS = 16
ATTN_HEADS = 8
ATTN_HEAD_DIM = 64
ATTN_WIDTH = ATTN_HEADS * ATTN_HEAD_DIM
DILATIONS = (1, 4, 16)
ATTN_HALF = 64
ATTN_QBLK = 128
ATTN_KBLK = 256
ATTN_DIAG = 512
ATTN_GROUP = 2
LOG2E = math.log2(math.e)
REL_BUCKETS = 32
REL_MAX_DIST = 1024
N_EXPERTS = 16
EC_CAPACITY_FACTOR = 2
MOE_TOKEN_TILE = 256
MOE_STEP_TILES = 4
MOE_WINDOW = 64
NEG_BIG = -1e30

V7X_LANES = 128
V7X_VMEM_LIMIT = 56 * 1024 * 1024


def _sigmoid(x):
    return 1.0 / (1.0 + jnp.exp(-x))


def _dot(a, b):
    return jnp.dot(a, b, preferred_element_type=F32)


def _dot_nt(a, b):
    return lax.dot_general(a, b, (((1,), (1,)), ((), ())), preferred_element_type=F32)


def _dot_tn(a, b):
    return lax.dot_general(a, b, (((0,), (0,)), ((), ())), preferred_element_type=F32)


def _split_bf16(x):
    hi = x.astype(BF16)
    lo = (x - hi.astype(F32)).astype(BF16)
    return hi, lo


def _params(sem, vmem=None):
    return pltpu.CompilerParams(dimension_semantics=sem, vmem_limit_bytes=vmem)


def _ada_kernel(c_ref, w_ref, b_ref, o_ref):
    c = c_ref[...]
    s = c * _sigmoid(c)
    o_ref[...] = jnp.dot(s, w_ref[...], preferred_element_type=F32,
                         precision=lax.Precision.HIGHEST) + b_ref[...]


def _ada(c, w, b):
    B, D = c.shape
    N = w.shape[1]
    tn = 1024
    return pl.pallas_call(
        _ada_kernel,
        grid=(N // tn,),
        in_specs=[pl.BlockSpec((B, D), lambda j: (0, 0)),
                  pl.BlockSpec((D, tn), lambda j: (0, j)),
                  pl.BlockSpec((1, tn), lambda j: (0, j))],
        out_specs=pl.BlockSpec((B, tn), lambda j: (0, j)),
        out_shape=jax.ShapeDtypeStruct((B, N), F32),
        compiler_params=_params(("arbitrary",)),
        name="ada",
    )(c, w, b.reshape(1, N))


def _modulated_norm(x, g, scale, shift):
    ms = jnp.mean(x * x, axis=-1, keepdims=True)
    return x * lax.rsqrt(ms + NORM_EPS) * (g * (1.0 + scale)) + shift


def _inproj_kernel(x_ref, sc_ref, sh_ref, g_ref, w_ref, hm_ref, qg_ref, kg_ref,
                   xm_ref, op_ref, q_ref, k_ref, v_ref, gt_ref):
    h = _modulated_norm(x_ref[0], g_ref[...], sc_ref[0], sh_ref[0]).astype(BF16)
    col = lambda i: _dot(h, w_ref[:, 512 * i:512 * (i + 1)])
    q = col(2)
    k = col(3)
    xm_ref[0] = col(0)
    op_ref[0] = col(1)
    v_ref[0] = col(4)
    gt_ref[0] = _dot(h, w_ref[:, 2560:2688]).T[:16, :]

    def head_norm(t, g):
        ms = _dot((t * t).astype(BF16), hm_ref[...])
        return t * lax.rsqrt(ms + NORM_EPS) * g

    q_ref[0] = head_norm(q, qg_ref[...])
    k_ref[0] = head_norm(k, kg_ref[...])


def _inproj(x, scale, shift, g, w, q_norm_g, k_norm_g):
    B, S, D = x.shape
    tm = 512
    W = ATTN_WIDTH
    hid = jnp.arange(W) // ATTN_HEAD_DIM
    head_mean = jnp.where(hid[:, None] == hid[None, :], 1.0 / ATTN_HEAD_DIM, 0.0).astype(BF16)
    qg = jnp.tile(q_norm_g, ATTN_HEADS).reshape(1, W) * (LOG2E / math.sqrt(ATTN_HEAD_DIM))
    kg = jnp.tile(k_norm_g, ATTN_HEADS).reshape(1, W)
    row = lambda b, i: (b, i, 0)
    vec = lambda b, i: (b, 0, 0)
    const = lambda b, i: (0, 0)
    outs = ([jax.ShapeDtypeStruct((B, S, 512), F32)] * 2 + [jax.ShapeDtypeStruct((B, S, W), F32)] * 3
            + [jax.ShapeDtypeStruct((B, 16, S), F32)])
    return pl.pallas_call(
        _inproj_kernel,
        grid=(B, S // tm),
        in_specs=[pl.BlockSpec((1, tm, D), row),
                  pl.BlockSpec((1, 1, D), vec),
                  pl.BlockSpec((1, 1, D), vec),
                  pl.BlockSpec((1, D), const),
                  pl.BlockSpec(w.shape, const),
                  pl.BlockSpec((W, W), const),
                  pl.BlockSpec((1, W), const),
                  pl.BlockSpec((1, W), const)],
        out_specs=[pl.BlockSpec((1, tm, 512), row)] * 5 + [pl.BlockSpec((1, 16, tm), lambda b, i: (b, 0, i))],
        out_shape=outs,
        compiler_params=_params(("parallel", "arbitrary"), V7X_VMEM_LIMIT),
        name="inproj",
    )(x, scale, shift, g, w, head_mean, qg, kg)


def _chunk_scan(x, op, reverse):
    n = x.shape[1]
    idx = lax.broadcasted_iota(I32, x.shape, 1) & (MLSTM_CHUNK - 1)
    s = 1
    while s < MLSTM_CHUNK:
        if reverse:
            r = pltpu.roll(x, n - s, 1)
            x = jnp.where(idx < MLSTM_CHUNK - s, op(x, r), x)
        else:
            r = pltpu.roll(x, s, 1)
            x = jnp.where(idx >= s, op(x, r), x)
        s *= 2
    return x


def _log_sigmoid(x):
    return jnp.minimum(x, 0.0) - jnp.log(1.0 + jnp.exp(-jnp.abs(x)))


def _mlstm_kernel(xm_ref, op_ref, grow_ref, brow_ref, cw_ref, cb_ref,
                  wqt_ref, wk_ref, wvt_ref, ng_ref, sk_ref, y_ref,
                  qt_s, k_s, vat_s, xc_s, hf_s, hb_s, row_s, col_s):
    S = xm_ref.shape[1]
    L = MLSTM_CHUNK
    NC = S // L
    E = MLSTM_HEAD_DIM
    A = E + MLSTM_ONES_ROWS
    NH = MLSTM_STEP_HEADS
    half = MLSTM_CONV // 2
    rows = lax.broadcasted_iota(I32, (S, E), 0)
    kj = lax.broadcasted_iota(I32, (L, L), 0)
    qi = lax.broadcasted_iota(I32, (L, L), 1)
    causal = (kj <= qi, kj >= qi)

    def projections(hh):
        lanes = slice(hh * E, (hh + 1) * E)
        xm = xm_ref[0, :, lanes]
        conv = xm * cw_ref[half:half + 1, lanes]
        for j in range(MLSTM_CONV):
            off = j - half
            if off == 0:
                continue
            shifted = pltpu.roll(xm, (-off) % S, 0)
            valid = rows >= -off if off < 0 else rows < S - off
            conv = conv + jnp.where(valid, shifted, 0.0) * cw_ref[j:j + 1, lanes]
        conv = conv + cb_ref[:, lanes]
        xc = conv * _sigmoid(conv)
        xc_s[hh] = xc
        xcb = xc.astype(BF16)
        qt_s[hh] = _dot_nt(wqt_ref[hh], xcb).astype(BF16)
        k_s[hh] = (_dot(xcb, wk_ref[hh]) * (1.0 / math.sqrt(E))).astype(BF16)
        vat_s[hh, :E, :] = _dot_nt(wvt_ref[hh], xm.astype(BF16))
        vat_s[hh, E:, :] = jnp.ones((MLSTM_ONES_ROWS, S), F32)

    def gate_stats(hh):
        gr = grow_ref[0, 4 * hh:4 * hh + 4, :] + brow_ref[hh]
        kind = lax.broadcasted_iota(I32, gr.shape, 0)
        gr = jnp.where((kind & 1) == 1, _log_sigmoid(gr), gr)
        b_f = _chunk_scan(gr[1:2], jnp.add, False)
        b_b = _chunk_scan(gr[3:4], jnp.add, True)
        u_f = gr[0:1] - b_f
        u_b = gr[2:3] - b_b
        m_f = _chunk_scan(u_f, jnp.maximum, False)
        m_b = _chunk_scan(u_b, jnp.maximum, True)
        row_s[hh] = jnp.concatenate([b_f, m_f, u_f, gr[1:2], b_b, m_b, u_b, gr[3:4]], axis=0)
        for k, stat in enumerate((u_f, u_b)):
            for c in range(NC):
                col_s[hh, k, c * L:(c + 1) * L, :] = jnp.broadcast_to(stat[:, c * L:(c + 1) * L], (L, L)).T

    def local_part(hh, c, dirn):
        r0 = c * L
        qt = qt_s[hh, :, pl.ds(r0, L)]
        kc = k_s[hh, pl.ds(r0, L), :]
        vat = vat_s[hh, :, pl.ds(r0, L)]
        rr = row_s[hh, :, pl.ds(r0, L)]
        b_r, m_r, u_r = (rr[4 * dirn + i:4 * dirn + i + 1, :] for i in range(3))
        g = jnp.sum(rr[4 * dirn + 3:4 * dirn + 4, :], axis=1, keepdims=True)
        m_loc = g + jnp.max(u_r, axis=1, keepdims=True)
        return dict(
            hh=hh, r0=r0, dirn=dirn, qt=qt, vat=vat.astype(BF16), b_r=b_r, m_r=m_r, g=g, m_loc=m_loc,
            u_c=col_s[hh, dirn, pl.ds(r0, L), :],
            s=_dot(kc, qt),
            d_state=_dot((vat * jnp.exp(g + u_r - m_loc)).astype(BF16), kc))

    def intra_part(t):
        p = jnp.exp(jnp.where(causal[t["dirn"]], t["u_c"] - t["m_r"], NEG_BIG)) * t.pop("s")
        t["y_loc"] = _dot(t["vat"], p.astype(BF16))

    def state_part(t, state):
        Ct, m = state
        x_st = _dot(Ct.astype(BF16), t["qt"])
        mm = jnp.maximum(m, t["m_r"])
        z = jnp.exp(m - mm) * x_st + jnp.exp(t["m_r"] - mm) * t["y_loc"]
        h = z[:E, :] / jnp.maximum(jnp.abs(z[E:E + 1, :]), jnp.exp(-t["b_r"] - mm))
        m_new = jnp.maximum(t["g"] + m, t["m_loc"])
        Ct_new = jnp.exp(t["g"] + m - m_new) * Ct + jnp.exp(t["m_loc"] - m_new) * t["d_state"]
        return h, (Ct_new, m_new)

    for hh in range(NH):
        projections(hh)
    for hh in range(NH):
        gate_stats(hh)
    work = [local_part(hh, c if dirn == 0 else NC - 1 - c, dirn)
            for c in range(NC) for hh in range(NH) for dirn in range(2)]
    for t in work:
        intra_part(t)
    out_s = (hf_s, hb_s)
    states = {(hh, dirn): (jnp.zeros((A, E), F32), jnp.zeros((1, 1), F32)) for hh in range(NH) for dirn in range(2)}
    for t in work:
        key = (t["hh"], t["dirn"])
        h, states[key] = state_part(t, states[key])
        out_s[t["dirn"]][t["hh"], :, pl.ds(t["r0"], L)] = h

    for hh in range(NH):
        lanes = slice(hh * E, (hh + 1) * E)
        h = (hf_s[hh] + hb_s[hh]).T
        hn = h * lax.rsqrt(jnp.mean(h * h, axis=-1, keepdims=True) + NORM_EPS) * ng_ref[:, lanes]
        y = (hn + sk_ref[:, lanes] * xc_s[hh]) * _sigmoid(op_ref[0, :, lanes])
        y_ref[0, :, lanes] = y.astype(y_ref.dtype)


def _blockdiag_dense(w_blk, transposed=False):
    E, Q = MLSTM_HEAD_DIM, MLSTM_QKV_BLOCK
    rows = w_blk.reshape(MLSTM_HEADS, E, Q)
    idx = np.arange(E)
    spread = jnp.asarray(idx[None, :] % Q == np.arange(Q)[:, None], w_blk.dtype)
    same_block = jnp.asarray(idx[:, None] // Q == idx[None, :] // Q, w_blk.dtype)
    out = 'hcr' if transposed else 'hrc'
    return jnp.einsum(f'hrj,jc->{out}', rows, spread, precision=lax.Precision.HIGHEST) * same_block


def _mlstm(x_m, o_pre, gates, conv_w, conv_b, w_q_blk, w_k_blk, w_v_blk, b_igate, b_fgate, norm_g, skip):
    B, S, W = x_m.shape
    H, E, NH = MLSTM_HEADS, MLSTM_HEAD_DIM, MLSTM_STEP_HEADS
    A = E + MLSTM_ONES_ROWS
    assert S % MLSTM_CHUNK == 0 and H % NH == 0
    bk = jnp.stack([b_igate[0], b_fgate[0], b_igate[1], b_fgate[1]], axis=0)
    brow = bk.T.reshape(H, 4, 1)
    wqt = _blockdiag_dense(w_q_blk, transposed=True).astype(BF16)
    wk = _blockdiag_dense(w_k_blk).astype(BF16)
    wvt = _blockdiag_dense(w_v_blk, transposed=True).astype(BF16)
    headcol = lambda b, h: (b, 0, h)
    perhead = lambda b, h: (h, 0, 0)
    lanes = lambda b, h: (0, h)
    return pl.pallas_call(
        _mlstm_kernel,
        grid=(B, H // NH),
        in_specs=[pl.BlockSpec((1, S, NH * E), headcol),
                  pl.BlockSpec((1, S, NH * E), headcol),
                  pl.BlockSpec((1, 4 * NH, S), lambda b, h: (b, h, 0)),
                  pl.BlockSpec((NH, 4, 1), perhead),
                  pl.BlockSpec((MLSTM_CONV, NH * E), lanes),
                  pl.BlockSpec((1, NH * E), lanes),
                  pl.BlockSpec((NH, E, E), perhead),
                  pl.BlockSpec((NH, E, E), perhead),
                  pl.BlockSpec((NH, E, E), perhead),
                  pl.BlockSpec((1, NH * E), lanes),
                  pl.BlockSpec((1, NH * E), lanes)],
        out_specs=pl.BlockSpec((1, S, NH * E), headcol),
        out_shape=jax.ShapeDtypeStruct((B, S, W), BF16),
        scratch_shapes=[pltpu.VMEM((NH, E, S), BF16), pltpu.VMEM((NH, S, E), BF16),
                        pltpu.VMEM((NH, A, S), F32),
                        pltpu.VMEM((NH, S, E), F32), pltpu.VMEM((NH, E, S), F32), pltpu.VMEM((NH, E, S), F32),
                        pltpu.VMEM((NH, 8, S), F32), pltpu.VMEM((NH, 2, S, E), F32)],
        compiler_params=_params(("parallel", "arbitrary"), V7X_VMEM_LIMIT),
        name="mlstm",
    )(x_m, o_pre, gates, brow, conv_w, conv_b.reshape(1, W), wqt, wk, wvt,
      norm_g.reshape(1, W), skip.reshape(1, W))


def _t5_bucket_static(rel):
    half = REL_BUCKETS // 2
    exact = half // 2
    n = np.abs(rel)
    log_ratio = (np.log(np.maximum(n, 1).astype(np.float32) / np.float32(exact))
                 / np.float32(math.log(REL_MAX_DIST / exact)))
    large = np.minimum(exact + (log_ratio * np.float32(half - exact)).astype(np.int32), half - 1)
    return np.where(rel > 0, half, 0) + np.where(n < exact, n, large)


def _attn_bias_diagonals(rel_bias):
    n = ATTN_DIAG
    nv = 3 * len(DILATIONS)
    x = np.arange(n)
    offset = np.where(x <= ATTN_KBLK, x, x - n)
    rel = offset[None, :] - ATTN_HALF * np.arange(3)[:, None]
    valid = np.tile(np.abs(rel) <= ATTN_HALF, (len(DILATIONS), 1))
    bucket = np.concatenate([_t5_bucket_static(rel * d) for d in DILATIONS], axis=0)
    onehot = (bucket[..., None] == np.arange(REL_BUCKETS)) & valid[..., None]
    w = jnp.einsum('vnb,bh->vhn', jnp.asarray(onehot, F32), rel_bias.astype(F32) * LOG2E,
                   precision=lax.Precision.HIGHEST)
    w = jnp.where(jnp.asarray(valid)[:, None, :], w, NEG_BIG)
    w = w.reshape(nv, ATTN_HEADS // 2, 2, n).transpose(1, 0, 2, 3)
    return w.reshape(ATTN_HEADS // 2, 2 * nv, n)


def _attn_kernel(q_ref, k_ref, v_ref, diag_ref, y_ref, q0_s, q1_s, k_s, v0_s, v1_s, o_s, l_s, bias_s):
    S = q_ref.shape[1]
    QB, KB = ATTN_QBLK, ATTN_KBLK
    lane = lax.broadcasted_iota(I32, (1, V7X_LANES), 1)
    in_head = (lane < ATTN_HEAD_DIM, lane >= ATTN_HEAD_DIM)
    q_s = (q0_s, q1_s)
    v_s = (v0_s, v1_s)

    pair = pl.program_id(1)

    @pl.when(pl.program_id(0) == 0)
    def _():
        for i in range(bias_s.shape[1]):
            rows_i = jnp.broadcast_to(diag_ref[0, i:i + 1, :], (QB, ATTN_DIAG))
            bias_s[pair, i] = pltpu.roll(rows_i, 0, 1, stride=1, stride_axis=0)[:, :KB]

    def strided(start, size, d):
        return pl.ds(start, size) if d == 1 else pl.ds(start, size, stride=d)

    for p, d in enumerate(DILATIONS):
        L = S // d
        for r in range(d):
            src = strided(r, L, d)
            dst = slice(r * L, (r + 1) * L)
            q = q_ref[0, src, :]
            v = v_ref[0, src, :]
            k_s[p, dst, :] = k_ref[0, src, :].astype(BF16)
            for a in range(2):
                q_s[a][p, dst, :] = jnp.where(in_head[a], q, 0.0).astype(BF16)
                v_s[a][p, dst, :] = jnp.where(in_head[a], v, 1.0).astype(BF16)

    def logits(p, d, r, qb):
        L = S // d
        nqb = L // QB
        nk = min(L, KB)
        if nqb == 1 or qb == 0:
            k0, variant = 0, 0
        elif qb == nqb - 1:
            k0, variant = L - nk, 2
        else:
            k0, variant = qb * QB - ATTN_HALF, 1
        qrows = slice(r * L + qb * QB, r * L + (qb + 1) * QB)
        krows = slice(r * L + k0, r * L + k0 + nk)
        kt = k_s[p, krows, :]
        s = [_dot_nt(q_s[a][p, qrows, :], kt) + bias_s[pair, p * 6 + variant * 2 + a][:, :nk] for a in range(2)]
        return dict(p=p, krows=krows, out_rows=strided(r + d * qb * QB, QB, d), s=s)

    def softmax(t):
        t["m"] = [jnp.max(s, axis=1, keepdims=True) for s in t["s"]]
        t["e"] = [jnp.exp2(s - m).astype(BF16) for s, m in zip(t.pop("s"), t["m"])]

    def outputs(t):
        p = t["p"]
        acc = [_dot(t["e"][a], v_s[a][p, t["krows"], :]) for a in range(2)]
        num = jnp.where(in_head[0], acc[0], acc[1])
        den = pltpu.roll(jnp.where(in_head[0], acc[1], acc[0]), ATTN_HEAD_DIM, 1)
        o_s[p, t["out_rows"], :] = num / den
        l_s[p, t["out_rows"], :] = jnp.where(in_head[0], t["m"][0], t["m"][1]) + jnp.log2(den)

    units = [(p, d, r, qb) for p, d in enumerate(DILATIONS) for r in range(d) for qb in range(S // d // QB)]
    prev = []
    for i in range(0, len(units), ATTN_GROUP):
        cur = [logits(*u) for u in units[i:i + ATTN_GROUP]]
        for t in prev:
            outputs(t)
        for t in cur:
            softmax(t)
        prev = cur
    for t in prev:
        outputs(t)

    mx = jnp.maximum(jnp.maximum(l_s[0], l_s[1]), l_s[2])
    num = jnp.zeros((S, V7X_LANES), F32)
    den = jnp.zeros((S, V7X_LANES), F32)
    for p in range(len(DILATIONS)):
        w = jnp.exp2(l_s[p] - mx)
        num = num + w * o_s[p]
        den = den + w
    y_ref[0] = (num / den).astype(y_ref.dtype)


def _attn(a_q, a_k, a_v, diag):
    B, S, W = a_q.shape
    P = ATTN_HEADS // 2
    NP = len(DILATIONS)
    pair = lambda b, p: (b, 0, p)
    blk = pl.BlockSpec((1, S, V7X_LANES), pair)
    return pl.pallas_call(
        _attn_kernel,
        grid=(B, P),
        in_specs=[blk, blk, blk, pl.BlockSpec((1,) + diag.shape[1:], lambda b, p: (p, 0, 0))],
        out_specs=blk,
        out_shape=jax.ShapeDtypeStruct((B, S, W), BF16),
        scratch_shapes=[pltpu.VMEM((NP, S, V7X_LANES), BF16)] * 5 + [pltpu.VMEM((NP, S, V7X_LANES), F32)] * 2
                       + [pltpu.VMEM((P, diag.shape[1], ATTN_QBLK, ATTN_KBLK), F32)],
        compiler_params=_params(("arbitrary", "arbitrary"), V7X_VMEM_LIMIT),
        name="attn",
    )(a_q, a_k, a_v, diag)


def _outproj_kernel(ym_ref, ya_ref, x_ref, g1_ref, w1_ref, w2_ref, o_ref):
    mix = _dot(ym_ref[0], w1_ref[...]) + _dot(ya_ref[0], w2_ref[...])
    o_ref[0] = x_ref[0] + g1_ref[0] * mix


def _outproj(y_m, y_a, x, gate1, w_out):
    B, S, D = x.shape
    Wm = y_m.shape[-1]
    tm = 512
    w1 = w_out[:Wm].astype(BF16)
    w2 = w_out[Wm:].astype(BF16)
    row = lambda b, i: (b, i, 0)
    const = lambda b, i: (0, 0)
    return pl.pallas_call(
        _outproj_kernel,
        grid=(B, S // tm),
        in_specs=[pl.BlockSpec((1, tm, Wm), row),
                  pl.BlockSpec((1, tm, y_a.shape[-1]), row),
                  pl.BlockSpec((1, tm, D), row),
                  pl.BlockSpec((1, 1, D), lambda b, i: (b, 0, 0)),
                  pl.BlockSpec(w1.shape, const),
                  pl.BlockSpec(w2.shape, const)],
        out_specs=pl.BlockSpec((1, tm, D), row),
        out_shape=jax.ShapeDtypeStruct((B, S, D), F32),
        compiler_params=_params(("parallel", "arbitrary"), V7X_VMEM_LIMIT),
        name="outproj",
    )(y_m, y_a, x, gate1, w1, w2)


def _select_kernel(x_ref, sc_ref, sh_ref, g_ref, whl_ref, br_ref, tri_ref,
                   h_ref, pos_ref, gate_ref, off_ref, *, cap):
    S = x_ref.shape[1]
    NE = N_EXPERTS
    h = _modulated_norm(x_ref[0], g_ref[...], sc_ref[0], sh_ref[0])
    hi, lo = _split_bf16(h)
    h_ref[0] = hi
    both = _dot(hi, whl_ref[...])
    logits = both[:, :V7X_LANES] + both[:, V7X_LANES:] + _dot(lo, whl_ref[:, :V7X_LANES])
    lt = logits.T[:NE, :] + br_ref[...]
    ex = jnp.exp(lt - jnp.max(lt, axis=0, keepdims=True))
    aff = ex / jnp.sum(ex, axis=0, keepdims=True)
    gate_ref[0] = aff

    bits = pltpu.bitcast(aff, I32)

    def count_ge(cand):
        return jnp.sum((bits >= cand).astype(F32), axis=1, keepdims=True)

    def search(i, v):
        shift = 27 - 3 * i
        best = v
        for c in range(1, 8):
            cand = v | lax.shift_left(jnp.int32(c), shift)
            best = jnp.where(count_ge(cand) >= cap, cand, best)
        return best

    top = jnp.full((NE, 1), 1 << 30, I32)
    thr = lax.fori_loop(0, 10, search, jnp.where(count_ge(top) >= cap, top, 0))
    gt = (bits > thr).astype(F32)
    eq = (bits == thr).astype(F32)
    need = cap - jnp.sum(gt, axis=1, keepdims=True)

    def prefix_count(mask):
        off = jnp.zeros((NE, 1), F32)
        parts, starts = [], []
        for j in range(S // V7X_LANES):
            t = mask[:, j * V7X_LANES:(j + 1) * V7X_LANES]
            starts.append(off)
            parts.append(_dot(t.astype(BF16), tri_ref[...]) + off)
            off = off + jnp.sum(t, axis=1, keepdims=True)
        return jnp.concatenate(parts, axis=1), starts + [off]

    sel = jnp.maximum(gt, jnp.where(prefix_count(eq)[0] < need, eq, 0.0))
    slot, starts = prefix_count(sel)
    pos_ref[0] = jnp.where(sel > 0.0, slot, -1.0).astype(I32)
    step = MOE_TOKEN_TILE // V7X_LANES
    off_ref[0] = jnp.concatenate(starts[::step], axis=1).astype(I32)


def _select(x1, scale, shift, g, w_router, b_router):
    B, S, D = x1.shape
    NE = N_EXPERTS
    cap = (EC_CAPACITY_FACTOR * S) // NE
    nt = S // MOE_TOKEN_TILE
    wpad = jnp.zeros((D, V7X_LANES), F32).at[:, :NE].set(w_router)
    whl = jnp.concatenate(_split_bf16(wpad), axis=1)
    ti = jnp.arange(V7X_LANES)
    tri = (ti[:, None] < ti[None, :]).astype(BF16)
    vec = lambda b: (b, 0, 0)
    const = lambda b: (0, 0)
    return pl.pallas_call(
        functools.partial(_select_kernel, cap=cap),
        grid=(B,),
        in_specs=[pl.BlockSpec((1, S, D), vec),
                  pl.BlockSpec((1, 1, D), vec),
                  pl.BlockSpec((1, 1, D), vec),
                  pl.BlockSpec((1, D), const),
                  pl.BlockSpec((D, 2 * V7X_LANES), const),
                  pl.BlockSpec((NE, 1), const),
                  pl.BlockSpec((V7X_LANES, V7X_LANES), const)],
        out_specs=[pl.BlockSpec((1, S, D), vec),
                   pl.BlockSpec((1, NE, S), vec),
                   pl.BlockSpec((1, NE, S), vec),
                   pl.BlockSpec((1, NE, nt + 1), vec)],
        out_shape=[jax.ShapeDtypeStruct((B, S, D), BF16),
                   jax.ShapeDtypeStruct((B, NE, S), I32),
                   jax.ShapeDtypeStruct((B, NE, S), F32),
                   jax.ShapeDtypeStruct((B, NE, nt + 1), I32)],
        compiler_params=_params(("parallel",), V7X_VMEM_LIMIT),
        name="select",
    )(x1, scale, shift, g, whl, b_router.reshape(NE, 1), tri)


def _window_start(off, cap, rows):
    return pl.multiple_of(jnp.minimum((off // 16) * 16, cap - rows), 16)


def _gather_kernel(off_ref, h_ref, pos_ref, xin_ref):
    NE, _, cap, D = xin_ref.shape
    TT, W = MOE_TOKEN_TILE, MOE_WINDOW
    b, j = pl.program_id(0), pl.program_id(1)
    nt = pl.num_programs(1) * MOE_STEP_TILES

    @pl.when(j == 0)
    def _():
        xin_ref[...] = jnp.zeros(xin_ref.shape, xin_ref.dtype)

    for t in range(MOE_STEP_TILES):
        tok = slice(t * TT, (t + 1) * TT)
        base = (b * NE) * (nt + 1) + j * MOE_STEP_TILES + t
        offs = [off_ref[base + e * (nt + 1)] for e in range(NE)]
        ends = [off_ref[base + e * (nt + 1) + 1] for e in range(NE)]

        def onehot(e, start, tok=tok):
            slot = start + lax.broadcasted_iota(I32, (W, TT), 0)
            return jnp.where(pos_ref[0, e:e + 1, tok] == slot, 1.0, 0.0).astype(BF16)

        def place(e, start, new, offs=offs):
            slot = start + lax.broadcasted_iota(I32, (W, 1), 0)
            old = xin_ref[e, 0, pl.ds(start, W), :].astype(F32)
            xin_ref[e, 0, pl.ds(start, W), :] = jnp.where(slot >= offs[e], new, old).astype(xin_ref.dtype)

        starts = [_window_start(offs[e], cap, W) for e in range(NE)]
        res = _dot(jnp.concatenate([onehot(e, starts[e]) for e in range(NE)], axis=0), h_ref[0, tok, :])
        for e in range(NE):
            place(e, starts[e], res[e * W:(e + 1) * W])

        n_extra = [(jnp.maximum(ends[e] - starts[e] - W, 0) + W - 1) // W for e in range(NE)]

        @pl.when(functools.reduce(jnp.maximum, n_extra) > 0)
        def _(tok=tok, starts=starts, n_extra=n_extra, onehot=onehot, place=place):
            for e in range(NE):
                def extra(c, carry, e=e):
                    start = _window_start(starts[e] + W + c * W, cap, W)
                    place(e, start, _dot(onehot(e, start), h_ref[0, tok, :]))
                    return carry

                lax.fori_loop(0, n_extra[e], extra, 0)


def _gather(h2, pos, offs):
    B, S, D = h2.shape
    NE = N_EXPERTS
    cap = (EC_CAPACITY_FACTOR * S) // NE
    ts = MOE_TOKEN_TILE * MOE_STEP_TILES
    grid_spec = pltpu.PrefetchScalarGridSpec(
        num_scalar_prefetch=1,
        grid=(B, S // ts),
        in_specs=[pl.BlockSpec((1, ts, D), lambda b, j, o: (b, j, 0)),
                  pl.BlockSpec((1, NE, ts), lambda b, j, o: (b, 0, j))],
        out_specs=pl.BlockSpec((NE, 1, cap, D), lambda b, j, o: (0, b, 0, 0)),
    )
    return pl.pallas_call(
        _gather_kernel,
        grid_spec=grid_spec,
        out_shape=jax.ShapeDtypeStruct((NE, B, cap, D), BF16),
        compiler_params=_params(("parallel", "arbitrary"), V7X_VMEM_LIMIT),
        name="gather",
    )(offs.reshape(-1), h2, pos)


def _expert_kernel(x_ref, wg_ref, wu_ref, wd_ref, o_ref, acc_s, wg_s, wu_s, wd_s, *, row_tile):
    f = pl.program_id(2)
    nf = pl.num_programs(2)
    n_tiles = x_ref.shape[1] // row_tile

    def sweep(first, last):
        wg_s[...] = wg_ref[0].astype(BF16)
        wu_s[...] = wu_ref[0].astype(BF16)
        wd_s[...] = wd_ref[0].astype(BF16)

        def down(r, hid):
            y = _dot(hid, wd_s[...])
            if not first:
                y = y + acc_s[r, :]
            if last:
                o_ref[0, r, :] = y.astype(o_ref.dtype)
            else:
                acc_s[r, :] = y

        pending = None
        for i in range(n_tiles):
            r = slice(i * row_tile, (i + 1) * row_tile)
            xb = x_ref[0, r, :]
            g = _dot(xb, wg_s[...])
            u = _dot(xb, wu_s[...])
            if pending is not None:
                down(*pending)
            pending = (r, (g * _sigmoid(g) * u).astype(BF16))
        down(*pending)

    @pl.when(f == 0)
    def _():
        sweep(True, False)

    @pl.when(jnp.logical_and(f > 0, f < nf - 1))
    def _():
        sweep(False, False)

    @pl.when(f == nf - 1)
    def _():
        sweep(False, True)


def _experts(xin, w_gate, w_up, w_down):
    NE, R, D = xin.shape
    F = w_gate.shape[-1]
    tr = min(R, 2048)
    tf = 512
    row_tile = min(tr, 512)
    return pl.pallas_call(
        functools.partial(_expert_kernel, row_tile=row_tile),
        grid=(NE, R // tr, F // tf),
        in_specs=[pl.BlockSpec((1, tr, D), lambda e, r, f: (e, r, 0)),
                  pl.BlockSpec((1, D, tf), lambda e, r, f: (e, 0, f)),
                  pl.BlockSpec((1, D, tf), lambda e, r, f: (e, 0, f)),
                  pl.BlockSpec((1, tf, D), lambda e, r, f: (e, f, 0))],
        out_specs=pl.BlockSpec((1, tr, D), lambda e, r, f: (e, r, 0)),
        out_shape=jax.ShapeDtypeStruct((NE, R, D), BF16),
        scratch_shapes=[pltpu.VMEM((tr, D), F32), pltpu.VMEM((D, tf), BF16),
                        pltpu.VMEM((D, tf), BF16), pltpu.VMEM((tf, D), BF16)],
        compiler_params=_params(("parallel", "parallel", "arbitrary"), V7X_VMEM_LIMIT),
        name="experts",
    )(xin, w_gate, w_up, w_down)


def _combine_kernel(off_ref, y_ref, pos_ref, gate_ref, x_ref, g2_ref, o_ref):
    NE, _, cap, D = y_ref.shape
    TT, W = MOE_TOKEN_TILE, MOE_WINDOW
    b, j = pl.program_id(0), pl.program_id(1)
    nt = pl.num_programs(1) * MOE_STEP_TILES

    for t in range(MOE_STEP_TILES):
        tok = slice(t * TT, (t + 1) * TT)
        base = (b * NE) * (nt + 1) + j * MOE_STEP_TILES + t
        offs = [off_ref[base + e * (nt + 1)] for e in range(NE)]
        ends = [off_ref[base + e * (nt + 1) + 1] for e in range(NE)]
        starts = [_window_start(offs[e], cap, W) for e in range(NE)]

        def weighted_onehot(e, start, lo, tok=tok):
            slot = start + lax.broadcasted_iota(I32, (W, TT), 0)
            hit = (pos_ref[0, e:e + 1, tok] == slot) & (slot >= lo)
            return jnp.where(hit, gate_ref[0, e:e + 1, tok], 0.0).astype(BF16)

        scatter = jnp.concatenate([weighted_onehot(e, starts[e], 0) for e in range(NE)], axis=0)
        ystack = jnp.concatenate([y_ref[e, 0, pl.ds(starts[e], W), :] for e in range(NE)], axis=0)
        o_ref[0, tok, :] = x_ref[0, tok, :] + g2_ref[0] * _dot_tn(scatter, ystack)

        n_extra = [(jnp.maximum(ends[e] - starts[e] - W, 0) + W - 1) // W for e in range(NE)]

        @pl.when(functools.reduce(jnp.maximum, n_extra) > 0)
        def _(tok=tok, starts=starts, n_extra=n_extra, weighted_onehot=weighted_onehot):
            for e in range(NE):
                def extra(c, carry, e=e):
                    lo = starts[e] + W + c * W
                    start = _window_start(lo, cap, W)
                    part = _dot_tn(weighted_onehot(e, start, lo), y_ref[e, 0, pl.ds(start, W), :])
                    o_ref[0, tok, :] = o_ref[0, tok, :] + g2_ref[0] * part
                    return carry

                lax.fori_loop(0, n_extra[e], extra, 0)


def _combine(y, pos, gate, offs, x1, gate2):
    B, S, D = x1.shape
    NE, _, cap, _ = y.shape
    ts = MOE_TOKEN_TILE * MOE_STEP_TILES
    row = lambda b, j, o: (b, j, 0)
    grid_spec = pltpu.PrefetchScalarGridSpec(
        num_scalar_prefetch=1,
        grid=(B, S // ts),
        in_specs=[pl.BlockSpec((NE, 1, cap, D), lambda b, j, o: (0, b, 0, 0)),
                  pl.BlockSpec((1, NE, ts), lambda b, j, o: (b, 0, j)),
                  pl.BlockSpec((1, NE, ts), lambda b, j, o: (b, 0, j)),
                  pl.BlockSpec((1, ts, D), row),
                  pl.BlockSpec((1, 1, D), lambda b, j, o: (b, 0, 0))],
        out_specs=pl.BlockSpec((1, ts, D), row),
    )
    return pl.pallas_call(
        _combine_kernel,
        grid_spec=grid_spec,
        out_shape=jax.ShapeDtypeStruct((B, S, D), F32),
        compiler_params=_params(("parallel", "arbitrary"), V7X_VMEM_LIMIT),
        name="combine",
    )(offs.reshape(-1), y, pos, gate, x1, gate2)


def _inproj_weight(w_in):
    D = w_in.shape[0]
    H = MLSTM_HEADS
    gates = w_in[:, 1024:1040].reshape(D, 4, H).transpose(0, 2, 1).reshape(D, 4 * H)
    main = jnp.concatenate([w_in[:, :1024], w_in[:, 1040:2576], gates,
                            jnp.zeros((D, V7X_LANES - 16), w_in.dtype)], axis=1)
    return main.astype(BF16)


def kernel(x, c, w_ada, b_ada, norm1_g, w_in, conv_w, conv_b, w_q_blk, w_k_blk, w_v_blk, b_igate, b_fgate,
           mlstm_norm_g, mlstm_skip, q_norm_g, k_norm_g, rel_bias, w_out, norm2_g, w_router, b_router,
           w_gate, w_up, w_down):
    B, S, D = x.shape
    depth = w_ada.shape[0]
    diag = _attn_bias_diagonals(rel_bias)
    for l in range(depth):
        mod = _ada(c, w_ada[l], b_ada[l])
        shift1, scale1, gate1, shift2, scale2, gate2 = (
            mod[:, i * D:(i + 1) * D].reshape(B, 1, D) for i in range(N_MOD))

        x_m, o_pre, a_q, a_k, a_v, gates = _inproj(x, scale1, shift1, norm1_g[l].reshape(1, D),
                                                   _inproj_weight(w_in[l]), q_norm_g[l], k_norm_g[l])
        y_m = _mlstm(x_m, o_pre, gates, conv_w[l], conv_b[l], w_q_blk[l], w_k_blk[l], w_v_blk[l],
                     b_igate[l], b_fgate[l], mlstm_norm_g[l], mlstm_skip[l])
        y_a = _attn(a_q, a_k, a_v, diag)
        x1 = _outproj(y_m, y_a, x, gate1, w_out[l])

        h2, pos, aff, offs = _select(x1, scale2, shift2, norm2_g[l].reshape(1, D), w_router[l], b_router[l])
        xin = _gather(h2, pos, offs)
        NE, _, cap, _ = xin.shape
        y = _experts(xin.reshape(NE, B * cap, D), w_gate[l], w_up[l], w_down[l]).reshape(NE, B, cap, D)
        x = _combine(y, pos, aff, offs, x1, gate2)
    return x
```

```python
import functools
import math

import numpy as np
import jax
import jax.numpy as jnp
from jax import lax
from jax.experimental import pallas as pl
from jax.experimental.pallas import tpu as pltpu

F32 = jnp.float32
BF16 = jnp.bfloat16
I32 = jnp.int32

NORM_EPS = 1e-6
N_MOD = 6
ROW_TILE = 512
OUT_ROW_TILE = 2048
MLSTM_HEADS = 4
MLSTM_HEAD_DIM = 128
MLSTM_WIDTH = MLSTM_HEADS * MLSTM_HEAD_DIM
N_GATES = 4 * MLSTM_HEADS
MLSTM_QKV_BLOCK = 4
MLSTM_CONV = 5
MLSTM_CHUNK = 128
MLSTM_STEP_HEADS = 2
MLSTM_ONES_ROWS = 16
ATTN_HEADS = 8
ATTN_HEAD_DIM = 64
ATTN_WIDTH = ATTN_HEADS * ATTN_HEAD_DIM
DILATIONS = (1, 4, 16)
ATTN_HALF = 64
ATTN_QBLK = 128
ATTN_KBLK = 256
ATTN_DIAG = 512
ATTN_GROUP = 2
LOG2E = math.log2(math.e)
REL_BUCKETS = 32
REL_MAX_DIST = 1024
N_EXPERTS = 16
EC_CAPACITY_FACTOR = 2
EXPERT_ROWS = 2048
EXPERT_ROW_TILE = 512
EXPERT_FF_TILE = 512
MOE_TOKEN_TILE = 256
MOE_STEP_TILES = 4
MOE_WINDOW = 64
NEG_BIG = -1e30

V7X_LANES = 128
V7X_VMEM_LIMIT = 56 * 1024 * 1024


def _sigmoid(x):
    return 1.0 / (1.0 + jnp.exp(-x))


def _dot(a, b):
    return jnp.dot(a, b, preferred_element_type=F32)


def _dot_nt(a, b):
    return lax.dot_general(a, b, (((1,), (1,)), ((), ())), preferred_element_type=F32)


def _dot_tn(a, b):
    return lax.dot_general(a, b, (((0,), (0,)), ((), ())), preferred_element_type=F32)


def _split_bf16(x):
    hi = x.astype(BF16)
    lo = (x - hi.astype(F32)).astype(BF16)
    return hi, lo


def _params(sem, vmem=None):
    return pltpu.CompilerParams(dimension_semantics=sem, vmem_limit_bytes=vmem)


def _ada_kernel(c_ref, w_ref, b_ref, o_ref):
    c = c_ref[...]
    s = c * _sigmoid(c)
    o_ref[...] = jnp.dot(s, w_ref[...], preferred_element_type=F32,
                         precision=lax.Precision.HIGHEST) + b_ref[...]


def _ada(c, w, b):
    B, D = c.shape
    N = w.shape[1]
    tn = 1024
    return pl.pallas_call(
        _ada_kernel,
        grid=(N // tn,),
        in_specs=[pl.BlockSpec((B, D), lambda j: (0, 0)),
                  pl.BlockSpec((D, tn), lambda j: (0, j)),
                  pl.BlockSpec((1, tn), lambda j: (0, j))],
        out_specs=pl.BlockSpec((B, tn), lambda j: (0, j)),
        out_shape=jax.ShapeDtypeStruct((B, N), F32),
        compiler_params=_params(("arbitrary",)),
        name="ada",
    )(c, w, b.reshape(1, N))


def _modulated_norm(x, g, scale, shift):
    ms = jnp.mean(x * x, axis=-1, keepdims=True)
    return x * lax.rsqrt(ms + NORM_EPS) * (g * (1.0 + scale)) + shift


def _inproj_kernel(x_ref, sc_ref, sh_ref, g_ref, w_ref, hm_ref, qg_ref, kg_ref,
                   xm_ref, op_ref, q_ref, k_ref, v_ref, gt_ref):
    h = _modulated_norm(x_ref[0], g_ref[...], sc_ref[0], sh_ref[0]).astype(BF16)
    cw = MLSTM_WIDTH
    col = lambda i: _dot(h, w_ref[:, cw * i:cw * (i + 1)])
    q = col(2)
    k = col(3)
    xm_ref[0] = col(0)
    op_ref[0] = col(1)
    v_ref[0] = col(4)
    gt_ref[0] = _dot(h, w_ref[:, 5 * cw:5 * cw + V7X_LANES]).T[:N_GATES, :]

    def head_norm(t, g):
        ms = _dot((t * t).astype(BF16), hm_ref[...])
        return t * lax.rsqrt(ms + NORM_EPS) * g

    q_ref[0] = head_norm(q, qg_ref[...])
    k_ref[0] = head_norm(k, kg_ref[...])


def _inproj(x, scale, shift, g, w, q_norm_g, k_norm_g):
    B, S, D = x.shape
    tm = ROW_TILE
    W = ATTN_WIDTH
    assert W == MLSTM_WIDTH
    hid = jnp.arange(W) // ATTN_HEAD_DIM
    head_mean = jnp.where(hid[:, None] == hid[None, :], 1.0 / ATTN_HEAD_DIM, 0.0).astype(BF16)
    qg = jnp.tile(q_norm_g, ATTN_HEADS).reshape(1, W) * (LOG2E / math.sqrt(ATTN_HEAD_DIM))
    kg = jnp.tile(k_norm_g, ATTN_HEADS).reshape(1, W)
    row = lambda b, i: (b, i, 0)
    vec = lambda b, i: (b, 0, 0)
    const = lambda b, i: (0, 0)
    outs = ([jax.ShapeDtypeStruct((B, S, MLSTM_WIDTH), F32)] * 2 + [jax.ShapeDtypeStruct((B, S, W), F32)] * 3
            + [jax.ShapeDtypeStruct((B, N_GATES, S), F32)])
    return pl.pallas_call(
        _inproj_kernel,
        grid=(B, S // tm),
        in_specs=[pl.BlockSpec((1, tm, D), row),
                  pl.BlockSpec((1, 1, D), vec),
                  pl.BlockSpec((1, 1, D), vec),
                  pl.BlockSpec((1, D), const),
                  pl.BlockSpec(w.shape, const),
                  pl.BlockSpec((W, W), const),
                  pl.BlockSpec((1, W), const),
                  pl.BlockSpec((1, W), const)],
        out_specs=[pl.BlockSpec((1, tm, W), row)] * 5 + [pl.BlockSpec((1, N_GATES, tm), lambda b, i: (b, 0, i))],
        out_shape=outs,
        compiler_params=_params(("parallel", "arbitrary"), V7X_VMEM_LIMIT),
        name="inproj",
    )(x, scale, shift, g, w, head_mean, qg, kg)


def _chunk_scan(x, op, reverse):
    n = x.shape[1]
    idx = lax.broadcasted_iota(I32, x.shape, 1) & (MLSTM_CHUNK - 1)
    s = 1
    while s < MLSTM_CHUNK:
        if reverse:
            r = pltpu.roll(x, n - s, 1)
            x = jnp.where(idx < MLSTM_CHUNK - s, op(x, r), x)
        else:
            r = pltpu.roll(x, s, 1)
            x = jnp.where(idx >= s, op(x, r), x)
        s *= 2
    return x


def _log_sigmoid(x):
    return jnp.minimum(x, 0.0) - jnp.log(1.0 + jnp.exp(-jnp.abs(x)))


def _mlstm_kernel(xm_ref, op_ref, grow_ref, brow_ref, cw_ref, cb_ref,
                  wqt_ref, wk_ref, wvt_ref, ng_ref, sk_ref, y_ref,
                  qt_s, k_s, vat_s, xc_s, hf_s, hb_s, row_s, col_s):
    S = xm_ref.shape[1]
    L = MLSTM_CHUNK
    NC = S // L
    E = MLSTM_HEAD_DIM
    A = E + MLSTM_ONES_ROWS
    NH = MLSTM_STEP_HEADS
    half = MLSTM_CONV // 2
    rows = lax.broadcasted_iota(I32, (S, E), 0)
    kj = lax.broadcasted_iota(I32, (L, L), 0)
    qi = lax.broadcasted_iota(I32, (L, L), 1)
    causal = (kj <= qi, kj >= qi)

    def projections(hh):
        lanes = slice(hh * E, (hh + 1) * E)
        xm = xm_ref[0, :, lanes]
        conv = xm * cw_ref[half:half + 1, lanes]
        for j in range(MLSTM_CONV):
            off = j - half
            if off == 0:
                continue
            shifted = pltpu.roll(xm, (-off) % S, 0)
            valid = rows >= -off if off < 0 else rows < S - off
            conv = conv + jnp.where(valid, shifted, 0.0) * cw_ref[j:j + 1, lanes]
        conv = conv + cb_ref[:, lanes]
        xc = conv * _sigmoid(conv)
        xc_s[hh] = xc
        xcb = xc.astype(BF16)
        qt_s[hh] = _dot_nt(wqt_ref[hh], xcb).astype(BF16)
        k_s[hh] = (_dot(xcb, wk_ref[hh]) * (1.0 / math.sqrt(E))).astype(BF16)
        vat_s[hh, :E, :] = _dot_nt(wvt_ref[hh], xm.astype(BF16))
        vat_s[hh, E:, :] = jnp.ones((MLSTM_ONES_ROWS, S), F32)

    def gate_stats(hh):
        gr = grow_ref[0, 4 * hh:4 * hh + 4, :] + brow_ref[hh]
        kind = lax.broadcasted_iota(I32, gr.shape, 0)
        gr = jnp.where((kind & 1) == 1, _log_sigmoid(gr), gr)
        b_f = _chunk_scan(gr[1:2], jnp.add, False)
        b_b = _chunk_scan(gr[3:4], jnp.add, True)
        u_f = gr[0:1] - b_f
        u_b = gr[2:3] - b_b
        m_f = _chunk_scan(u_f, jnp.maximum, False)
        m_b = _chunk_scan(u_b, jnp.maximum, True)
        row_s[hh] = jnp.concatenate([b_f, m_f, u_f, gr[1:2], b_b, m_b, u_b, gr[3:4]], axis=0)
        for k, stat in enumerate((u_f, u_b)):
            for c in range(NC):
                col_s[hh, k, c * L:(c + 1) * L, :] = jnp.broadcast_to(stat[:, c * L:(c + 1) * L], (L, L)).T

    def local_part(hh, c, dirn):
        r0 = c * L
        qt = qt_s[hh, :, pl.ds(r0, L)]
        kc = k_s[hh, pl.ds(r0, L), :]
        vat = vat_s[hh, :, pl.ds(r0, L)]
        rr = row_s[hh, :, pl.ds(r0, L)]
        b_r, m_r, u_r = (rr[4 * dirn + i:4 * dirn + i + 1, :] for i in range(3))
        g = jnp.sum(rr[4 * dirn + 3:4 * dirn + 4, :], axis=1, keepdims=True)
        m_loc = g + jnp.max(u_r, axis=1, keepdims=True)
        return dict(
            hh=hh, r0=r0, dirn=dirn, qt=qt, vat=vat.astype(BF16), b_r=b_r, m_r=m_r, g=g, m_loc=m_loc,
            u_c=col_s[hh, dirn, pl.ds(r0, L), :],
            s=_dot(kc, qt),
            d_state=_dot((vat * jnp.exp(g + u_r - m_loc)).astype(BF16), kc))

    def intra_part(t):
        p = jnp.exp(jnp.where(causal[t["dirn"]], t["u_c"] - t["m_r"], NEG_BIG)) * t.pop("s")
        t["y_loc"] = _dot(t["vat"], p.astype(BF16))

    def state_part(t, state):
        Ct, m = state
        x_st = _dot(Ct.astype(BF16), t["qt"])
        mm = jnp.maximum(m, t["m_r"])
        z = jnp.exp(m - mm) * x_st + jnp.exp(t["m_r"] - mm) * t["y_loc"]
        h = z[:E, :] / jnp.maximum(jnp.abs(z[E:E + 1, :]), jnp.exp(-t["b_r"] - mm))
        m_new = jnp.maximum(t["g"] + m, t["m_loc"])
        Ct_new = jnp.exp(t["g"] + m - m_new) * Ct + jnp.exp(t["m_loc"] - m_new) * t["d_state"]
        return h, (Ct_new, m_new)

    for hh in range(NH):
        projections(hh)
    for hh in range(NH):
        gate_stats(hh)
    work = [local_part(hh, c if dirn == 0 else NC - 1 - c, dirn)
            for c in range(NC) for hh in range(NH) for dirn in range(2)]
    for t in work:
        intra_part(t)
    out_s = (hf_s, hb_s)
    states = {(hh, dirn): (jnp.zeros((A, E), F32), jnp.zeros((1, 1), F32)) for hh in range(NH) for dirn in range(2)}
    for t in work:
        key = (t["hh"], t["dirn"])
        h, states[key] = state_part(t, states[key])
        out_s[t["dirn"]][t["hh"], :, pl.ds(t["r0"], L)] = h

    for hh in range(NH):
        lanes = slice(hh * E, (hh + 1) * E)
        h = (hf_s[hh] + hb_s[hh]).T
        hn = h * lax.rsqrt(jnp.mean(h * h, axis=-1, keepdims=True) + NORM_EPS) * ng_ref[:, lanes]
        y = (hn + sk_ref[:, lanes] * xc_s[hh]) * _sigmoid(op_ref[0, :, lanes])
        y_ref[0, :, lanes] = y.astype(y_ref.dtype)


def _blockdiag_dense(w_blk, transposed=False):
    E, Q = MLSTM_HEAD_DIM, MLSTM_QKV_BLOCK
    rows = w_blk.reshape(MLSTM_HEADS, E, Q)
    idx = np.arange(E)
    spread = jnp.asarray(idx[None, :] % Q == np.arange(Q)[:, None], w_blk.dtype)
    same_block = jnp.asarray(idx[:, None] // Q == idx[None, :] // Q, w_blk.dtype)
    out = 'hcr' if transposed else 'hrc'
    return jnp.einsum(f'hrj,jc->{out}', rows, spread, precision=lax.Precision.HIGHEST) * same_block


def _mlstm(x_m, o_pre, gates, conv_w, conv_b, w_q_blk, w_k_blk, w_v_blk, b_igate, b_fgate, norm_g, skip):
    B, S, W = x_m.shape
    H, E, NH = MLSTM_HEADS, MLSTM_HEAD_DIM, MLSTM_STEP_HEADS
    A = E + MLSTM_ONES_ROWS
    assert S % MLSTM_CHUNK == 0 and H % NH == 0
    bk = jnp.stack([b_igate[0], b_fgate[0], b_igate[1], b_fgate[1]], axis=0)
    brow = bk.T.reshape(H, 4, 1)
    wqt = _blockdiag_dense(w_q_blk, transposed=True).astype(BF16)
    wk = _blockdiag_dense(w_k_blk).astype(BF16)
    wvt = _blockdiag_dense(w_v_blk, transposed=True).astype(BF16)
    headcol = lambda b, h: (b, 0, h)
    perhead = lambda b, h: (h, 0, 0)
    lanes = lambda b, h: (0, h)
    return pl.pallas_call(
        _mlstm_kernel,
        grid=(B, H // NH),
        in_specs=[pl.BlockSpec((1, S, NH * E), headcol),
                  pl.BlockSpec((1, S, NH * E), headcol),
                  pl.BlockSpec((1, 4 * NH, S), lambda b, h: (b, h, 0)),
                  pl.BlockSpec((NH, 4, 1), perhead),
                  pl.BlockSpec((MLSTM_CONV, NH * E), lanes),
                  pl.BlockSpec((1, NH * E), lanes),
                  pl.BlockSpec((NH, E, E), perhead),
                  pl.BlockSpec((NH, E, E), perhead),
                  pl.BlockSpec((NH, E, E), perhead),
                  pl.BlockSpec((1, NH * E), lanes),
                  pl.BlockSpec((1, NH * E), lanes)],
        out_specs=pl.BlockSpec((1, S, NH * E), headcol),
        out_shape=jax.ShapeDtypeStruct((B, S, W), BF16),
        scratch_shapes=[pltpu.VMEM((NH, E, S), BF16), pltpu.VMEM((NH, S, E), BF16),
                        pltpu.VMEM((NH, A, S), F32),
                        pltpu.VMEM((NH, S, E), F32), pltpu.VMEM((NH, E, S), F32), pltpu.VMEM((NH, E, S), F32),
                        pltpu.VMEM((NH, 8, S), F32), pltpu.VMEM((NH, 2, S, E), F32)],
        compiler_params=_params(("parallel", "arbitrary"), V7X_VMEM_LIMIT),
        name="mlstm",
    )(x_m, o_pre, gates, brow, conv_w, conv_b.reshape(1, W), wqt, wk, wvt,
      norm_g.reshape(1, W), skip.reshape(1, W))


def _t5_bucket_static(rel):
    half = REL_BUCKETS // 2
    exact = half // 2
    n = np.abs(rel)
    log_ratio = (np.log(np.maximum(n, 1).astype(np.float32) / np.float32(exact))
                 / np.float32(math.log(REL_MAX_DIST / exact)))
    large = np.minimum(exact + (log_ratio * np.float32(half - exact)).astype(np.int32), half - 1)
    return np.where(rel > 0, half, 0) + np.where(n < exact, n, large)


def _attn_bias_diagonals(rel_bias):
    n = ATTN_DIAG
    nv = 3 * len(DILATIONS)
    x = np.arange(n)
    offset = np.where(x <= ATTN_KBLK, x, x - n)
    rel = offset[None, :] - ATTN_HALF * np.arange(3)[:, None]
    valid = np.tile(np.abs(rel) <= ATTN_HALF, (len(DILATIONS), 1))
    bucket = np.concatenate([_t5_bucket_static(rel * d) for d in DILATIONS], axis=0)
    onehot = (bucket[..., None] == np.arange(REL_BUCKETS)) & valid[..., None]
    w = jnp.einsum('vnb,bh->vhn', jnp.asarray(onehot, F32), rel_bias.astype(F32) * LOG2E,
                   precision=lax.Precision.HIGHEST)
    w = jnp.where(jnp.asarray(valid)[:, None, :], w, NEG_BIG)
    w = w.reshape(nv, ATTN_HEADS // 2, 2, n).transpose(1, 0, 2, 3)
    return w.reshape(ATTN_HEADS // 2, 2 * nv, n)


def _attn_kernel(q_ref, k_ref, v_ref, diag_ref, y_ref, q0_s, q1_s, k_s, v_s, o_s, l_s, bias_s):
    S = q_ref.shape[1]
    QB, KB = ATTN_QBLK, ATTN_KBLK
    lane = lax.broadcasted_iota(I32, (1, V7X_LANES), 1)
    in_head = (lane < ATTN_HEAD_DIM, lane >= ATTN_HEAD_DIM)
    q_s = (q0_s, q1_s)

    pair = pl.program_id(1)

    @pl.when(pl.program_id(0) == 0)
    def _():
        for i in range(bias_s.shape[1]):
            rows_i = jnp.broadcast_to(diag_ref[0, i:i + 1, :], (QB, ATTN_DIAG))
            bias_s[pair, i] = pltpu.roll(rows_i, 0, 1, stride=1, stride_axis=0)[:, :KB]

    def strided(start, size, d):
        return pl.ds(start, size) if d == 1 else pl.ds(start, size, stride=d)

    for p, d in enumerate(DILATIONS):
        L = S // d
        for r in range(d):
            src = strided(r, L, d)
            dst = slice(r * L, (r + 1) * L)
            q = q_ref[0, src, :]
            v = v_ref[0, src, :]
            k_s[p, dst, :] = k_ref[0, src, :].astype(BF16)
            for a in range(2):
                q_s[a][p, dst, :] = jnp.where(in_head[a], q, 0.0).astype(BF16)
                v_s[p, dst, a * V7X_LANES:(a + 1) * V7X_LANES] = jnp.where(in_head[a], v, 1.0).astype(BF16)

    def logits(p, d, r, qb):
        L = S // d
        nqb = L // QB
        nk = min(L, KB)
        if nqb == 1 or qb == 0:
            k0, variant = 0, 0
        elif qb == nqb - 1:
            k0, variant = L - nk, 2
        else:
            k0, variant = qb * QB - ATTN_HALF, 1
        qrows = slice(r * L + qb * QB, r * L + (qb + 1) * QB)
        krows = slice(r * L + k0, r * L + k0 + nk)
        both = _dot_nt(jnp.concatenate([q_s[a][p, qrows, :] for a in range(2)], axis=0), k_s[p, krows, :])
        s = [both[a * QB:(a + 1) * QB] + bias_s[pair, p * 6 + variant * 2 + a][:, :nk] for a in range(2)]
        return dict(p=p, krows=krows, out_rows=strided(r + d * qb * QB, QB, d), s=s)

    def softmax(t):
        t["m"] = [jnp.max(s, axis=1, keepdims=True) for s in t["s"]]
        t["e"] = [jnp.exp2(s - m).astype(BF16) for s, m in zip(t.pop("s"), t["m"])]

    def outputs(t):
        p = t["p"]
        both = _dot(jnp.concatenate(t["e"], axis=0), v_s[p, t["krows"], :])
        acc = [both[a * QB:(a + 1) * QB, a * V7X_LANES:(a + 1) * V7X_LANES] for a in range(2)]
        num = jnp.where(in_head[0], acc[0], acc[1])
        den = pltpu.roll(jnp.where(in_head[0], acc[1], acc[0]), ATTN_HEAD_DIM, 1)
        o_s[p, t["out_rows"], :] = num / den
        l_s[p, t["out_rows"], :] = jnp.where(in_head[0], t["m"][0], t["m"][1]) + jnp.log2(den)

    units = [(p, d, r, qb) for p, d in enumerate(DILATIONS) for r in range(d) for qb in range(S // d // QB)]
    prev = []
    for i in range(0, len(units), ATTN_GROUP):
        cur = [logits(*u) for u in units[i:i + ATTN_GROUP]]
        for t in prev:
            outputs(t)
        for t in cur:
            softmax(t)
        prev = cur
    for t in prev:
        outputs(t)

    mx = jnp.maximum(jnp.maximum(l_s[0], l_s[1]), l_s[2])
    num = jnp.zeros((S, V7X_LANES), F32)
    den = jnp.zeros((S, V7X_LANES), F32)
    for p in range(len(DILATIONS)):
        w = jnp.exp2(l_s[p] - mx)
        num = num + w * o_s[p]
        den = den + w
    y_ref[0] = (num / den).astype(y_ref.dtype)


def _attn(a_q, a_k, a_v, diag):
    B, S, W = a_q.shape
    P = ATTN_HEADS // 2
    NP = len(DILATIONS)
    pair = lambda b, p: (b, 0, p)
    blk = pl.BlockSpec((1, S, V7X_LANES), pair)
    return pl.pallas_call(
        _attn_kernel,
        grid=(B, P),
        in_specs=[blk, blk, blk, pl.BlockSpec((1,) + diag.shape[1:], lambda b, p: (p, 0, 0))],
        out_specs=blk,
        out_shape=jax.ShapeDtypeStruct((B, S, W), BF16),
        scratch_shapes=[pltpu.VMEM((NP, S, V7X_LANES), BF16)] * 3 + [pltpu.VMEM((NP, S, 2 * V7X_LANES), BF16)]
                       + [pltpu.VMEM((NP, S, V7X_LANES), F32)] * 2
                       + [pltpu.VMEM((P, diag.shape[1], ATTN_QBLK, ATTN_KBLK), F32)],
        compiler_params=_params(("arbitrary", "arbitrary"), V7X_VMEM_LIMIT),
        name="attn",
    )(a_q, a_k, a_v, diag)


def _outproj_kernel(ym_ref, ya_ref, x_ref, g1_ref, w1_ref, w2_ref, o_ref):
    mix = _dot(ym_ref[0], w1_ref[...]) + _dot(ya_ref[0], w2_ref[...])
    o_ref[0] = x_ref[0] + g1_ref[0] * mix


def _outproj(y_m, y_a, x, gate1, w_out):
    B, S, D = x.shape
    Wm = y_m.shape[-1]
    tm = OUT_ROW_TILE
    w1 = w_out[:Wm].astype(BF16)
    w2 = w_out[Wm:].astype(BF16)
    row = lambda b, i: (b, i, 0)
    const = lambda b, i: (0, 0)
    return pl.pallas_call(
        _outproj_kernel,
        grid=(B, S // tm),
        in_specs=[pl.BlockSpec((1, tm, Wm), row),
                  pl.BlockSpec((1, tm, y_a.shape[-1]), row),
                  pl.BlockSpec((1, tm, D), row),
                  pl.BlockSpec((1, 1, D), lambda b, i: (b, 0, 0)),
                  pl.BlockSpec(w1.shape, const),
                  pl.BlockSpec(w2.shape, const)],
        out_specs=pl.BlockSpec((1, tm, D), row),
        out_shape=jax.ShapeDtypeStruct((B, S, D), F32),
        compiler_params=_params(("parallel", "arbitrary"), V7X_VMEM_LIMIT),
        name="outproj",
    )(y_m, y_a, x, gate1, w1, w2)


def _select_kernel(x_ref, sc_ref, sh_ref, g_ref, whl_ref, br_ref, tri_ref,
                   h_ref, pos_ref, gate_ref, off_ref, *, cap):
    S = x_ref.shape[1]
    NE = N_EXPERTS
    h = _modulated_norm(x_ref[0], g_ref[...], sc_ref[0], sh_ref[0])
    hi, lo = _split_bf16(h)
    h_ref[0] = hi
    both = _dot(hi, whl_ref[...])
    logits = both[:, :V7X_LANES] + both[:, V7X_LANES:] + _dot(lo, whl_ref[:, :V7X_LANES])
    lt = logits.T[:NE, :] + br_ref[...]
    ex = jnp.exp(lt - jnp.max(lt, axis=0, keepdims=True))
    aff = ex / jnp.sum(ex, axis=0, keepdims=True)
    gate_ref[0] = aff

    bits = pltpu.bitcast(aff, I32)

    def count_ge(cand):
        return jnp.sum((bits >= cand).astype(F32), axis=1, keepdims=True)

    def search(i, v):
        shift = 27 - 3 * i
        best = v
        for c in range(1, 8):
            cand = v | lax.shift_left(jnp.int32(c), shift)
            best = jnp.where(count_ge(cand) >= cap, cand, best)
        return best

    top = jnp.full((NE, 1), 1 << 30, I32)
    thr = lax.fori_loop(0, 10, search, jnp.where(count_ge(top) >= cap, top, 0))
    gt = (bits > thr).astype(F32)
    eq = (bits == thr).astype(F32)
    need = cap - jnp.sum(gt, axis=1, keepdims=True)

    def prefix_count(mask):
        off = jnp.zeros((NE, 1), F32)
        parts, starts = [], []
        for j in range(S // V7X_LANES):
            t = mask[:, j * V7X_LANES:(j + 1) * V7X_LANES]
            starts.append(off)
            parts.append(_dot(t.astype(BF16), tri_ref[...]) + off)
            off = off + jnp.sum(t, axis=1, keepdims=True)
        return jnp.concatenate(parts, axis=1), starts + [off]

    sel = jnp.maximum(gt, jnp.where(prefix_count(eq)[0] < need, eq, 0.0))
    slot, starts = prefix_count(sel)
    pos_ref[0] = jnp.where(sel > 0.0, slot, -1.0).astype(I32)
    step = MOE_TOKEN_TILE // V7X_LANES
    off_ref[0] = jnp.concatenate(starts[::step], axis=1).astype(I32)


def _select(x1, scale, shift, g, w_router, b_router):
    B, S, D = x1.shape
    NE = N_EXPERTS
    cap = (EC_CAPACITY_FACTOR * S) // NE
    nt = S // MOE_TOKEN_TILE
    wpad = jnp.zeros((D, V7X_LANES), F32).at[:, :NE].set(w_router)
    whl = jnp.concatenate(_split_bf16(wpad), axis=1)
    ti = jnp.arange(V7X_LANES)
    tri = (ti[:, None] < ti[None, :]).astype(BF16)
    vec = lambda b: (b, 0, 0)
    const = lambda b: (0, 0)
    return pl.pallas_call(
        functools.partial(_select_kernel, cap=cap),
        grid=(B,),
        in_specs=[pl.BlockSpec((1, S, D), vec),
                  pl.BlockSpec((1, 1, D), vec),
                  pl.BlockSpec((1, 1, D), vec),
                  pl.BlockSpec((1, D), const),
                  pl.BlockSpec((D, 2 * V7X_LANES), const),
                  pl.BlockSpec((NE, 1), const),
                  pl.BlockSpec((V7X_LANES, V7X_LANES), const)],
        out_specs=[pl.BlockSpec((1, S, D), vec),
                   pl.BlockSpec((1, NE, S), vec),
                   pl.BlockSpec((1, NE, S), vec),
                   pl.BlockSpec((1, NE, nt + 1), vec)],
        out_shape=[jax.ShapeDtypeStruct((B, S, D), BF16),
                   jax.ShapeDtypeStruct((B, NE, S), I32),
                   jax.ShapeDtypeStruct((B, NE, S), F32),
                   jax.ShapeDtypeStruct((B, NE, nt + 1), I32)],
        compiler_params=_params(("parallel",), V7X_VMEM_LIMIT),
        name="select",
    )(x1, scale, shift, g, whl, b_router.reshape(NE, 1), tri)


def _window_start(off, cap, rows):
    return pl.multiple_of(jnp.minimum((off // 16) * 16, cap - rows), 16)


def _gather_kernel(off_ref, h_ref, pos_ref, xin_ref):
    NE, _, cap, D = xin_ref.shape
    TT, W = MOE_TOKEN_TILE, MOE_WINDOW
    b, j = pl.program_id(0), pl.program_id(1)
    nt = pl.num_programs(1) * MOE_STEP_TILES

    @pl.when(j == 0)
    def _():
        xin_ref[...] = jnp.zeros(xin_ref.shape, xin_ref.dtype)

    for t in range(MOE_STEP_TILES):
        tok = slice(t * TT, (t + 1) * TT)
        base = (b * NE) * (nt + 1) + j * MOE_STEP_TILES + t
        offs = [off_ref[base + e * (nt + 1)] for e in range(NE)]
        ends = [off_ref[base + e * (nt + 1) + 1] for e in range(NE)]

        def onehot(e, start, tok=tok):
            slot = start + lax.broadcasted_iota(I32, (W, TT), 0)
            return jnp.where(pos_ref[0, e:e + 1, tok] == slot, 1.0, 0.0).astype(BF16)

        def place(e, start, new, offs=offs):
            slot = start + lax.broadcasted_iota(I32, (W, 1), 0)
            old = xin_ref[e, 0, pl.ds(start, W), :].astype(F32)
            xin_ref[e, 0, pl.ds(start, W), :] = jnp.where(slot >= offs[e], new, old).astype(xin_ref.dtype)

        starts = [_window_start(offs[e], cap, W) for e in range(NE)]
        res = _dot(jnp.concatenate([onehot(e, starts[e]) for e in range(NE)], axis=0), h_ref[0, tok, :])
        for e in range(NE):
            place(e, starts[e], res[e * W:(e + 1) * W])

        n_extra = [(jnp.maximum(ends[e] - starts[e] - W, 0) + W - 1) // W for e in range(NE)]

        @pl.when(functools.reduce(jnp.maximum, n_extra) > 0)
        def _(tok=tok, starts=starts, n_extra=n_extra, onehot=onehot, place=place):
            for e in range(NE):
                def extra(c, carry, e=e):
                    start = _window_start(starts[e] + W + c * W, cap, W)
                    place(e, start, _dot(onehot(e, start), h_ref[0, tok, :]))
                    return carry

                lax.fori_loop(0, n_extra[e], extra, 0)


def _gather(h2, pos, offs):
    B, S, D = h2.shape
    NE = N_EXPERTS
    cap = (EC_CAPACITY_FACTOR * S) // NE
    ts = MOE_TOKEN_TILE * MOE_STEP_TILES
    grid_spec = pltpu.PrefetchScalarGridSpec(
        num_scalar_prefetch=1,
        grid=(B, S // ts),
        in_specs=[pl.BlockSpec((1, ts, D), lambda b, j, o: (b, j, 0)),
                  pl.BlockSpec((1, NE, ts), lambda b, j, o: (b, 0, j))],
        out_specs=pl.BlockSpec((NE, 1, cap, D), lambda b, j, o: (0, b, 0, 0)),
    )
    return pl.pallas_call(
        _gather_kernel,
        grid_spec=grid_spec,
        out_shape=jax.ShapeDtypeStruct((NE, B, cap, D), BF16),
        compiler_params=_params(("parallel", "arbitrary"), V7X_VMEM_LIMIT),
        name="gather",
    )(offs.reshape(-1), h2, pos)


def _expert_kernel(x_ref, wg_ref, wu_ref, wd_ref, o_ref, acc_s, wg_s, wu_s, wd_s, *, row_tile):
    f = pl.program_id(2)
    nf = pl.num_programs(2)
    n_tiles = x_ref.shape[1] // row_tile

    def sweep(first, last):
        wg_s[...] = wg_ref[0].astype(BF16)
        wu_s[...] = wu_ref[0].astype(BF16)
        wd_s[...] = wd_ref[0].astype(BF16)

        def down(r, hid):
            y = _dot(hid, wd_s[...])
            if not first:
                y = y + acc_s[r, :]
            if last:
                o_ref[0, r, :] = y.astype(o_ref.dtype)
            else:
                acc_s[r, :] = y

        pending = None
        for i in range(n_tiles):
            r = slice(i * row_tile, (i + 1) * row_tile)
            xb = x_ref[0, r, :]
            g = _dot(xb, wg_s[...])
            u = _dot(xb, wu_s[...])
            if pending is not None:
                down(*pending)
            pending = (r, (g * _sigmoid(g) * u).astype(BF16))
        down(*pending)

    @pl.when(f == 0)
    def _():
        sweep(True, False)

    @pl.when(jnp.logical_and(f > 0, f < nf - 1))
    def _():
        sweep(False, False)

    @pl.when(f == nf - 1)
    def _():
        sweep(False, True)


def _experts(xin, w_gate, w_up, w_down):
    NE, R, D = xin.shape
    F = w_gate.shape[-1]
    tr = min(R, EXPERT_ROWS)
    tf = EXPERT_FF_TILE
    row_tile = min(tr, EXPERT_ROW_TILE)
    return pl.pallas_call(
        functools.partial(_expert_kernel, row_tile=row_tile),
        grid=(NE, R // tr, F // tf),
        in_specs=[pl.BlockSpec((1, tr, D), lambda e, r, f: (e, r, 0)),
                  pl.BlockSpec((1, D, tf), lambda e, r, f: (e, 0, f)),
                  pl.BlockSpec((1, D, tf), lambda e, r, f: (e, 0, f)),
                  pl.BlockSpec((1, tf, D), lambda e, r, f: (e, f, 0))],
        out_specs=pl.BlockSpec((1, tr, D), lambda e, r, f: (e, r, 0)),
        out_shape=jax.ShapeDtypeStruct((NE, R, D), BF16),
        scratch_shapes=[pltpu.VMEM((tr, D), F32), pltpu.VMEM((D, tf), BF16),
                        pltpu.VMEM((D, tf), BF16), pltpu.VMEM((tf, D), BF16)],
        compiler_params=_params(("parallel", "parallel", "arbitrary"), V7X_VMEM_LIMIT),
        name="experts",
    )(xin, w_gate, w_up, w_down)


def _combine_kernel(off_ref, y_ref, pos_ref, gate_ref, x_ref, g2_ref, o_ref):
    NE, _, cap, D = y_ref.shape
    TT, W = MOE_TOKEN_TILE, MOE_WINDOW
    b, j = pl.program_id(0), pl.program_id(1)
    nt = pl.num_programs(1) * MOE_STEP_TILES

    for t in range(MOE_STEP_TILES):
        tok = slice(t * TT, (t + 1) * TT)
        base = (b * NE) * (nt + 1) + j * MOE_STEP_TILES + t
        offs = [off_ref[base + e * (nt + 1)] for e in range(NE)]
        ends = [off_ref[base + e * (nt + 1) + 1] for e in range(NE)]
        starts = [_window_start(offs[e], cap, W) for e in range(NE)]

        def weighted_onehot(e, start, lo, tok=tok):
            slot = start + lax.broadcasted_iota(I32, (W, TT), 0)
            hit = (pos_ref[0, e:e + 1, tok] == slot) & (slot >= lo)
            return jnp.where(hit, gate_ref[0, e:e + 1, tok], 0.0).astype(BF16)

        scatter = jnp.concatenate([weighted_onehot(e, starts[e], 0) for e in range(NE)], axis=0)
        ystack = jnp.concatenate([y_ref[e, 0, pl.ds(starts[e], W), :] for e in range(NE)], axis=0)
        o_ref[0, tok, :] = x_ref[0, tok, :] + g2_ref[0] * _dot_tn(scatter, ystack)

        n_extra = [(jnp.maximum(ends[e] - starts[e] - W, 0) + W - 1) // W for e in range(NE)]

        @pl.when(functools.reduce(jnp.maximum, n_extra) > 0)
        def _(tok=tok, starts=starts, n_extra=n_extra, weighted_onehot=weighted_onehot):
            for e in range(NE):
                def extra(c, carry, e=e):
                    lo = starts[e] + W + c * W
                    start = _window_start(lo, cap, W)
                    part = _dot_tn(weighted_onehot(e, start, lo), y_ref[e, 0, pl.ds(start, W), :])
                    o_ref[0, tok, :] = o_ref[0, tok, :] + g2_ref[0] * part
                    return carry

                lax.fori_loop(0, n_extra[e], extra, 0)


def _combine(y, pos, gate, offs, x1, gate2):
    B, S, D = x1.shape
    NE, _, cap, _ = y.shape
    ts = MOE_TOKEN_TILE * MOE_STEP_TILES
    row = lambda b, j, o: (b, j, 0)
    grid_spec = pltpu.PrefetchScalarGridSpec(
        num_scalar_prefetch=1,
        grid=(B, S // ts),
        in_specs=[pl.BlockSpec((NE, 1, cap, D), lambda b, j, o: (0, b, 0, 0)),
                  pl.BlockSpec((1, NE, ts), lambda b, j, o: (b, 0, j)),
                  pl.BlockSpec((1, NE, ts), lambda b, j, o: (b, 0, j)),
                  pl.BlockSpec((1, ts, D), row),
                  pl.BlockSpec((1, 1, D), lambda b, j, o: (b, 0, 0))],
        out_specs=pl.BlockSpec((1, ts, D), row),
    )
    return pl.pallas_call(
        _combine_kernel,
        grid_spec=grid_spec,
        out_shape=jax.ShapeDtypeStruct((B, S, D), F32),
        compiler_params=_params(("parallel", "arbitrary"), V7X_VMEM_LIMIT),
        name="combine",
    )(offs.reshape(-1), y, pos, gate, x1, gate2)


def _inproj_weight(w_in):
    D = w_in.shape[0]
    H = MLSTM_HEADS
    g0 = 2 * MLSTM_WIDTH
    gates = w_in[:, g0:g0 + N_GATES].reshape(D, 4, H).transpose(0, 2, 1).reshape(D, N_GATES)
    main = jnp.concatenate([w_in[:, :g0], w_in[:, g0 + N_GATES:], gates,
                            jnp.zeros((D, V7X_LANES - N_GATES), w_in.dtype)], axis=1)
    return main.astype(BF16)


def kernel(x, c, w_ada, b_ada, norm1_g, w_in, conv_w, conv_b, w_q_blk, w_k_blk, w_v_blk, b_igate, b_fgate,
           mlstm_norm_g, mlstm_skip, q_norm_g, k_norm_g, rel_bias, w_out, norm2_g, w_router, b_router,
           w_gate, w_up, w_down):
    B, S, D = x.shape
    depth = w_ada.shape[0]
    diag = _attn_bias_diagonals(rel_bias)
    for l in range(depth):
        mod = _ada(c, w_ada[l], b_ada[l])
        shift1, scale1, gate1, shift2, scale2, gate2 = (
            mod[:, i * D:(i + 1) * D].reshape(B, 1, D) for i in range(N_MOD))

        x_m, o_pre, a_q, a_k, a_v, gates = _inproj(x, scale1, shift1, norm1_g[l].reshape(1, D),
                                                   _inproj_weight(w_in[l]), q_norm_g[l], k_norm_g[l])
        y_m = _mlstm(x_m, o_pre, gates, conv_w[l], conv_b[l], w_q_blk[l], w_k_blk[l], w_v_blk[l],
                     b_igate[l], b_fgate[l], mlstm_norm_g[l], mlstm_skip[l])
        y_a = _attn(a_q, a_k, a_v, diag)
        x1 = _outproj(y_m, y_a, x, gate1, w_out[l])

        h2, pos, aff, offs = _select(x1, scale2, shift2, norm2_g[l].reshape(1, D), w_router[l], b_router[l])
        xin = _gather(h2, pos, offs)
        NE, _, cap, _ = xin.shape
        y = _experts(xin.reshape(NE, B * cap, D), w_gate[l], w_up[l], w_down[l]).reshape(NE, B, cap, D)
        x = _combine(y, pos, aff, offs, x1, gate2)
    return x
```

```python
import functools
import math

import numpy as np
import jax
import jax.numpy as jnp
from jax import lax
from jax.experimental import pallas as pl
from jax.experimental.pallas import tpu as pltpu

F32 = jnp.float32
BF16 = jnp.bfloat16
I32 = jnp.int32

NORM_EPS = 1e-6
N_MOD = 6
ADA_COL_TILE = 1024
ROW_TILE = 1024
OUT_ROW_TILE = 2048
MLSTM_HEADS = 4
MLSTM_HEAD_DIM = 128
MLSTM_WIDTH = MLSTM_HEADS * MLSTM_HEAD_DIM
N_GATES = 4 * MLSTM_HEADS
MLSTM_QKV_BLOCK = 4
MLSTM_CONV = 5
MLSTM_CHUNK = 128
MLSTM_STEP_HEADS = 2
MLSTM_ONES_ROWS = 16
ATTN_HEADS = 8
ATTN_HEAD_DIM = 64
ATTN_WIDTH = ATTN_HEADS * ATTN_HEAD_DIM
DILATIONS = (1, 4, 16)
ATTN_HALF = 64
ATTN_QBLK = 128
ATTN_KBLK = 256
ATTN_DIAG = 512
ATTN_GROUP = 2
LOG2E = math.log2(math.e)
REL_BUCKETS = 32
REL_MAX_DIST = 1024
N_EXPERTS = 16
EC_CAPACITY_FACTOR = 2
EXPERT_ROWS = 2048
EXPERT_ROW_TILE = 512
EXPERT_FF_TILE = 512
MOE_TOKEN_TILE = 256
MOE_STEP_TILES = 4
GATHER_STEP_TILES = 4
MOE_WINDOW = 64
NEG_BIG = -1e30

V7X_LANES = 128
V7X_VMEM_LIMIT = 56 * 1024 * 1024


def _sigmoid(x):
    return 1.0 / (1.0 + jnp.exp(-x))


def _dot(a, b):
    return jnp.dot(a, b, preferred_element_type=F32)


def _dot_nt(a, b):
    return lax.dot_general(a, b, (((1,), (1,)), ((), ())), preferred_element_type=F32)


def _dot_tn(a, b):
    return lax.dot_general(a, b, (((0,), (0,)), ((), ())), preferred_element_type=F32)


def _split_bf16(x):
    hi = x.astype(BF16)
    lo = (x - hi.astype(F32)).astype(BF16)
    return hi, lo


def _params(sem, vmem=None):
    return pltpu.CompilerParams(dimension_semantics=sem, vmem_limit_bytes=vmem)


def _ada_kernel(c_ref, w_ref, b_ref, o_ref):
    c = c_ref[...]
    s = c * _sigmoid(c)
    o_ref[...] = jnp.dot(s, w_ref[...], preferred_element_type=F32,
                         precision=lax.Precision.HIGHEST) + b_ref[...]


def _ada(c, w, b):
    B, D = c.shape
    N = w.shape[1]
    tn = ADA_COL_TILE
    return pl.pallas_call(
        _ada_kernel,
        grid=(N // tn,),
        in_specs=[pl.BlockSpec((B, D), lambda j: (0, 0)),
                  pl.BlockSpec((D, tn), lambda j: (0, j)),
                  pl.BlockSpec((1, tn), lambda j: (0, j))],
        out_specs=pl.BlockSpec((B, tn), lambda j: (0, j)),
        out_shape=jax.ShapeDtypeStruct((B, N), F32),
        compiler_params=_params(("arbitrary",)),
        name="ada",
    )(c, w, b.reshape(1, N))


def _modulated_norm(x, g, scale, shift):
    ms = jnp.mean(x * x, axis=-1, keepdims=True)
    return x * lax.rsqrt(ms + NORM_EPS) * (g * (1.0 + scale)) + shift


def _inproj_kernel(x_ref, sc_ref, sh_ref, g_ref, w_ref, hm_ref, qg_ref, kg_ref,
                   xm_ref, op_ref, q_ref, k_ref, v_ref, gt_ref):
    h = _modulated_norm(x_ref[0], g_ref[...], sc_ref[0], sh_ref[0]).astype(BF16)
    cw = MLSTM_WIDTH
    col = lambda i: _dot(h, w_ref[:, cw * i:cw * (i + 1)])
    q = col(2)
    k = col(3)
    xm_ref[0] = col(0)
    op_ref[0] = col(1)
    v_ref[0] = col(4)
    gt_ref[0] = _dot(h, w_ref[:, 5 * cw:5 * cw + V7X_LANES]).T[:N_GATES, :]

    def head_norm(t, g):
        ms = _dot((t * t).astype(BF16), hm_ref[...])
        return t * lax.rsqrt(ms + NORM_EPS) * g

    q_ref[0] = head_norm(q, qg_ref[...])
    k_ref[0] = head_norm(k, kg_ref[...])


def _inproj(x, scale, shift, g, w, q_norm_g, k_norm_g):
    B, S, D = x.shape
    tm = ROW_TILE
    W = ATTN_WIDTH
    assert W == MLSTM_WIDTH
    hid = jnp.arange(W) // ATTN_HEAD_DIM
    head_mean = jnp.where(hid[:, None] == hid[None, :], 1.0 / ATTN_HEAD_DIM, 0.0).astype(BF16)
    qg = jnp.tile(q_norm_g, ATTN_HEADS).reshape(1, W) * (LOG2E / math.sqrt(ATTN_HEAD_DIM))
    kg = jnp.tile(k_norm_g, ATTN_HEADS).reshape(1, W)
    row = lambda b, i: (b, i, 0)
    vec = lambda b, i: (b, 0, 0)
    const = lambda b, i: (0, 0)
    outs = ([jax.ShapeDtypeStruct((B, S, MLSTM_WIDTH), F32)] * 2 + [jax.ShapeDtypeStruct((B, S, W), F32)] * 3
            + [jax.ShapeDtypeStruct((B, N_GATES, S), F32)])
    return pl.pallas_call(
        _inproj_kernel,
        grid=(B, S // tm),
        in_specs=[pl.BlockSpec((1, tm, D), row),
                  pl.BlockSpec((1, 1, D), vec),
                  pl.BlockSpec((1, 1, D), vec),
                  pl.BlockSpec((1, D), const),
                  pl.BlockSpec(w.shape, const),
                  pl.BlockSpec((W, W), const),
                  pl.BlockSpec((1, W), const),
                  pl.BlockSpec((1, W), const)],
        out_specs=[pl.BlockSpec((1, tm, W), row)] * 5 + [pl.BlockSpec((1, N_GATES, tm), lambda b, i: (b, 0, i))],
        out_shape=outs,
        compiler_params=_params(("parallel", "arbitrary"), V7X_VMEM_LIMIT),
        name="inproj",
    )(x, scale, shift, g, w, head_mean, qg, kg)


def _chunk_scan(x, op, reverse):
    n = x.shape[1]
    idx = lax.broadcasted_iota(I32, x.shape, 1) & (MLSTM_CHUNK - 1)
    s = 1
    while s < MLSTM_CHUNK:
        if reverse:
            r = pltpu.roll(x, n - s, 1)
            x = jnp.where(idx < MLSTM_CHUNK - s, op(x, r), x)
        else:
            r = pltpu.roll(x, s, 1)
            x = jnp.where(idx >= s, op(x, r), x)
        s *= 2
    return x


def _log_sigmoid(x):
    return jnp.minimum(x, 0.0) - jnp.log(1.0 + jnp.exp(-jnp.abs(x)))


def _mlstm_kernel(xm_ref, op_ref, grow_ref, brow_ref, cw_ref, cb_ref,
                  wqt_ref, wk_ref, wvt_ref, ng_ref, sk_ref, y_ref,
                  qt_s, k_s, vat_s, xc_s, hf_s, hb_s, row_s, col_s):
    S = xm_ref.shape[1]
    L = MLSTM_CHUNK
    NC = S // L
    E = MLSTM_HEAD_DIM
    A = E + MLSTM_ONES_ROWS
    NH = MLSTM_STEP_HEADS
    half = MLSTM_CONV // 2
    rows = lax.broadcasted_iota(I32, (S, E), 0)
    kj = lax.broadcasted_iota(I32, (L, L), 0)
    qi = lax.broadcasted_iota(I32, (L, L), 1)
    causal = (kj <= qi, kj >= qi)

    def projections(hh):
        lanes = slice(hh * E, (hh + 1) * E)
        xm = xm_ref[0, :, lanes]
        conv = xm * cw_ref[half:half + 1, lanes]
        for j in range(MLSTM_CONV):
            off = j - half
            if off == 0:
                continue
            shifted = pltpu.roll(xm, (-off) % S, 0)
            valid = rows >= -off if off < 0 else rows < S - off
            conv = conv + jnp.where(valid, shifted, 0.0) * cw_ref[j:j + 1, lanes]
        conv = conv + cb_ref[:, lanes]
        xc = conv * _sigmoid(conv)
        xc_s[hh] = xc
        xcb = xc.astype(BF16)
        qt_s[hh] = _dot_nt(wqt_ref[hh], xcb).astype(BF16)
        k_s[hh] = (_dot(xcb, wk_ref[hh]) * (1.0 / math.sqrt(E))).astype(BF16)
        vat_s[hh, :E, :] = _dot_nt(wvt_ref[hh], xm.astype(BF16))
        vat_s[hh, E:, :] = jnp.ones((MLSTM_ONES_ROWS, S), F32)

    def gate_stats(hh):
        gr = grow_ref[0, 4 * hh:4 * hh + 4, :] + brow_ref[hh]
        kind = lax.broadcasted_iota(I32, gr.shape, 0)
        gr = jnp.where((kind & 1) == 1, _log_sigmoid(gr), gr)
        b_f = _chunk_scan(gr[1:2], jnp.add, False)
        b_b = _chunk_scan(gr[3:4], jnp.add, True)
        u_f = gr[0:1] - b_f
        u_b = gr[2:3] - b_b
        m_f = _chunk_scan(u_f, jnp.maximum, False)
        m_b = _chunk_scan(u_b, jnp.maximum, True)
        row_s[hh] = jnp.concatenate([b_f, m_f, u_f, gr[1:2], b_b, m_b, u_b, gr[3:4]], axis=0)
        for k, stat in enumerate((u_f, u_b)):
            for c in range(NC):
                col_s[hh, k, c * L:(c + 1) * L, :] = jnp.broadcast_to(stat[:, c * L:(c + 1) * L], (L, L)).T

    def local_part(hh, c, dirn):
        r0 = c * L
        qt = qt_s[hh, :, pl.ds(r0, L)]
        kc = k_s[hh, pl.ds(r0, L), :]
        vat = vat_s[hh, :, pl.ds(r0, L)]
        rr = row_s[hh, :, pl.ds(r0, L)]
        b_r, m_r, u_r = (rr[4 * dirn + i:4 * dirn + i + 1, :] for i in range(3))
        g = jnp.sum(rr[4 * dirn + 3:4 * dirn + 4, :], axis=1, keepdims=True)
        m_loc = g + jnp.max(u_r, axis=1, keepdims=True)
        return dict(
            hh=hh, r0=r0, dirn=dirn, qt=qt, vat=vat.astype(BF16), b_r=b_r, m_r=m_r, g=g, m_loc=m_loc,
            u_c=col_s[hh, dirn, pl.ds(r0, L), :],
            s=_dot(kc, qt),
            d_state=_dot((vat * jnp.exp(g + u_r - m_loc)).astype(BF16), kc))

    def intra_part(t):
        p = jnp.exp(jnp.where(causal[t["dirn"]], t["u_c"] - t["m_r"], NEG_BIG)) * t.pop("s")
        t["y_loc"] = _dot(t["vat"], p.astype(BF16))

    def state_part(t, state):
        Ct, m = state
        x_st = _dot(Ct.astype(BF16), t["qt"])
        mm = jnp.maximum(m, t["m_r"])
        z = jnp.exp(m - mm) * x_st + jnp.exp(t["m_r"] - mm) * t["y_loc"]
        h = z[:E, :] / jnp.maximum(jnp.abs(z[E:E + 1, :]), jnp.exp(-t["b_r"] - mm))
        m_new = jnp.maximum(t["g"] + m, t["m_loc"])
        Ct_new = jnp.exp(t["g"] + m - m_new) * Ct + jnp.exp(t["m_loc"] - m_new) * t["d_state"]
        return h, (Ct_new, m_new)

    for hh in range(NH):
        projections(hh)
    for hh in range(NH):
        gate_stats(hh)
    work = [local_part(hh, c if dirn == 0 else NC - 1 - c, dirn)
            for c in range(NC) for hh in range(NH) for dirn in range(2)]
    for t in work:
        intra_part(t)
    out_s = (hf_s, hb_s)
    states = {(hh, dirn): (jnp.zeros((A, E), F32), jnp.zeros((1, 1), F32)) for hh in range(NH) for dirn in range(2)}
    for t in work:
        key = (t["hh"], t["dirn"])
        h, states[key] = state_part(t, states[key])
        out_s[t["dirn"]][t["hh"], :, pl.ds(t["r0"], L)] = h

    for hh in range(NH):
        lanes = slice(hh * E, (hh + 1) * E)
        h = (hf_s[hh] + hb_s[hh]).T
        hn = h * lax.rsqrt(jnp.mean(h * h, axis=-1, keepdims=True) + NORM_EPS) * ng_ref[:, lanes]
        y = (hn + sk_ref[:, lanes] * xc_s[hh]) * _sigmoid(op_ref[0, :, lanes])
        y_ref[0, :, lanes] = y.astype(y_ref.dtype)


def _blockdiag_dense(w_blk, transposed=False):
    E, Q = MLSTM_HEAD_DIM, MLSTM_QKV_BLOCK
    rows = w_blk.reshape(MLSTM_HEADS, E, Q)
    idx = np.arange(E)
    spread = jnp.asarray(idx[None, :] % Q == np.arange(Q)[:, None], w_blk.dtype)
    same_block = jnp.asarray(idx[:, None] // Q == idx[None, :] // Q, w_blk.dtype)
    out = 'hcr' if transposed else 'hrc'
    return jnp.einsum(f'hrj,jc->{out}', rows, spread, precision=lax.Precision.HIGHEST) * same_block


def _mlstm(x_m, o_pre, gates, conv_w, conv_b, w_q_blk, w_k_blk, w_v_blk, b_igate, b_fgate, norm_g, skip):
    B, S, W = x_m.shape
    H, E, NH = MLSTM_HEADS, MLSTM_HEAD_DIM, MLSTM_STEP_HEADS
    A = E + MLSTM_ONES_ROWS
    assert S % MLSTM_CHUNK == 0 and H % NH == 0
    bk = jnp.stack([b_igate[0], b_fgate[0], b_igate[1], b_fgate[1]], axis=0)
    brow = bk.T.reshape(H, 4, 1)
    wqt = _blockdiag_dense(w_q_blk, transposed=True).astype(BF16)
    wk = _blockdiag_dense(w_k_blk).astype(BF16)
    wvt = _blockdiag_dense(w_v_blk, transposed=True).astype(BF16)
    headcol = lambda b, h: (b, 0, h)
    perhead = lambda b, h: (h, 0, 0)
    lanes = lambda b, h: (0, h)
    return pl.pallas_call(
        _mlstm_kernel,
        grid=(B, H // NH),
        in_specs=[pl.BlockSpec((1, S, NH * E), headcol),
                  pl.BlockSpec((1, S, NH * E), headcol),
                  pl.BlockSpec((1, 4 * NH, S), lambda b, h: (b, h, 0)),
                  pl.BlockSpec((NH, 4, 1), perhead),
                  pl.BlockSpec((MLSTM_CONV, NH * E), lanes),
                  pl.BlockSpec((1, NH * E), lanes),
                  pl.BlockSpec((NH, E, E), perhead),
                  pl.BlockSpec((NH, E, E), perhead),
                  pl.BlockSpec((NH, E, E), perhead),
                  pl.BlockSpec((1, NH * E), lanes),
                  pl.BlockSpec((1, NH * E), lanes)],
        out_specs=pl.BlockSpec((1, S, NH * E), headcol),
        out_shape=jax.ShapeDtypeStruct((B, S, W), BF16),
        scratch_shapes=[pltpu.VMEM((NH, E, S), BF16), pltpu.VMEM((NH, S, E), BF16),
                        pltpu.VMEM((NH, A, S), F32),
                        pltpu.VMEM((NH, S, E), F32), pltpu.VMEM((NH, E, S), F32), pltpu.VMEM((NH, E, S), F32),
                        pltpu.VMEM((NH, 8, S), F32), pltpu.VMEM((NH, 2, S, E), F32)],
        compiler_params=_params(("parallel", "arbitrary"), V7X_VMEM_LIMIT),
        name="mlstm",
    )(x_m, o_pre, gates, brow, conv_w, conv_b.reshape(1, W), wqt, wk, wvt,
      norm_g.reshape(1, W), skip.reshape(1, W))


def _t5_bucket_static(rel):
    half = REL_BUCKETS // 2
    exact = half // 2
    n = np.abs(rel)
    log_ratio = (np.log(np.maximum(n, 1).astype(np.float32) / np.float32(exact))
                 / np.float32(math.log(REL_MAX_DIST / exact)))
    large = np.minimum(exact + (log_ratio * np.float32(half - exact)).astype(np.int32), half - 1)
    return np.where(rel > 0, half, 0) + np.where(n < exact, n, large)


def _attn_bias_diagonals(rel_bias):
    n = ATTN_DIAG
    nv = 3 * len(DILATIONS)
    x = np.arange(n)
    offset = np.where(x <= ATTN_KBLK, x, x - n)
    rel = offset[None, :] - ATTN_HALF * np.arange(3)[:, None]
    valid = np.tile(np.abs(rel) <= ATTN_HALF, (len(DILATIONS), 1))
    bucket = np.concatenate([_t5_bucket_static(rel * d) for d in DILATIONS], axis=0)
    onehot = (bucket[..., None] == np.arange(REL_BUCKETS)) & valid[..., None]
    w = jnp.einsum('vnb,bh->vhn', jnp.asarray(onehot, F32), rel_bias.astype(F32) * LOG2E,
                   precision=lax.Precision.HIGHEST)
    w = jnp.where(jnp.asarray(valid)[:, None, :], w, NEG_BIG)
    w = w.reshape(nv, ATTN_HEADS // 2, 2, n).transpose(1, 0, 2, 3)
    return w.reshape(ATTN_HEADS // 2, 2 * nv, n)


def _attn_kernel(q_ref, k_ref, v_ref, diag_ref, y_ref, q0_s, q1_s, k_s, v0_s, v1_s, o_s, l_s, bias_s):
    S = q_ref.shape[1]
    QB, KB = ATTN_QBLK, ATTN_KBLK
    lane = lax.broadcasted_iota(I32, (1, V7X_LANES), 1)
    in_head = (lane < ATTN_HEAD_DIM, lane >= ATTN_HEAD_DIM)
    q_s = (q0_s, q1_s)
    v_s = (v0_s, v1_s)

    pair = pl.program_id(1)

    @pl.when(pl.program_id(0) == 0)
    def _():
        for i in range(bias_s.shape[1]):
            rows_i = jnp.broadcast_to(diag_ref[0, i:i + 1, :], (QB, ATTN_DIAG))
            bias_s[pair, i] = pltpu.roll(rows_i, 0, 1, stride=1, stride_axis=0)[:, :KB]

    def strided(start, size, d):
        return pl.ds(start, size) if d == 1 else pl.ds(start, size, stride=d)

    for p, d in enumerate(DILATIONS):
        L = S // d
        for r in range(d):
            src = strided(r, L, d)
            dst = slice(r * L, (r + 1) * L)
            q = q_ref[0, src, :]
            v = v_ref[0, src, :]
            k_s[p, dst, :] = k_ref[0, src, :].astype(BF16)
            for a in range(2):
                q_s[a][p, dst, :] = jnp.where(in_head[a], q, 0.0).astype(BF16)
                v_s[a][p, dst, :] = jnp.where(in_head[a], v, 1.0).astype(BF16)

    def logits(p, d, r, qb):
        L = S // d
        nqb = L // QB
        nk = min(L, KB)
        if nqb == 1 or qb == 0:
            k0, variant = 0, 0
        elif qb == nqb - 1:
            k0, variant = L - nk, 2
        else:
            k0, variant = qb * QB - ATTN_HALF, 1
        qrows = slice(r * L + qb * QB, r * L + (qb + 1) * QB)
        krows = slice(r * L + k0, r * L + k0 + nk)
        kt = k_s[p, krows, :]
        s = [_dot_nt(q_s[a][p, qrows, :], kt) + bias_s[pair, p * 6 + variant * 2 + a][:, :nk] for a in range(2)]
        return dict(p=p, krows=krows, out_rows=strided(r + d * qb * QB, QB, d), s=s)

    def softmax(t):
        t["m"] = [jnp.max(s, axis=1, keepdims=True) for s in t["s"]]
        t["e"] = [jnp.exp2(s - m).astype(BF16) for s, m in zip(t.pop("s"), t["m"])]

    def outputs(t):
        p = t["p"]
        acc = [_dot(t["e"][a], v_s[a][p, t["krows"], :]) for a in range(2)]
        num = jnp.where(in_head[0], acc[0], acc[1])
        den = pltpu.roll(jnp.where(in_head[0], acc[1], acc[0]), ATTN_HEAD_DIM, 1)
        o_s[p, t["out_rows"], :] = num / den
        l_s[p, t["out_rows"], :] = jnp.where(in_head[0], t["m"][0], t["m"][1]) + jnp.log2(den)

    units = [(p, d, r, qb) for p, d in enumerate(DILATIONS) for r in range(d) for qb in range(S // d // QB)]
    prev = []
    for i in range(0, len(units), ATTN_GROUP):
        cur = [logits(*u) for u in units[i:i + ATTN_GROUP]]
        for t in prev:
            outputs(t)
        for t in cur:
            softmax(t)
        prev = cur
    for t in prev:
        outputs(t)

    mx = jnp.maximum(jnp.maximum(l_s[0], l_s[1]), l_s[2])
    num = jnp.zeros((S, V7X_LANES), F32)
    den = jnp.zeros((S, V7X_LANES), F32)
    for p in range(len(DILATIONS)):
        w = jnp.exp2(l_s[p] - mx)
        num = num + w * o_s[p]
        den = den + w
    y_ref[0] = (num / den).astype(y_ref.dtype)


def _attn(a_q, a_k, a_v, diag):
    B, S, W = a_q.shape
    P = ATTN_HEADS // 2
    NP = len(DILATIONS)
    pair = lambda b, p: (b, 0, p)
    blk = pl.BlockSpec((1, S, V7X_LANES), pair)
    return pl.pallas_call(
        _attn_kernel,
        grid=(B, P),
        in_specs=[blk, blk, blk, pl.BlockSpec((1,) + diag.shape[1:], lambda b, p: (p, 0, 0))],
        out_specs=blk,
        out_shape=jax.ShapeDtypeStruct((B, S, W), BF16),
        scratch_shapes=[pltpu.VMEM((NP, S, V7X_LANES), BF16)] * 5 + [pltpu.VMEM((NP, S, V7X_LANES), F32)] * 2
                       + [pltpu.VMEM((P, diag.shape[1], ATTN_QBLK, ATTN_KBLK), F32)],
        compiler_params=_params(("arbitrary", "arbitrary"), V7X_VMEM_LIMIT),
        name="attn",
    )(a_q, a_k, a_v, diag)


def _outproj_kernel(ym_ref, ya_ref, x_ref, g1_ref, w1_ref, w2_ref, o_ref):
    mix = _dot(ym_ref[0], w1_ref[...]) + _dot(ya_ref[0], w2_ref[...])
    o_ref[0] = x_ref[0] + g1_ref[0] * mix


def _outproj(y_m, y_a, x, gate1, w_out):
    B, S, D = x.shape
    Wm = y_m.shape[-1]
    tm = OUT_ROW_TILE
    w1 = w_out[:Wm].astype(BF16)
    w2 = w_out[Wm:].astype(BF16)
    row = lambda b, i: (b, i, 0)
    const = lambda b, i: (0, 0)
    return pl.pallas_call(
        _outproj_kernel,
        grid=(B, S // tm),
        in_specs=[pl.BlockSpec((1, tm, Wm), row),
                  pl.BlockSpec((1, tm, y_a.shape[-1]), row),
                  pl.BlockSpec((1, tm, D), row),
                  pl.BlockSpec((1, 1, D), lambda b, i: (b, 0, 0)),
                  pl.BlockSpec(w1.shape, const),
                  pl.BlockSpec(w2.shape, const)],
        out_specs=pl.BlockSpec((1, tm, D), row),
        out_shape=jax.ShapeDtypeStruct((B, S, D), F32),
        compiler_params=_params(("parallel", "arbitrary"), V7X_VMEM_LIMIT),
        name="outproj",
    )(y_m, y_a, x, gate1, w1, w2)


def _select_kernel(x_ref, sc_ref, sh_ref, g_ref, whl_ref, br_ref, tri_ref,
                   h_ref, pos_ref, gate_ref, off_ref, *, cap):
    S = x_ref.shape[1]
    NE = N_EXPERTS
    h = _modulated_norm(x_ref[0], g_ref[...], sc_ref[0], sh_ref[0])
    hi, lo = _split_bf16(h)
    h_ref[0] = hi
    both = _dot(hi, whl_ref[...])
    logits = both[:, :V7X_LANES] + both[:, V7X_LANES:] + _dot(lo, whl_ref[:, :V7X_LANES])
    lt = logits.T[:NE, :] + br_ref[...]
    ex = jnp.exp(lt - jnp.max(lt, axis=0, keepdims=True))
    aff = ex / jnp.sum(ex, axis=0, keepdims=True)
    gate_ref[0] = aff

    bits = pltpu.bitcast(aff, I32)

    def count_ge(cand):
        return jnp.sum((bits >= cand).astype(F32), axis=1, keepdims=True)

    def search(i, v):
        shift = 27 - 3 * i
        best = v
        for c in range(1, 8):
            cand = v | lax.shift_left(jnp.int32(c), shift)
            best = jnp.where(count_ge(cand) >= cap, cand, best)
        return best

    top = jnp.full((NE, 1), 1 << 30, I32)
    thr = lax.fori_loop(0, 10, search, jnp.where(count_ge(top) >= cap, top, 0))
    gt = (bits > thr).astype(F32)
    eq = (bits == thr).astype(F32)
    need = cap - jnp.sum(gt, axis=1, keepdims=True)

    def prefix_count(mask):
        off = jnp.zeros((NE, 1), F32)
        parts, starts = [], []
        for j in range(S // V7X_LANES):
            t = mask[:, j * V7X_LANES:(j + 1) * V7X_LANES]
            starts.append(off)
            parts.append(_dot(t.astype(BF16), tri_ref[...]) + off)
            off = off + jnp.sum(t, axis=1, keepdims=True)
        return jnp.concatenate(parts, axis=1), starts + [off]

    sel = jnp.maximum(gt, jnp.where(prefix_count(eq)[0] < need, eq, 0.0))
    slot, starts = prefix_count(sel)
    pos_ref[0] = jnp.where(sel > 0.0, slot, -1.0).astype(I32)
    step = MOE_TOKEN_TILE // V7X_LANES
    off_ref[0] = jnp.concatenate(starts[::step], axis=1).astype(I32)


def _select(x1, scale, shift, g, w_router, b_router):
    B, S, D = x1.shape
    NE = N_EXPERTS
    cap = (EC_CAPACITY_FACTOR * S) // NE
    nt = S // MOE_TOKEN_TILE
    wpad = jnp.zeros((D, V7X_LANES), F32).at[:, :NE].set(w_router)
    whl = jnp.concatenate(_split_bf16(wpad), axis=1)
    ti = jnp.arange(V7X_LANES)
    tri = (ti[:, None] < ti[None, :]).astype(BF16)
    vec = lambda b: (b, 0, 0)
    const = lambda b: (0, 0)
    return pl.pallas_call(
        functools.partial(_select_kernel, cap=cap),
        grid=(B,),
        in_specs=[pl.BlockSpec((1, S, D), vec),
                  pl.BlockSpec((1, 1, D), vec),
                  pl.BlockSpec((1, 1, D), vec),
                  pl.BlockSpec((1, D), const),
                  pl.BlockSpec((D, 2 * V7X_LANES), const),
                  pl.BlockSpec((NE, 1), const),
                  pl.BlockSpec((V7X_LANES, V7X_LANES), const)],
        out_specs=[pl.BlockSpec((1, S, D), vec),
                   pl.BlockSpec((1, NE, S), vec),
                   pl.BlockSpec((1, NE, S), vec),
                   pl.BlockSpec((1, NE, nt + 1), vec)],
        out_shape=[jax.ShapeDtypeStruct((B, S, D), BF16),
                   jax.ShapeDtypeStruct((B, NE, S), I32),
                   jax.ShapeDtypeStruct((B, NE, S), F32),
                   jax.ShapeDtypeStruct((B, NE, nt + 1), I32)],
        compiler_params=_params(("parallel",), V7X_VMEM_LIMIT),
        name="select",
    )(x1, scale, shift, g, whl, b_router.reshape(NE, 1), tri)


def _window_start(off, cap, rows):
    return pl.multiple_of(jnp.minimum((off // 16) * 16, cap - rows), 16)


def _gather_kernel(off_ref, h_ref, pos_ref, xin_ref):
    NE, _, cap, D = xin_ref.shape
    TT, W = MOE_TOKEN_TILE, MOE_WINDOW
    b, j = pl.program_id(0), pl.program_id(1)
    nt = pl.num_programs(1) * GATHER_STEP_TILES

    @pl.when(j == 0)
    def _():
        xin_ref[...] = jnp.zeros(xin_ref.shape, xin_ref.dtype)

    for t in range(GATHER_STEP_TILES):
        tok = slice(t * TT, (t + 1) * TT)
        base = (b * NE) * (nt + 1) + j * GATHER_STEP_TILES + t
        offs = [off_ref[base + e * (nt + 1)] for e in range(NE)]
        ends = [off_ref[base + e * (nt + 1) + 1] for e in range(NE)]

        def onehot(e, start, tok=tok):
            slot = start + lax.broadcasted_iota(I32, (W, TT), 0)
            return jnp.where(pos_ref[0, e:e + 1, tok] == slot, 1.0, 0.0).astype(BF16)

        def place(e, start, new, offs=offs):
            slot = start + lax.broadcasted_iota(I32, (W, 1), 0)
            old = xin_ref[e, 0, pl.ds(start, W), :].astype(F32)
            xin_ref[e, 0, pl.ds(start, W), :] = jnp.where(slot >= offs[e], new, old).astype(xin_ref.dtype)

        starts = [_window_start(offs[e], cap, W) for e in range(NE)]
        res = _dot(jnp.concatenate([onehot(e, starts[e]) for e in range(NE)], axis=0), h_ref[0, tok, :])
        for e in range(NE):
            place(e, starts[e], res[e * W:(e + 1) * W])

        n_extra = [(jnp.maximum(ends[e] - starts[e] - W, 0) + W - 1) // W for e in range(NE)]

        @pl.when(functools.reduce(jnp.maximum, n_extra) > 0)
        def _(tok=tok, starts=starts, n_extra=n_extra, onehot=onehot, place=place):
            for e in range(NE):
                def extra(c, carry, e=e):
                    start = _window_start(starts[e] + W + c * W, cap, W)
                    place(e, start, _dot(onehot(e, start), h_ref[0, tok, :]))
                    return carry

                lax.fori_loop(0, n_extra[e], extra, 0)


def _gather(h2, pos, offs):
    B, S, D = h2.shape
    NE = N_EXPERTS
    cap = (EC_CAPACITY_FACTOR * S) // NE
    ts = MOE_TOKEN_TILE * GATHER_STEP_TILES
    grid_spec = pltpu.PrefetchScalarGridSpec(
        num_scalar_prefetch=1,
        grid=(B, S // ts),
        in_specs=[pl.BlockSpec((1, ts, D), lambda b, j, o: (b, j, 0)),
                  pl.BlockSpec((1, NE, ts), lambda b, j, o: (b, 0, j))],
        out_specs=pl.BlockSpec((NE, 1, cap, D), lambda b, j, o: (0, b, 0, 0)),
    )
    return pl.pallas_call(
        _gather_kernel,
        grid_spec=grid_spec,
        out_shape=jax.ShapeDtypeStruct((NE, B, cap, D), BF16),
        compiler_params=_params(("parallel", "arbitrary"), V7X_VMEM_LIMIT),
        name="gather",
    )(offs.reshape(-1), h2, pos)


def _expert_kernel(x_ref, wg_ref, wu_ref, wd_ref, o_ref, acc_s, wg_s, wu_s, wd_s, *, row_tile):
    f = pl.program_id(2)
    nf = pl.num_programs(2)
    n_tiles = x_ref.shape[1] // row_tile

    def sweep(first, last):
        wg_s[...] = wg_ref[0].astype(BF16)
        wu_s[...] = wu_ref[0].astype(BF16)
        wd_s[...] = wd_ref[0].astype(BF16)

        def down(r, hid):
            y = _dot(hid, wd_s[...])
            if not first:
                y = y + acc_s[r, :]
            if last:
                o_ref[0, r, :] = y.astype(o_ref.dtype)
            else:
                acc_s[r, :] = y

        pending = None
        for i in range(n_tiles):
            r = slice(i * row_tile, (i + 1) * row_tile)
            xb = x_ref[0, r, :]
            g = _dot(xb, wg_s[...])
            u = _dot(xb, wu_s[...])
            if pending is not None:
                down(*pending)
            pending = (r, (g * _sigmoid(g) * u).astype(BF16))
        down(*pending)

    @pl.when(f == 0)
    def _():
        sweep(True, False)

    @pl.when(jnp.logical_and(f > 0, f < nf - 1))
    def _():
        sweep(False, False)

    @pl.when(f == nf - 1)
    def _():
        sweep(False, True)


def _experts(xin, w_gate, w_up, w_down):
    NE, R, D = xin.shape
    F = w_gate.shape[-1]
    tr = min(R, EXPERT_ROWS)
    tf = EXPERT_FF_TILE
    row_tile = min(tr, EXPERT_ROW_TILE)
    return pl.pallas_call(
        functools.partial(_expert_kernel, row_tile=row_tile),
        grid=(NE, R // tr, F // tf),
        in_specs=[pl.BlockSpec((1, tr, D), lambda e, r, f: (e, r, 0)),
                  pl.BlockSpec((1, D, tf), lambda e, r, f: (e, 0, f)),
                  pl.BlockSpec((1, D, tf), lambda e, r, f: (e, 0, f)),
                  pl.BlockSpec((1, tf, D), lambda e, r, f: (e, f, 0))],
        out_specs=pl.BlockSpec((1, tr, D), lambda e, r, f: (e, r, 0)),
        out_shape=jax.ShapeDtypeStruct((NE, R, D), BF16),
        scratch_shapes=[pltpu.VMEM((tr, D), F32), pltpu.VMEM((D, tf), BF16),
                        pltpu.VMEM((D, tf), BF16), pltpu.VMEM((tf, D), BF16)],
        compiler_params=_params(("parallel", "parallel", "arbitrary"), V7X_VMEM_LIMIT),
        name="experts",
    )(xin, w_gate, w_up, w_down)


def _combine_kernel(off_ref, y_ref, pos_ref, gate_ref, x_ref, g2_ref, o_ref):
    NE, _, cap, D = y_ref.shape
    TT, W = MOE_TOKEN_TILE, MOE_WINDOW
    b, j = pl.program_id(0), pl.program_id(1)
    nt = pl.num_programs(1) * MOE_STEP_TILES

    for t in range(MOE_STEP_TILES):
        tok = slice(t * TT, (t + 1) * TT)
        base = (b * NE) * (nt + 1) + j * MOE_STEP_TILES + t
        offs = [off_ref[base + e * (nt + 1)] for e in range(NE)]
        ends = [off_ref[base + e * (nt + 1) + 1] for e in range(NE)]
        starts = [_window_start(offs[e], cap, W) for e in range(NE)]

        def weighted_onehot(e, start, lo, tok=tok):
            slot = start + lax.broadcasted_iota(I32, (W, TT), 0)
            hit = (pos_ref[0, e:e + 1, tok] == slot) & (slot >= lo)
            return jnp.where(hit, gate_ref[0, e:e + 1, tok], 0.0).astype(BF16)

        scatter = jnp.concatenate([weighted_onehot(e, starts[e], 0) for e in range(NE)], axis=0)
        ystack = jnp.concatenate([y_ref[e, 0, pl.ds(starts[e], W), :] for e in range(NE)], axis=0)
        o_ref[0, tok, :] = x_ref[0, tok, :] + g2_ref[0] * _dot_tn(scatter, ystack)

        n_extra = [(jnp.maximum(ends[e] - starts[e] - W, 0) + W - 1) // W for e in range(NE)]

        @pl.when(functools.reduce(jnp.maximum, n_extra) > 0)
        def _(tok=tok, starts=starts, n_extra=n_extra, weighted_onehot=weighted_onehot):
            for e in range(NE):
                def extra(c, carry, e=e):
                    lo = starts[e] + W + c * W
                    start = _window_start(lo, cap, W)
                    part = _dot_tn(weighted_onehot(e, start, lo), y_ref[e, 0, pl.ds(start, W), :])
                    o_ref[0, tok, :] = o_ref[0, tok, :] + g2_ref[0] * part
                    return carry

                lax.fori_loop(0, n_extra[e], extra, 0)


def _combine(y, pos, gate, offs, x1, gate2):
    B, S, D = x1.shape
    NE, _, cap, _ = y.shape
    ts = MOE_TOKEN_TILE * MOE_STEP_TILES
    row = lambda b, j, o: (b, j, 0)
    grid_spec = pltpu.PrefetchScalarGridSpec(
        num_scalar_prefetch=1,
        grid=(B, S // ts),
        in_specs=[pl.BlockSpec((NE, 1, cap, D), lambda b, j, o: (0, b, 0, 0)),
                  pl.BlockSpec((1, NE, ts), lambda b, j, o: (b, 0, j)),
                  pl.BlockSpec((1, NE, ts), lambda b, j, o: (b, 0, j)),
                  pl.BlockSpec((1, ts, D), row),
                  pl.BlockSpec((1, 1, D), lambda b, j, o: (b, 0, 0))],
        out_specs=pl.BlockSpec((1, ts, D), row),
    )
    return pl.pallas_call(
        _combine_kernel,
        grid_spec=grid_spec,
        out_shape=jax.ShapeDtypeStruct((B, S, D), F32),
        compiler_params=_params(("parallel", "arbitrary"), V7X_VMEM_LIMIT),
        name="combine",
    )(offs.reshape(-1), y, pos, gate, x1, gate2)


def _inproj_weight(w_in):
    D = w_in.shape[0]
    H = MLSTM_HEADS
    g0 = 2 * MLSTM_WIDTH
    gates = w_in[:, g0:g0 + N_GATES].reshape(D, 4, H).transpose(0, 2, 1).reshape(D, N_GATES)
    main = jnp.concatenate([w_in[:, :g0], w_in[:, g0 + N_GATES:], gates,
                            jnp.zeros((D, V7X_LANES - N_GATES), w_in.dtype)], axis=1)
    return main.astype(BF16)


def kernel(x, c, w_ada, b_ada, norm1_g, w_in, conv_w, conv_b, w_q_blk, w_k_blk, w_v_blk, b_igate, b_fgate,
           mlstm_norm_g, mlstm_skip, q_norm_g, k_norm_g, rel_bias, w_out, norm2_g, w_router, b_router,
           w_gate, w_up, w_down):
    B, S, D = x.shape
    depth = w_ada.shape[0]
    diag = _attn_bias_diagonals(rel_bias)
    for l in range(depth):
        mod = _ada(c, w_ada[l], b_ada[l])
        shift1, scale1, gate1, shift2, scale2, gate2 = (
            mod[:, i * D:(i + 1) * D].reshape(B, 1, D) for i in range(N_MOD))

        x_m, o_pre, a_q, a_k, a_v, gates = _inproj(x, scale1, shift1, norm1_g[l].reshape(1, D),
                                                   _inproj_weight(w_in[l]), q_norm_g[l], k_norm_g[l])
        y_m = _mlstm(x_m, o_pre, gates, conv_w[l], conv_b[l], w_q_blk[l], w_k_blk[l], w_v_blk[l],
                     b_igate[l], b_fgate[l], mlstm_norm_g[l], mlstm_skip[l])
        y_a = _attn(a_q, a_k, a_v, diag)
        x1 = _outproj(y_m, y_a, x, gate1, w_out[l])

        h2, pos, aff, offs = _select(x1, scale2, shift2, norm2_g[l].reshape(1, D), w_router[l], b_router[l])
        xin = _gather(h2, pos, offs)
        NE, _, cap, _ = xin.shape
        y = _experts(xin.reshape(NE, B * cap, D), w_gate[l], w_up[l], w_down[l]).reshape(NE, B, cap, D)
        x = _combine(y, pos, aff, offs, x1, gate2)
    return x
```

```python
import functools
import math

import numpy as np
import jax
import jax.numpy as jnp
from jax import lax
from jax.experimental import pallas as pl
from jax.experimental.pallas import tpu as pltpu

F32 = jnp.float32
BF16 = jnp.bfloat16
I32 = jnp.int32

NORM_EPS = 1e-6
N_MOD = 6
ADA_COL_TILE = 1024
ROW_TILE = 1024
OUT_ROW_TILE = 2048
MLSTM_HEADS = 4
MLSTM_HEAD_DIM = 128
MLSTM_WIDTH = MLSTM_HEADS * MLSTM_HEAD_DIM
N_GATES = 4 * MLSTM_HEADS
MLSTM_QKV_BLOCK = 4
MLSTM_CONV = 5
MLSTM_CHUNK = 128
MLSTM_STEP_HEADS = 4
MLSTM_ONES_ROWS = 16
ATTN_HEADS = 8
ATTN_HEAD_DIM = 64
ATTN_WIDTH = ATTN_HEADS * ATTN_HEAD_DIM
DILATIONS = (1, 4, 16)
ATTN_HALF = 64
ATTN_QBLK = 128
ATTN_KBLK = 256
ATTN_DIAG = 512
ATTN_GROUP = 2
LOG2E = math.log2(math.e)
REL_BUCKETS = 32
REL_MAX_DIST = 1024
N_EXPERTS = 16
EC_CAPACITY_FACTOR = 2
EXPERT_ROWS = 2048
EXPERT_ROW_TILE = 512
EXPERT_FF_TILE = 512
MOE_TOKEN_TILE = 256
MOE_STEP_TILES = 4
GATHER_STEP_TILES = 4
MOE_WINDOW = 64
NEG_BIG = -1e30

V7X_LANES = 128
V7X_VMEM_LIMIT = 56 * 1024 * 1024


def _sigmoid(x):
    return 1.0 / (1.0 + jnp.exp(-x))


def _dot(a, b):
    return jnp.dot(a, b, preferred_element_type=F32)


def _dot_nt(a, b):
    return lax.dot_general(a, b, (((1,), (1,)), ((), ())), preferred_element_type=F32)


def _dot_tn(a, b):
    return lax.dot_general(a, b, (((0,), (0,)), ((), ())), preferred_element_type=F32)


def _split_bf16(x):
    hi = x.astype(BF16)
    lo = (x - hi.astype(F32)).astype(BF16)
    return hi, lo


def _params(sem, vmem=None):
    return pltpu.CompilerParams(dimension_semantics=sem, vmem_limit_bytes=vmem)


def _ada_kernel(c_ref, w_ref, b_ref, o_ref):
    c = c_ref[...]
    s = c * _sigmoid(c)
    o_ref[...] = jnp.dot(s, w_ref[...], preferred_element_type=F32,
                         precision=lax.Precision.HIGHEST) + b_ref[...]


def _ada(c, w, b):
    B, D = c.shape
    N = w.shape[1]
    tn = ADA_COL_TILE
    return pl.pallas_call(
        _ada_kernel,
        grid=(N // tn,),
        in_specs=[pl.BlockSpec((B, D), lambda j: (0, 0)),
                  pl.BlockSpec((D, tn), lambda j: (0, j)),
                  pl.BlockSpec((1, tn), lambda j: (0, j))],
        out_specs=pl.BlockSpec((B, tn), lambda j: (0, j)),
        out_shape=jax.ShapeDtypeStruct((B, N), F32),
        compiler_params=_params(("arbitrary",)),
        name="ada",
    )(c, w, b.reshape(1, N))


def _modulated_norm(x, g, scale, shift):
    ms = jnp.mean(x * x, axis=-1, keepdims=True)
    return x * lax.rsqrt(ms + NORM_EPS) * (g * (1.0 + scale)) + shift


def _inproj_kernel(x_ref, sc_ref, sh_ref, g_ref, w_ref, hm_ref, qg_ref, kg_ref,
                   xm_ref, op_ref, q_ref, k_ref, v_ref, gt_ref):
    h = _modulated_norm(x_ref[0], g_ref[...], sc_ref[0], sh_ref[0]).astype(BF16)
    cw = MLSTM_WIDTH
    col = lambda i: _dot(h, w_ref[:, cw * i:cw * (i + 1)])
    q = col(2)
    k = col(3)
    xm_ref[0] = col(0)
    op_ref[0] = col(1)
    v_ref[0] = col(4)
    gt_ref[0] = _dot(h, w_ref[:, 5 * cw:5 * cw + V7X_LANES]).T[:N_GATES, :]

    def head_norm(t, g):
        ms = _dot((t * t).astype(BF16), hm_ref[...])
        return t * lax.rsqrt(ms + NORM_EPS) * g

    q_ref[0] = head_norm(q, qg_ref[...])
    k_ref[0] = head_norm(k, kg_ref[...])


def _inproj(x, scale, shift, g, w, q_norm_g, k_norm_g):
    B, S, D = x.shape
    tm = ROW_TILE
    W = ATTN_WIDTH
    assert W == MLSTM_WIDTH
    hid = jnp.arange(W) // ATTN_HEAD_DIM
    head_mean = jnp.where(hid[:, None] == hid[None, :], 1.0 / ATTN_HEAD_DIM, 0.0).astype(BF16)
    qg = jnp.tile(q_norm_g, ATTN_HEADS).reshape(1, W) * (LOG2E / math.sqrt(ATTN_HEAD_DIM))
    kg = jnp.tile(k_norm_g, ATTN_HEADS).reshape(1, W)
    row = lambda b, i: (b, i, 0)
    vec = lambda b, i: (b, 0, 0)
    const = lambda b, i: (0, 0)
    outs = ([jax.ShapeDtypeStruct((B, S, MLSTM_WIDTH), F32)] * 2 + [jax.ShapeDtypeStruct((B, S, W), F32)] * 3
            + [jax.ShapeDtypeStruct((B, N_GATES, S), F32)])
    return pl.pallas_call(
        _inproj_kernel,
        grid=(B, S // tm),
        in_specs=[pl.BlockSpec((1, tm, D), row),
                  pl.BlockSpec((1, 1, D), vec),
                  pl.BlockSpec((1, 1, D), vec),
                  pl.BlockSpec((1, D), const),
                  pl.BlockSpec(w.shape, const),
                  pl.BlockSpec((W, W), const),
                  pl.BlockSpec((1, W), const),
                  pl.BlockSpec((1, W), const)],
        out_specs=[pl.BlockSpec((1, tm, W), row)] * 5 + [pl.BlockSpec((1, N_GATES, tm), lambda b, i: (b, 0, i))],
        out_shape=outs,
        compiler_params=_params(("parallel", "arbitrary"), V7X_VMEM_LIMIT),
        name="inproj",
    )(x, scale, shift, g, w, head_mean, qg, kg)


def _chunk_scan(x, op, reverse):
    n = x.shape[1]
    idx = lax.broadcasted_iota(I32, x.shape, 1) & (MLSTM_CHUNK - 1)
    s = 1
    while s < MLSTM_CHUNK:
        if reverse:
            r = pltpu.roll(x, n - s, 1)
            x = jnp.where(idx < MLSTM_CHUNK - s, op(x, r), x)
        else:
            r = pltpu.roll(x, s, 1)
            x = jnp.where(idx >= s, op(x, r), x)
        s *= 2
    return x


def _log_sigmoid(x):
    return jnp.minimum(x, 0.0) - jnp.log(1.0 + jnp.exp(-jnp.abs(x)))


def _mlstm_kernel(xm_ref, op_ref, grow_ref, brow_ref, cw_ref, cb_ref,
                  wqt_ref, wk_ref, wvt_ref, ng_ref, sk_ref, y_ref,
                  qt_s, k_s, vat_s, xc_s, hf_s, hb_s, row_s, col_s):
    S = xm_ref.shape[1]
    L = MLSTM_CHUNK
    NC = S // L
    E = MLSTM_HEAD_DIM
    A = E + MLSTM_ONES_ROWS
    NH = MLSTM_STEP_HEADS
    half = MLSTM_CONV // 2
    rows = lax.broadcasted_iota(I32, (S, E), 0)
    kj = lax.broadcasted_iota(I32, (L, L), 0)
    qi = lax.broadcasted_iota(I32, (L, L), 1)
    causal = (kj <= qi, kj >= qi)

    def projections(hh):
        lanes = slice(hh * E, (hh + 1) * E)
        xm = xm_ref[0, :, lanes]
        conv = xm * cw_ref[half:half + 1, lanes]
        for j in range(MLSTM_CONV):
            off = j - half
            if off == 0:
                continue
            shifted = pltpu.roll(xm, (-off) % S, 0)
            valid = rows >= -off if off < 0 else rows < S - off
            conv = conv + jnp.where(valid, shifted, 0.0) * cw_ref[j:j + 1, lanes]
        conv = conv + cb_ref[:, lanes]
        xc = conv * _sigmoid(conv)
        xc_s[hh] = xc
        xcb = xc.astype(BF16)
        qt_s[hh] = _dot_nt(wqt_ref[hh], xcb).astype(BF16)
        k_s[hh] = (_dot(xcb, wk_ref[hh]) * (1.0 / math.sqrt(E))).astype(BF16)
        vat_s[hh, :E, :] = _dot_nt(wvt_ref[hh], xm.astype(BF16))
        vat_s[hh, E:, :] = jnp.ones((MLSTM_ONES_ROWS, S), F32)

    def gate_stats(hh):
        gr = grow_ref[0, 4 * hh:4 * hh + 4, :] + brow_ref[hh]
        kind = lax.broadcasted_iota(I32, gr.shape, 0)
        gr = jnp.where((kind & 1) == 1, _log_sigmoid(gr), gr)
        b_f = _chunk_scan(gr[1:2], jnp.add, False)
        b_b = _chunk_scan(gr[3:4], jnp.add, True)
        u_f = gr[0:1] - b_f
        u_b = gr[2:3] - b_b
        m_f = _chunk_scan(u_f, jnp.maximum, False)
        m_b = _chunk_scan(u_b, jnp.maximum, True)
        row_s[hh] = jnp.concatenate([b_f, m_f, u_f, gr[1:2], b_b, m_b, u_b, gr[3:4]], axis=0)
        for k, stat in enumerate((u_f, u_b)):
            for c in range(NC):
                col_s[hh, k, c * L:(c + 1) * L, :] = jnp.broadcast_to(stat[:, c * L:(c + 1) * L], (L, L)).T

    def local_part(hh, c, dirn):
        r0 = c * L
        qt = qt_s[hh, :, pl.ds(r0, L)]
        kc = k_s[hh, pl.ds(r0, L), :]
        vat = vat_s[hh, :, pl.ds(r0, L)]
        rr = row_s[hh, :, pl.ds(r0, L)]
        b_r, m_r, u_r = (rr[4 * dirn + i:4 * dirn + i + 1, :] for i in range(3))
        g = jnp.sum(rr[4 * dirn + 3:4 * dirn + 4, :], axis=1, keepdims=True)
        m_loc = g + jnp.max(u_r, axis=1, keepdims=True)
        return dict(
            hh=hh, r0=r0, dirn=dirn, qt=qt, vat=vat.astype(BF16), b_r=b_r, m_r=m_r, g=g, m_loc=m_loc,
            u_c=col_s[hh, dirn, pl.ds(r0, L), :],
            s=_dot(kc, qt),
            d_state=_dot((vat * jnp.exp(g + u_r - m_loc)).astype(BF16), kc))

    def intra_part(t):
        p = jnp.exp(jnp.where(causal[t["dirn"]], t["u_c"] - t["m_r"], NEG_BIG)) * t.pop("s")
        t["y_loc"] = _dot(t["vat"], p.astype(BF16))

    def state_part(t, state):
        Ct, m = state
        x_st = _dot(Ct.astype(BF16), t["qt"])
        mm = jnp.maximum(m, t["m_r"])
        z = jnp.exp(m - mm) * x_st + jnp.exp(t["m_r"] - mm) * t["y_loc"]
        h = z[:E, :] / jnp.maximum(jnp.abs(z[E:E + 1, :]), jnp.exp(-t["b_r"] - mm))
        m_new = jnp.maximum(t["g"] + m, t["m_loc"])
        Ct_new = jnp.exp(t["g"] + m - m_new) * Ct + jnp.exp(t["m_loc"] - m_new) * t["d_state"]
        return h, (Ct_new, m_new)

    for hh in range(NH):
        projections(hh)
    for hh in range(NH):
        gate_stats(hh)
    work = [local_part(hh, c if dirn == 0 else NC - 1 - c, dirn)
            for c in range(NC) for hh in range(NH) for dirn in range(2)]
    for t in work:
        intra_part(t)
    out_s = (hf_s, hb_s)
    states = {(hh, dirn): (jnp.zeros((A, E), F32), jnp.zeros((1, 1), F32)) for hh in range(NH) for dirn in range(2)}
    for t in work:
        key = (t["hh"], t["dirn"])
        h, states[key] = state_part(t, states[key])
        out_s[t["dirn"]][t["hh"], :, pl.ds(t["r0"], L)] = h

    for hh in range(NH):
        lanes = slice(hh * E, (hh + 1) * E)
        h = (hf_s[hh] + hb_s[hh]).T
        hn = h * lax.rsqrt(jnp.mean(h * h, axis=-1, keepdims=True) + NORM_EPS) * ng_ref[:, lanes]
        y = (hn + sk_ref[:, lanes] * xc_s[hh]) * _sigmoid(op_ref[0, :, lanes])
        y_ref[0, :, lanes] = y.astype(y_ref.dtype)


def _blockdiag_dense(w_blk, transposed=False):
    E, Q = MLSTM_HEAD_DIM, MLSTM_QKV_BLOCK
    rows = w_blk.reshape(MLSTM_HEADS, E, Q)
    idx = np.arange(E)
    spread = jnp.asarray(idx[None, :] % Q == np.arange(Q)[:, None], w_blk.dtype)
    same_block = jnp.asarray(idx[:, None] // Q == idx[None, :] // Q, w_blk.dtype)
    out = 'hcr' if transposed else 'hrc'
    return jnp.einsum(f'hrj,jc->{out}', rows, spread, precision=lax.Precision.HIGHEST) * same_block


def _mlstm(x_m, o_pre, gates, conv_w, conv_b, w_q_blk, w_k_blk, w_v_blk, b_igate, b_fgate, norm_g, skip):
    B, S, W = x_m.shape
    H, E, NH = MLSTM_HEADS, MLSTM_HEAD_DIM, MLSTM_STEP_HEADS
    A = E + MLSTM_ONES_ROWS
    assert S % MLSTM_CHUNK == 0 and H % NH == 0
    bk = jnp.stack([b_igate[0], b_fgate[0], b_igate[1], b_fgate[1]], axis=0)
    brow = bk.T.reshape(H, 4, 1)
    wqt = _blockdiag_dense(w_q_blk, transposed=True).astype(BF16)
    wk = _blockdiag_dense(w_k_blk).astype(BF16)
    wvt = _blockdiag_dense(w_v_blk, transposed=True).astype(BF16)
    headcol = lambda b, h: (b, 0, h)
    perhead = lambda b, h: (h, 0, 0)
    lanes = lambda b, h: (0, h)
    return pl.pallas_call(
        _mlstm_kernel,
        grid=(B, H // NH),
        in_specs=[pl.BlockSpec((1, S, NH * E), headcol),
                  pl.BlockSpec((1, S, NH * E), headcol),
                  pl.BlockSpec((1, 4 * NH, S), lambda b, h: (b, h, 0)),
                  pl.BlockSpec((NH, 4, 1), perhead),
                  pl.BlockSpec((MLSTM_CONV, NH * E), lanes),
                  pl.BlockSpec((1, NH * E), lanes),
                  pl.BlockSpec((NH, E, E), perhead),
                  pl.BlockSpec((NH, E, E), perhead),
                  pl.BlockSpec((NH, E, E), perhead),
                  pl.BlockSpec((1, NH * E), lanes),
                  pl.BlockSpec((1, NH * E), lanes)],
        out_specs=pl.BlockSpec((1, S, NH * E), headcol),
        out_shape=jax.ShapeDtypeStruct((B, S, W), BF16),
        scratch_shapes=[pltpu.VMEM((NH, E, S), BF16), pltpu.VMEM((NH, S, E), BF16),
                        pltpu.VMEM((NH, A, S), F32),
                        pltpu.VMEM((NH, S, E), F32), pltpu.VMEM((NH, E, S), F32), pltpu.VMEM((NH, E, S), F32),
                        pltpu.VMEM((NH, 8, S), F32), pltpu.VMEM((NH, 2, S, E), F32)],
        compiler_params=_params(("parallel", "arbitrary"), V7X_VMEM_LIMIT),
        name="mlstm",
    )(x_m, o_pre, gates, brow, conv_w, conv_b.reshape(1, W), wqt, wk, wvt,
      norm_g.reshape(1, W), skip.reshape(1, W))


def _t5_bucket_static(rel):
    half = REL_BUCKETS // 2
    exact = half // 2
    n = np.abs(rel)
    log_ratio = (np.log(np.maximum(n, 1).astype(np.float32) / np.float32(exact))
                 / np.float32(math.log(REL_MAX_DIST / exact)))
    large = np.minimum(exact + (log_ratio * np.float32(half - exact)).astype(np.int32), half - 1)
    return np.where(rel > 0, half, 0) + np.where(n < exact, n, large)


def _attn_bias_diagonals(rel_bias):
    n = ATTN_DIAG
    nv = 3 * len(DILATIONS)
    x = np.arange(n)
    offset = np.where(x <= ATTN_KBLK, x, x - n)
    rel = offset[None, :] - ATTN_HALF * np.arange(3)[:, None]
    valid = np.tile(np.abs(rel) <= ATTN_HALF, (len(DILATIONS), 1))
    bucket = np.concatenate([_t5_bucket_static(rel * d) for d in DILATIONS], axis=0)
    onehot = (bucket[..., None] == np.arange(REL_BUCKETS)) & valid[..., None]
    w = jnp.einsum('vnb,bh->vhn', jnp.asarray(onehot, F32), rel_bias.astype(F32) * LOG2E,
                   precision=lax.Precision.HIGHEST)
    w = jnp.where(jnp.asarray(valid)[:, None, :], w, NEG_BIG)
    w = w.reshape(nv, ATTN_HEADS // 2, 2, n).transpose(1, 0, 2, 3)
    return w.reshape(ATTN_HEADS // 2, 2 * nv, n)


def _attn_kernel(q_ref, k_ref, v_ref, diag_ref, y_ref, q0_s, q1_s, k_s, v0_s, v1_s, o_s, l_s, bias_s):
    S = q_ref.shape[1]
    QB, KB = ATTN_QBLK, ATTN_KBLK
    lane = lax.broadcasted_iota(I32, (1, V7X_LANES), 1)
    in_head = (lane < ATTN_HEAD_DIM, lane >= ATTN_HEAD_DIM)
    q_s = (q0_s, q1_s)
    v_s = (v0_s, v1_s)

    pair = pl.program_id(1)

    @pl.when(pl.program_id(0) == 0)
    def _():
        for i in range(bias_s.shape[1]):
            rows_i = jnp.broadcast_to(diag_ref[0, i:i + 1, :], (QB, ATTN_DIAG))
            bias_s[pair, i] = pltpu.roll(rows_i, 0, 1, stride=1, stride_axis=0)[:, :KB]

    def strided(start, size, d):
        return pl.ds(start, size) if d == 1 else pl.ds(start, size, stride=d)

    for p, d in enumerate(DILATIONS):
        L = S // d
        for r in range(d):
            src = strided(r, L, d)
            dst = slice(r * L, (r + 1) * L)
            q = q_ref[0, src, :]
            v = v_ref[0, src, :]
            k_s[p, dst, :] = k_ref[0, src, :].astype(BF16)
            for a in range(2):
                q_s[a][p, dst, :] = jnp.where(in_head[a], q, 0.0).astype(BF16)
                v_s[a][p, dst, :] = jnp.where(in_head[a], v, 1.0).astype(BF16)

    def logits(p, d, r, qb):
        L = S // d
        nqb = L // QB
        nk = min(L, KB)
        if nqb == 1 or qb == 0:
            k0, variant = 0, 0
        elif qb == nqb - 1:
            k0, variant = L - nk, 2
        else:
            k0, variant = qb * QB - ATTN_HALF, 1
        qrows = slice(r * L + qb * QB, r * L + (qb + 1) * QB)
        krows = slice(r * L + k0, r * L + k0 + nk)
        both = _dot_nt(jnp.concatenate([q_s[a][p, qrows, :] for a in range(2)], axis=0), k_s[p, krows, :])
        s = [both[a * QB:(a + 1) * QB] + bias_s[pair, p * 6 + variant * 2 + a][:, :nk] for a in range(2)]
        return dict(p=p, krows=krows, out_rows=strided(r + d * qb * QB, QB, d), s=s)

    def softmax(t):
        t["m"] = [jnp.max(s, axis=1, keepdims=True) for s in t["s"]]
        t["e"] = [jnp.exp2(s - m).astype(BF16) for s, m in zip(t.pop("s"), t["m"])]

    def outputs(t):
        p = t["p"]
        acc = [_dot(t["e"][a], v_s[a][p, t["krows"], :]) for a in range(2)]
        num = jnp.where(in_head[0], acc[0], acc[1])
        den = pltpu.roll(jnp.where(in_head[0], acc[1], acc[0]), ATTN_HEAD_DIM, 1)
        o_s[p, t["out_rows"], :] = num / den
        l_s[p, t["out_rows"], :] = jnp.where(in_head[0], t["m"][0], t["m"][1]) + jnp.log2(den)

    units = [(p, d, r, qb) for p, d in enumerate(DILATIONS) for r in range(d) for qb in range(S // d // QB)]
    prev = []
    for i in range(0, len(units), ATTN_GROUP):
        cur = [logits(*u) for u in units[i:i + ATTN_GROUP]]
        for t in prev:
            outputs(t)
        for t in cur:
            softmax(t)
        prev = cur
    for t in prev:
        outputs(t)

    mx = jnp.maximum(jnp.maximum(l_s[0], l_s[1]), l_s[2])
    num = jnp.zeros((S, V7X_LANES), F32)
    den = jnp.zeros((S, V7X_LANES), F32)
    for p in range(len(DILATIONS)):
        w = jnp.exp2(l_s[p] - mx)
        num = num + w * o_s[p]
        den = den + w
    y_ref[0] = (num / den).astype(y_ref.dtype)


def _attn(a_q, a_k, a_v, diag):
    B, S, W = a_q.shape
    P = ATTN_HEADS // 2
    NP = len(DILATIONS)
    pair = lambda b, p: (b, 0, p)
    blk = pl.BlockSpec((1, S, V7X_LANES), pair)
    return pl.pallas_call(
        _attn_kernel,
        grid=(B, P),
        in_specs=[blk, blk, blk, pl.BlockSpec((1,) + diag.shape[1:], lambda b, p: (p, 0, 0))],
        out_specs=blk,
        out_shape=jax.ShapeDtypeStruct((B, S, W), BF16),
        scratch_shapes=[pltpu.VMEM((NP, S, V7X_LANES), BF16)] * 5 + [pltpu.VMEM((NP, S, V7X_LANES), F32)] * 2
                       + [pltpu.VMEM((P, diag.shape[1], ATTN_QBLK, ATTN_KBLK), F32)],
        compiler_params=_params(("arbitrary", "arbitrary"), V7X_VMEM_LIMIT),
        name="attn",
    )(a_q, a_k, a_v, diag)


def _outproj_kernel(ym_ref, ya_ref, x_ref, g1_ref, w1_ref, w2_ref, o_ref):
    mix = _dot(ym_ref[0], w1_ref[...]) + _dot(ya_ref[0], w2_ref[...])
    o_ref[0] = x_ref[0] + g1_ref[0] * mix


def _outproj(y_m, y_a, x, gate1, w_out):
    B, S, D = x.shape
    Wm = y_m.shape[-1]
    tm = OUT_ROW_TILE
    w1 = w_out[:Wm].astype(BF16)
    w2 = w_out[Wm:].astype(BF16)
    row = lambda b, i: (b, i, 0)
    const = lambda b, i: (0, 0)
    return pl.pallas_call(
        _outproj_kernel,
        grid=(B, S // tm),
        in_specs=[pl.BlockSpec((1, tm, Wm), row),
                  pl.BlockSpec((1, tm, y_a.shape[-1]), row),
                  pl.BlockSpec((1, tm, D), row),
                  pl.BlockSpec((1, 1, D), lambda b, i: (b, 0, 0)),
                  pl.BlockSpec(w1.shape, const),
                  pl.BlockSpec(w2.shape, const)],
        out_specs=pl.BlockSpec((1, tm, D), row),
        out_shape=jax.ShapeDtypeStruct((B, S, D), F32),
        compiler_params=_params(("parallel", "arbitrary"), V7X_VMEM_LIMIT),
        name="outproj",
    )(y_m, y_a, x, gate1, w1, w2)


def _select_kernel(x_ref, sc_ref, sh_ref, g_ref, whl_ref, br_ref, tri_ref,
                   h_ref, pos_ref, gate_ref, off_ref, *, cap):
    S = x_ref.shape[1]
    NE = N_EXPERTS
    h = _modulated_norm(x_ref[0], g_ref[...], sc_ref[0], sh_ref[0])
    hi, lo = _split_bf16(h)
    h_ref[0] = hi
    both = _dot(hi, whl_ref[...])
    logits = both[:, :V7X_LANES] + both[:, V7X_LANES:] + _dot(lo, whl_ref[:, :V7X_LANES])
    lt = logits.T[:NE, :] + br_ref[...]
    ex = jnp.exp(lt - jnp.max(lt, axis=0, keepdims=True))
    aff = ex / jnp.sum(ex, axis=0, keepdims=True)
    gate_ref[0] = aff

    bits = pltpu.bitcast(aff, I32)

    def count_ge(cand):
        return jnp.sum((bits >= cand).astype(F32), axis=1, keepdims=True)

    def search(i, v):
        shift = 27 - 3 * i
        best = v
        for c in range(1, 8):
            cand = v | lax.shift_left(jnp.int32(c), shift)
            best = jnp.where(count_ge(cand) >= cap, cand, best)
        return best

    top = jnp.full((NE, 1), 1 << 30, I32)
    thr = lax.fori_loop(0, 10, search, jnp.where(count_ge(top) >= cap, top, 0))
    gt = (bits > thr).astype(F32)
    eq = (bits == thr).astype(F32)
    need = cap - jnp.sum(gt, axis=1, keepdims=True)

    def prefix_count(mask):
        off = jnp.zeros((NE, 1), F32)
        parts, starts = [], []
        for j in range(S // V7X_LANES):
            t = mask[:, j * V7X_LANES:(j + 1) * V7X_LANES]
            starts.append(off)
            parts.append(_dot(t.astype(BF16), tri_ref[...]) + off)
            off = off + jnp.sum(t, axis=1, keepdims=True)
        return jnp.concatenate(parts, axis=1), starts + [off]

    sel = jnp.maximum(gt, jnp.where(prefix_count(eq)[0] < need, eq, 0.0))
    slot, starts = prefix_count(sel)
    pos_ref[0] = jnp.where(sel > 0.0, slot, -1.0).astype(I32)
    step = MOE_TOKEN_TILE // V7X_LANES
    off_ref[0] = jnp.concatenate(starts[::step], axis=1).astype(I32)


def _select(x1, scale, shift, g, w_router, b_router):
    B, S, D = x1.shape
    NE = N_EXPERTS
    cap = (EC_CAPACITY_FACTOR * S) // NE
    nt = S // MOE_TOKEN_TILE
    wpad = jnp.zeros((D, V7X_LANES), F32).at[:, :NE].set(w_router)
    whl = jnp.concatenate(_split_bf16(wpad), axis=1)
    ti = jnp.arange(V7X_LANES)
    tri = (ti[:, None] < ti[None, :]).astype(BF16)
    vec = lambda b: (b, 0, 0)
    const = lambda b: (0, 0)
    return pl.pallas_call(
        functools.partial(_select_kernel, cap=cap),
        grid=(B,),
        in_specs=[pl.BlockSpec((1, S, D), vec),
                  pl.BlockSpec((1, 1, D), vec),
                  pl.BlockSpec((1, 1, D), vec),
                  pl.BlockSpec((1, D), const),
                  pl.BlockSpec((D, 2 * V7X_LANES), const),
                  pl.BlockSpec((NE, 1), const),
                  pl.BlockSpec((V7X_LANES, V7X_LANES), const)],
        out_specs=[pl.BlockSpec((1, S, D), vec),
                   pl.BlockSpec((1, NE, S), vec),
                   pl.BlockSpec((1, NE, S), vec),
                   pl.BlockSpec((1, NE, nt + 1), vec)],
        out_shape=[jax.ShapeDtypeStruct((B, S, D), BF16),
                   jax.ShapeDtypeStruct((B, NE, S), I32),
                   jax.ShapeDtypeStruct((B, NE, S), F32),
                   jax.ShapeDtypeStruct((B, NE, nt + 1), I32)],
        compiler_params=_params(("parallel",), V7X_VMEM_LIMIT),
        name="select",
    )(x1, scale, shift, g, whl, b_router.reshape(NE, 1), tri)


def _window_start(off, cap, rows):
    return pl.multiple_of(jnp.minimum((off // 16) * 16, cap - rows), 16)


def _gather_kernel(off_ref, h_ref, pos_ref, xin_ref):
    NE, _, cap, D = xin_ref.shape
    TT, W = MOE_TOKEN_TILE, MOE_WINDOW
    b, j = pl.program_id(0), pl.program_id(1)
    nt = pl.num_programs(1) * GATHER_STEP_TILES

    @pl.when(j == 0)
    def _():
        xin_ref[...] = jnp.zeros(xin_ref.shape, xin_ref.dtype)

    for t in range(GATHER_STEP_TILES):
        tok = slice(t * TT, (t + 1) * TT)
        base = (b * NE) * (nt + 1) + j * GATHER_STEP_TILES + t
        offs = [off_ref[base + e * (nt + 1)] for e in range(NE)]
        ends = [off_ref[base + e * (nt + 1) + 1] for e in range(NE)]

        def onehot(e, start, tok=tok):
            slot = start + lax.broadcasted_iota(I32, (W, TT), 0)
            return jnp.where(pos_ref[0, e:e + 1, tok] == slot, 1.0, 0.0).astype(BF16)

        def place(e, start, new, offs=offs):
            slot = start + lax.broadcasted_iota(I32, (W, 1), 0)
            old = xin_ref[e, 0, pl.ds(start, W), :].astype(F32)
            xin_ref[e, 0, pl.ds(start, W), :] = jnp.where(slot >= offs[e], new, old).astype(xin_ref.dtype)

        starts = [_window_start(offs[e], cap, W) for e in range(NE)]
        res = _dot(jnp.concatenate([onehot(e, starts[e]) for e in range(NE)], axis=0), h_ref[0, tok, :])
        for e in range(NE):
            place(e, starts[e], res[e * W:(e + 1) * W])

        n_extra = [(jnp.maximum(ends[e] - starts[e] - W, 0) + W - 1) // W for e in range(NE)]

        @pl.when(functools.reduce(jnp.maximum, n_extra) > 0)
        def _(tok=tok, starts=starts, n_extra=n_extra, onehot=onehot, place=place):
            for e in range(NE):
                def extra(c, carry, e=e):
                    start = _window_start(starts[e] + W + c * W, cap, W)
                    place(e, start, _dot(onehot(e, start), h_ref[0, tok, :]))
                    return carry

                lax.fori_loop(0, n_extra[e], extra, 0)


def _gather(h2, pos, offs):
    B, S, D = h2.shape
    NE = N_EXPERTS
    cap = (EC_CAPACITY_FACTOR * S) // NE
    ts = MOE_TOKEN_TILE * GATHER_STEP_TILES
    grid_spec = pltpu.PrefetchScalarGridSpec(
        num_scalar_prefetch=1,
        grid=(B, S // ts),
        in_specs=[pl.BlockSpec((1, ts, D), lambda b, j, o: (b, j, 0)),
                  pl.BlockSpec((1, NE, ts), lambda b, j, o: (b, 0, j))],
        out_specs=pl.BlockSpec((NE, 1, cap, D), lambda b, j, o: (0, b, 0, 0)),
    )
    return pl.pallas_call(
        _gather_kernel,
        grid_spec=grid_spec,
        out_shape=jax.ShapeDtypeStruct((NE, B, cap, D), BF16),
        compiler_params=_params(("parallel", "arbitrary"), V7X_VMEM_LIMIT),
        name="gather",
    )(offs.reshape(-1), h2, pos)


def _expert_kernel(x_ref, wg_ref, wu_ref, wd_ref, o_ref, acc_s, wg_s, wu_s, wd_s, *, row_tile):
    f = pl.program_id(2)
    nf = pl.num_programs(2)
    n_tiles = x_ref.shape[1] // row_tile

    def sweep(first, last):
        wg_s[...] = wg_ref[0].astype(BF16)
        wu_s[...] = wu_ref[0].astype(BF16)
        wd_s[...] = wd_ref[0].astype(BF16)

        def down(r, hid):
            y = _dot(hid, wd_s[...])
            if not first:
                y = y + acc_s[r, :]
            if last:
                o_ref[0, r, :] = y.astype(o_ref.dtype)
            else:
                acc_s[r, :] = y

        pending = None
        for i in range(n_tiles):
            r = slice(i * row_tile, (i + 1) * row_tile)
            xb = x_ref[0, r, :]
            g = _dot(xb, wg_s[...])
            u = _dot(xb, wu_s[...])
            if pending is not None:
                down(*pending)
            pending = (r, (g * _sigmoid(g) * u).astype(BF16))
        down(*pending)

    @pl.when(f == 0)
    def _():
        sweep(True, False)

    @pl.when(jnp.logical_and(f > 0, f < nf - 1))
    def _():
        sweep(False, False)

    @pl.when(f == nf - 1)
    def _():
        sweep(False, True)


def _experts(xin, w_gate, w_up, w_down):
    NE, R, D = xin.shape
    F = w_gate.shape[-1]
    tr = min(R, EXPERT_ROWS)
    tf = EXPERT_FF_TILE
    row_tile = min(tr, EXPERT_ROW_TILE)
    return pl.pallas_call(
        functools.partial(_expert_kernel, row_tile=row_tile),
        grid=(NE, R // tr, F // tf),
        in_specs=[pl.BlockSpec((1, tr, D), lambda e, r, f: (e, r, 0)),
                  pl.BlockSpec((1, D, tf), lambda e, r, f: (e, 0, f)),
                  pl.BlockSpec((1, D, tf), lambda e, r, f: (e, 0, f)),
                  pl.BlockSpec((1, tf, D), lambda e, r, f: (e, f, 0))],
        out_specs=pl.BlockSpec((1, tr, D), lambda e, r, f: (e, r, 0)),
        out_shape=jax.ShapeDtypeStruct((NE, R, D), BF16),
        scratch_shapes=[pltpu.VMEM((tr, D), F32), pltpu.VMEM((D, tf), BF16),
                        pltpu.VMEM((D, tf), BF16), pltpu.VMEM((tf, D), BF16)],
        compiler_params=_params(("parallel", "parallel", "arbitrary"), V7X_VMEM_LIMIT),
        name="experts",
    )(xin, w_gate, w_up, w_down)


def _combine_kernel(off_ref, y_ref, pos_ref, gate_ref, x_ref, g2_ref, o_ref):
    NE, _, cap, D = y_ref.shape
    TT, W = MOE_TOKEN_TILE, MOE_WINDOW
    b, j = pl.program_id(0), pl.program_id(1)
    nt = pl.num_programs(1) * MOE_STEP_TILES

    for t in range(MOE_STEP_TILES):
        tok = slice(t * TT, (t + 1) * TT)
        base = (b * NE) * (nt + 1) + j * MOE_STEP_TILES + t
        offs = [off_ref[base + e * (nt + 1)] for e in range(NE)]
        ends = [off_ref[base + e * (nt + 1) + 1] for e in range(NE)]
        starts = [_window_start(offs[e], cap, W) for e in range(NE)]

        def weighted_onehot(e, start, lo, tok=tok):
            slot = start + lax.broadcasted_iota(I32, (W, TT), 0)
            hit = (pos_ref[0, e:e + 1, tok] == slot) & (slot >= lo)
            return jnp.where(hit, gate_ref[0, e:e + 1, tok], 0.0).astype(BF16)

        scatter = jnp.concatenate([weighted_onehot(e, starts[e], 0) for e in range(NE)], axis=0)
        ystack = jnp.concatenate([y_ref[e, 0, pl.ds(starts[e], W), :] for e in range(NE)], axis=0)
        o_ref[0, tok, :] = x_ref[0, tok, :] + g2_ref[0] * _dot_tn(scatter, ystack)

        n_extra = [(jnp.maximum(ends[e] - starts[e] - W, 0) + W - 1) // W for e in range(NE)]

        @pl.when(functools.reduce(jnp.maximum, n_extra) > 0)
        def _(tok=tok, starts=starts, n_extra=n_extra, weighted_onehot=weighted_onehot):
            for e in range(NE):
                def extra(c, carry, e=e):
                    lo = starts[e] + W + c * W
                    start = _window_start(lo, cap, W)
                    part = _dot_tn(weighted_onehot(e, start, lo), y_ref[e, 0, pl.ds(start, W), :])
                    o_ref[0, tok, :] = o_ref[0, tok, :] + g2_ref[0] * part
                    return carry

                lax.fori_loop(0, n_extra[e], extra, 0)


def _combine(y, pos, gate, offs, x1, gate2):
    B, S, D = x1.shape
    NE, _, cap, _ = y.shape
    ts = MOE_TOKEN_TILE * MOE_STEP_TILES
    row = lambda b, j, o: (b, j, 0)
    grid_spec = pltpu.PrefetchScalarGridSpec(
        num_scalar_prefetch=1,
        grid=(B, S // ts),
        in_specs=[pl.BlockSpec((NE, 1, cap, D), lambda b, j, o: (0, b, 0, 0)),
                  pl.BlockSpec((1, NE, ts), lambda b, j, o: (b, 0, j)),
                  pl.BlockSpec((1, NE, ts), lambda b, j, o: (b, 0, j)),
                  pl.BlockSpec((1, ts, D), row),
                  pl.BlockSpec((1, 1, D), lambda b, j, o: (b, 0, 0))],
        out_specs=pl.BlockSpec((1, ts, D), row),
    )
    return pl.pallas_call(
        _combine_kernel,
        grid_spec=grid_spec,
        out_shape=jax.ShapeDtypeStruct((B, S, D), F32),
        compiler_params=_params(("parallel", "arbitrary"), V7X_VMEM_LIMIT),
        name="combine",
    )(offs.reshape(-1), y, pos, gate, x1, gate2)


def _inproj_weight(w_in):
    D = w_in.shape[0]
    H = MLSTM_HEADS
    g0 = 2 * MLSTM_WIDTH
    gates = w_in[:, g0:g0 + N_GATES].reshape(D, 4, H).transpose(0, 2, 1).reshape(D, N_GATES)
    main = jnp.concatenate([w_in[:, :g0], w_in[:, g0 + N_GATES:], gates,
                            jnp.zeros((D, V7X_LANES - N_GATES), w_in.dtype)], axis=1)
    return main.astype(BF16)


def kernel(x, c, w_ada, b_ada, norm1_g, w_in, conv_w, conv_b, w_q_blk, w_k_blk, w_v_blk, b_igate, b_fgate,
           mlstm_norm_g, mlstm_skip, q_norm_g, k_norm_g, rel_bias, w_out, norm2_g, w_router, b_router,
           w_gate, w_up, w_down):
    B, S, D = x.shape
    depth = w_ada.shape[0]
    diag = _attn_bias_diagonals(rel_bias)
    for l in range(depth):
        mod = _ada(c, w_ada[l], b_ada[l])
        shift1, scale1, gate1, shift2, scale2, gate2 = (
            mod[:, i * D:(i + 1) * D].reshape(B, 1, D) for i in range(N_MOD))

        x_m, o_pre, a_q, a_k, a_v, gates = _inproj(x, scale1, shift1, norm1_g[l].reshape(1, D),
                                                   _inproj_weight(w_in[l]), q_norm_g[l], k_norm_g[l])
        y_m = _mlstm(x_m, o_pre, gates, conv_w[l], conv_b[l], w_q_blk[l], w_k_blk[l], w_v_blk[l],
                     b_igate[l], b_fgate[l], mlstm_norm_g[l], mlstm_skip[l])
        y_a = _attn(a_q, a_k, a_v, diag)
        x1 = _outproj(y_m, y_a, x, gate1, w_out[l])

        h2, pos, aff, offs = _select(x1, scale2, shift2, norm2_g[l].reshape(1, D), w_router[l], b_router[l])
        xin = _gather(h2, pos, offs)
        NE, _, cap, _ = xin.shape
        y = _experts(xin.reshape(NE, B * cap, D), w_gate[l], w_up[l], w_down[l]).reshape(NE, B, cap, D)
        x = _combine(y, pos, aff, offs, x1, gate2)
    return x
```

```python
import functools
import math

import numpy as np
import jax
import jax.numpy as jnp
from jax import lax
from jax.experimental import pallas as pl
from jax.experimental.pallas import tpu as pltpu

F32 = jnp.float32
BF16 = jnp.bfloat16
I32 = jnp.int32

NORM_EPS = 1e-6
N_MOD = 6
ADA_COL_TILE = 1024
ROW_TILE = 1024
OUT_ROW_TILE = 2048
MLSTM_HEADS = 4
MLSTM_HEAD_DIM = 128
MLSTM_WIDTH = MLSTM_HEADS * MLSTM_HEAD_DIM
N_GATES = 4 * MLSTM_HEADS
MLSTM_QKV_BLOCK = 4
MLSTM_CONV = 5
MLSTM_CHUNK = 128
MLSTM_STEP_HEADS = 4
MLSTM_ONES_ROWS = 16
ATTN_HEADS = 8
ATTN_HEAD_DIM = 64
ATTN_WIDTH = ATTN_HEADS * ATTN_HEAD_DIM
DILATIONS = (1, 4, 16)
ATTN_HALF = 64
ATTN_QBLK = 128
ATTN_KBLK = 256
ATTN_DIAG = 512
ATTN_GROUP = 2
LOG2E = math.log2(math.e)
REL_BUCKETS = 32
REL_MAX_DIST = 1024
N_EXPERTS = 16
EC_CAPACITY_FACTOR = 2
EXPERT_ROWS = 2048
EXPERT_ROW_TILE = 512
EXPERT_FF_TILE = 512
MOE_TOKEN_TILE = 256
MOE_STEP_TILES = 4
GATHER_STEP_TILES = 4
MOE_WINDOW = 64
NEG_BIG = -1e30

V7X_LANES = 128
V7X_VMEM_LIMIT = 56 * 1024 * 1024


def _sigmoid(x):
    return 1.0 / (1.0 + jnp.exp(-x))


def _dot(a, b):
    return jnp.dot(a, b, preferred_element_type=F32)


def _dot_nt(a, b):
    return lax.dot_general(a, b, (((1,), (1,)), ((), ())), preferred_element_type=F32)


def _dot_tn(a, b):
    return lax.dot_general(a, b, (((0,), (0,)), ((), ())), preferred_element_type=F32)


def _split_bf16(x):
    hi = x.astype(BF16)
    lo = (x - hi.astype(F32)).astype(BF16)
    return hi, lo


def _params(sem, vmem=None):
    return pltpu.CompilerParams(dimension_semantics=sem, vmem_limit_bytes=vmem)


def _ada_kernel(c_ref, w_ref, b_ref, o_ref):
    c = c_ref[...]
    s = c * _sigmoid(c)
    o_ref[...] = jnp.dot(s, w_ref[...], preferred_element_type=F32,
                         precision=lax.Precision.HIGHEST) + b_ref[...]


def _ada(c, w, b):
    B, D = c.shape
    N = w.shape[1]
    tn = ADA_COL_TILE
    return pl.pallas_call(
        _ada_kernel,
        grid=(N // tn,),
        in_specs=[pl.BlockSpec((B, D), lambda j: (0, 0)),
                  pl.BlockSpec((D, tn), lambda j: (0, j)),
                  pl.BlockSpec((1, tn), lambda j: (0, j))],
        out_specs=pl.BlockSpec((B, tn), lambda j: (0, j)),
        out_shape=jax.ShapeDtypeStruct((B, N), F32),
        compiler_params=_params(("arbitrary",)),
        name="ada",
    )(c, w, b.reshape(1, N))


def _modulated_norm(x, g, scale, shift):
    ms = jnp.mean(x * x, axis=-1, keepdims=True)
    return x * lax.rsqrt(ms + NORM_EPS) * (g * (1.0 + scale)) + shift


def _inproj_kernel(x_ref, sc_ref, sh_ref, g_ref, w_ref, hm_ref, qg_ref, kg_ref,
                   xm_ref, op_ref, q_ref, k_ref, v_ref, gt_ref):
    h = _modulated_norm(x_ref[0], g_ref[...], sc_ref[0], sh_ref[0]).astype(BF16)
    cw = MLSTM_WIDTH
    col = lambda i: _dot(h, w_ref[:, cw * i:cw * (i + 1)])
    q = col(2)
    k = col(3)
    xm_ref[0] = col(0)
    op_ref[0] = col(1)
    v_ref[0] = col(4)
    gt_ref[0] = _dot(h, w_ref[:, 5 * cw:5 * cw + V7X_LANES]).T[:N_GATES, :]

    def head_norm(t, g):
        ms = _dot((t * t).astype(BF16), hm_ref[...])
        return t * lax.rsqrt(ms + NORM_EPS) * g

    q_ref[0] = head_norm(q, qg_ref[...])
    k_ref[0] = head_norm(k, kg_ref[...])


def _inproj(x, scale, shift, g, w, q_norm_g, k_norm_g):
    B, S, D = x.shape
    tm = ROW_TILE
    W = ATTN_WIDTH
    assert W == MLSTM_WIDTH
    hid = jnp.arange(W) // ATTN_HEAD_DIM
    head_mean = jnp.where(hid[:, None] == hid[None, :], 1.0 / ATTN_HEAD_DIM, 0.0).astype(BF16)
    qg = jnp.tile(q_norm_g, ATTN_HEADS).reshape(1, W) * (LOG2E / math.sqrt(ATTN_HEAD_DIM))
    kg = jnp.tile(k_norm_g, ATTN_HEADS).reshape(1, W)
    row = lambda b, i: (b, i, 0)
    vec = lambda b, i: (b, 0, 0)
    const = lambda b, i: (0, 0)
    outs = ([jax.ShapeDtypeStruct((B, S, MLSTM_WIDTH), F32)] * 2 + [jax.ShapeDtypeStruct((B, S, W), F32)] * 3
            + [jax.ShapeDtypeStruct((B, N_GATES, S), F32)])
    return pl.pallas_call(
        _inproj_kernel,
        grid=(B, S // tm),
        in_specs=[pl.BlockSpec((1, tm, D), row),
                  pl.BlockSpec((1, 1, D), vec),
                  pl.BlockSpec((1, 1, D), vec),
                  pl.BlockSpec((1, D), const),
                  pl.BlockSpec(w.shape, const),
                  pl.BlockSpec((W, W), const),
                  pl.BlockSpec((1, W), const),
                  pl.BlockSpec((1, W), const)],
        out_specs=[pl.BlockSpec((1, tm, W), row)] * 5 + [pl.BlockSpec((1, N_GATES, tm), lambda b, i: (b, 0, i))],
        out_shape=outs,
        compiler_params=_params(("parallel", "arbitrary"), V7X_VMEM_LIMIT),
        name="inproj",
    )(x, scale, shift, g, w, head_mean, qg, kg)


def _chunk_scan(x, op, reverse):
    n = x.shape[1]
    idx = lax.broadcasted_iota(I32, x.shape, 1) & (MLSTM_CHUNK - 1)
    s = 1
    while s < MLSTM_CHUNK:
        if reverse:
            r = pltpu.roll(x, n - s, 1)
            x = jnp.where(idx < MLSTM_CHUNK - s, op(x, r), x)
        else:
            r = pltpu.roll(x, s, 1)
            x = jnp.where(idx >= s, op(x, r), x)
        s *= 2
    return x


def _log_sigmoid(x):
    return jnp.minimum(x, 0.0) - jnp.log(1.0 + jnp.exp(-jnp.abs(x)))


def _mlstm_kernel(xm_ref, op_ref, grow_ref, brow_ref, cw_ref, cb_ref,
                  wqt_ref, wk_ref, wvt_ref, ng_ref, sk_ref, y_ref,
                  qt_s, k_s, vat_s, xc_s, hf_s, hb_s, row_s, col_s):
    S = xm_ref.shape[1]
    L = MLSTM_CHUNK
    NC = S // L
    E = MLSTM_HEAD_DIM
    A = E + MLSTM_ONES_ROWS
    NH = MLSTM_STEP_HEADS
    half = MLSTM_CONV // 2
    rows = lax.broadcasted_iota(I32, (S, E), 0)
    kj = lax.broadcasted_iota(I32, (L, L), 0)
    qi = lax.broadcasted_iota(I32, (L, L), 1)
    causal = (kj <= qi, kj >= qi)

    def projections(hh):
        lanes = slice(hh * E, (hh + 1) * E)
        xm = xm_ref[0, :, lanes]
        conv = xm * cw_ref[half:half + 1, lanes]
        for j in range(MLSTM_CONV):
            off = j - half
            if off == 0:
                continue
            shifted = pltpu.roll(xm, (-off) % S, 0)
            valid = rows >= -off if off < 0 else rows < S - off
            conv = conv + jnp.where(valid, shifted, 0.0) * cw_ref[j:j + 1, lanes]
        conv = conv + cb_ref[:, lanes]
        xc = conv * _sigmoid(conv)
        xc_s[hh] = xc
        xcb = xc.astype(BF16)
        qt_s[hh] = _dot_nt(wqt_ref[hh], xcb).astype(BF16)
        k_s[hh] = (_dot(xcb, wk_ref[hh]) * (1.0 / math.sqrt(E))).astype(BF16)
        vat_s[hh, :E, :] = _dot_nt(wvt_ref[hh], xm.astype(BF16))
        vat_s[hh, E:, :] = jnp.ones((MLSTM_ONES_ROWS, S), F32)

    def gate_stats(hh):
        gr = grow_ref[0, 4 * hh:4 * hh + 4, :] + brow_ref[hh]
        kind = lax.broadcasted_iota(I32, gr.shape, 0)
        gr = jnp.where((kind & 1) == 1, _log_sigmoid(gr), gr)
        b_f = _chunk_scan(gr[1:2], jnp.add, False)
        b_b = _chunk_scan(gr[3:4], jnp.add, True)
        u_f = gr[0:1] - b_f
        u_b = gr[2:3] - b_b
        m_f = _chunk_scan(u_f, jnp.maximum, False)
        m_b = _chunk_scan(u_b, jnp.maximum, True)
        row_s[hh] = jnp.concatenate([b_f, m_f, u_f, gr[1:2], b_b, m_b, u_b, gr[3:4]], axis=0)
        for k, stat in enumerate((u_f, u_b)):
            for c in range(NC):
                col_s[hh, k, c * L:(c + 1) * L, :] = jnp.broadcast_to(stat[:, c * L:(c + 1) * L], (L, L)).T

    def local_part(hh, c, dirn):
        r0 = c * L
        qt = qt_s[hh, :, pl.ds(r0, L)]
        kc = k_s[hh, pl.ds(r0, L), :]
        vat = vat_s[hh, :, pl.ds(r0, L)]
        rr = row_s[hh, :, pl.ds(r0, L)]
        b_r, m_r, u_r = (rr[4 * dirn + i:4 * dirn + i + 1, :] for i in range(3))
        g = jnp.sum(rr[4 * dirn + 3:4 * dirn + 4, :], axis=1, keepdims=True)
        m_loc = g + jnp.max(u_r, axis=1, keepdims=True)
        return dict(
            hh=hh, r0=r0, dirn=dirn, qt=qt, vat=vat.astype(BF16), b_r=b_r, m_r=m_r, g=g, m_loc=m_loc,
            u_c=col_s[hh, dirn, pl.ds(r0, L), :],
            s=_dot(kc, qt),
            d_state=_dot((vat * jnp.exp(g + u_r - m_loc)).astype(BF16), kc))

    def intra_part(t):
        p = jnp.exp(jnp.where(causal[t["dirn"]], t["u_c"] - t["m_r"], NEG_BIG)) * t.pop("s")
        t["y_loc"] = _dot(t["vat"], p.astype(BF16))

    def state_part(t, state):
        Ct, m = state
        x_st = _dot(Ct.astype(BF16), t["qt"])
        mm = jnp.maximum(m, t["m_r"])
        z = jnp.exp(m - mm) * x_st + jnp.exp(t["m_r"] - mm) * t["y_loc"]
        h = z[:E, :] / jnp.maximum(jnp.abs(z[E:E + 1, :]), jnp.exp(-t["b_r"] - mm))
        m_new = jnp.maximum(t["g"] + m, t["m_loc"])
        Ct_new = jnp.exp(t["g"] + m - m_new) * Ct + jnp.exp(t["m_loc"] - m_new) * t["d_state"]
        return h, (Ct_new, m_new)

    for hh in range(NH):
        projections(hh)
    for hh in range(NH):
        gate_stats(hh)
    work = [local_part(hh, c if dirn == 0 else NC - 1 - c, dirn)
            for c in range(NC) for hh in range(NH) for dirn in range(2)]
    for t in work:
        intra_part(t)
    out_s = (hf_s, hb_s)
    states = {(hh, dirn): (jnp.zeros((A, E), F32), jnp.zeros((1, 1), F32)) for hh in range(NH) for dirn in range(2)}
    for t in work:
        key = (t["hh"], t["dirn"])
        h, states[key] = state_part(t, states[key])
        out_s[t["dirn"]][t["hh"], :, pl.ds(t["r0"], L)] = h

    for hh in range(NH):
        lanes = slice(hh * E, (hh + 1) * E)
        h = (hf_s[hh] + hb_s[hh]).T
        hn = h * lax.rsqrt(jnp.mean(h * h, axis=-1, keepdims=True) + NORM_EPS) * ng_ref[:, lanes]
        y = (hn + sk_ref[:, lanes] * xc_s[hh]) * _sigmoid(op_ref[0, :, lanes])
        y_ref[0, :, lanes] = y.astype(y_ref.dtype)


def _blockdiag_dense(w_blk, transposed=False):
    E, Q = MLSTM_HEAD_DIM, MLSTM_QKV_BLOCK
    rows = w_blk.reshape(MLSTM_HEADS, E, Q)
    idx = np.arange(E)
    spread = jnp.asarray(idx[None, :] % Q == np.arange(Q)[:, None], w_blk.dtype)
    same_block = jnp.asarray(idx[:, None] // Q == idx[None, :] // Q, w_blk.dtype)
    out = 'hcr' if transposed else 'hrc'
    return jnp.einsum(f'hrj,jc->{out}', rows, spread, precision=lax.Precision.HIGHEST) * same_block


def _mlstm(x_m, o_pre, gates, conv_w, conv_b, w_q_blk, w_k_blk, w_v_blk, b_igate, b_fgate, norm_g, skip):
    B, S, W = x_m.shape
    H, E, NH = MLSTM_HEADS, MLSTM_HEAD_DIM, MLSTM_STEP_HEADS
    A = E + MLSTM_ONES_ROWS
    assert S % MLSTM_CHUNK == 0 and H % NH == 0
    bk = jnp.stack([b_igate[0], b_fgate[0], b_igate[1], b_fgate[1]], axis=0)
    brow = bk.T.reshape(H, 4, 1)
    wqt = _blockdiag_dense(w_q_blk, transposed=True).astype(BF16)
    wk = _blockdiag_dense(w_k_blk).astype(BF16)
    wvt = _blockdiag_dense(w_v_blk, transposed=True).astype(BF16)
    headcol = lambda b, h: (b, 0, h)
    perhead = lambda b, h: (h, 0, 0)
    lanes = lambda b, h: (0, h)
    return pl.pallas_call(
        _mlstm_kernel,
        grid=(B, H // NH),
        in_specs=[pl.BlockSpec((1, S, NH * E), headcol),
                  pl.BlockSpec((1, S, NH * E), headcol),
                  pl.BlockSpec((1, 4 * NH, S), lambda b, h: (b, h, 0)),
                  pl.BlockSpec((NH, 4, 1), perhead),
                  pl.BlockSpec((MLSTM_CONV, NH * E), lanes),
                  pl.BlockSpec((1, NH * E), lanes),
                  pl.BlockSpec((NH, E, E), perhead),
                  pl.BlockSpec((NH, E, E), perhead),
                  pl.BlockSpec((NH, E, E), perhead),
                  pl.BlockSpec((1, NH * E), lanes),
                  pl.BlockSpec((1, NH * E), lanes)],
        out_specs=pl.BlockSpec((1, S, NH * E), headcol),
        out_shape=jax.ShapeDtypeStruct((B, S, W), BF16),
        scratch_shapes=[pltpu.VMEM((NH, E, S), BF16), pltpu.VMEM((NH, S, E), BF16),
                        pltpu.VMEM((NH, A, S), F32),
                        pltpu.VMEM((NH, S, E), F32), pltpu.VMEM((NH, E, S), F32), pltpu.VMEM((NH, E, S), F32),
                        pltpu.VMEM((NH, 8, S), F32), pltpu.VMEM((NH, 2, S, E), F32)],
        compiler_params=_params(("parallel", "arbitrary"), V7X_VMEM_LIMIT),
        name="mlstm",
    )(x_m, o_pre, gates, brow, conv_w, conv_b.reshape(1, W), wqt, wk, wvt,
      norm_g.reshape(1, W), skip.reshape(1, W))


def _t5_bucket_static(rel):
    half = REL_BUCKETS // 2
    exact = half // 2
    n = np.abs(rel)
    log_ratio = (np.log(np.maximum(n, 1).astype(np.float32) / np.float32(exact))
                 / np.float32(math.log(REL_MAX_DIST / exact)))
    large = np.minimum(exact + (log_ratio * np.float32(half - exact)).astype(np.int32), half - 1)
    return np.where(rel > 0, half, 0) + np.where(n < exact, n, large)


def _attn_bias_diagonals(rel_bias):
    n = ATTN_DIAG
    nv = 3 * len(DILATIONS)
    x = np.arange(n)
    offset = np.where(x <= ATTN_KBLK, x, x - n)
    rel = offset[None, :] - ATTN_HALF * np.arange(3)[:, None]
    valid = np.tile(np.abs(rel) <= ATTN_HALF, (len(DILATIONS), 1))
    bucket = np.concatenate([_t5_bucket_static(rel * d) for d in DILATIONS], axis=0)
    onehot = (bucket[..., None] == np.arange(REL_BUCKETS)) & valid[..., None]
    w = jnp.einsum('vnb,bh->vhn', jnp.asarray(onehot, F32), rel_bias.astype(F32) * LOG2E,
                   precision=lax.Precision.HIGHEST)
    w = jnp.where(jnp.asarray(valid)[:, None, :], w, NEG_BIG)
    w = w.reshape(nv, ATTN_HEADS // 2, 2, n).transpose(1, 0, 2, 3)
    return w.reshape(ATTN_HEADS // 2, 2 * nv, n)


def _attn_kernel(q_ref, k_ref, v_ref, diag_ref, y_ref, q0_s, q1_s, k_s, v0_s, v1_s, o_s, l_s, bias_s):
    S = q_ref.shape[1]
    QB, KB = ATTN_QBLK, ATTN_KBLK
    lane = lax.broadcasted_iota(I32, (1, V7X_LANES), 1)
    in_head = (lane < ATTN_HEAD_DIM, lane >= ATTN_HEAD_DIM)
    q_s = (q0_s, q1_s)
    v_s = (v0_s, v1_s)

    pair = pl.program_id(1)

    @pl.when(pl.program_id(0) == 0)
    def _():
        for i in range(bias_s.shape[1]):
            rows_i = jnp.broadcast_to(diag_ref[0, i:i + 1, :], (QB, ATTN_DIAG))
            bias_s[pair, i] = pltpu.roll(rows_i, 0, 1, stride=1, stride_axis=0)[:, :KB]

    def strided(start, size, d):
        return pl.ds(start, size) if d == 1 else pl.ds(start, size, stride=d)

    for p, d in enumerate(DILATIONS):
        L = S // d
        for r in range(d):
            src = strided(r, L, d)
            dst = slice(r * L, (r + 1) * L)
            q = q_ref[0, src, :]
            v = v_ref[0, src, :]
            k_s[p, dst, :] = k_ref[0, src, :].astype(BF16)
            for a in range(2):
                q_s[a][p, dst, :] = jnp.where(in_head[a], q, 0.0).astype(BF16)
                v_s[a][p, dst, :] = jnp.where(in_head[a], v, 1.0).astype(BF16)

    def logits(p, d, r, qb):
        L = S // d
        nqb = L // QB
        nk = min(L, KB)
        if nqb == 1 or qb == 0:
            k0, variant = 0, 0
        elif qb == nqb - 1:
            k0, variant = L - nk, 2
        else:
            k0, variant = qb * QB - ATTN_HALF, 1
        qrows = slice(r * L + qb * QB, r * L + (qb + 1) * QB)
        krows = slice(r * L + k0, r * L + k0 + nk)
        kt = k_s[p, krows, :]
        s = [_dot_nt(q_s[a][p, qrows, :], kt) + bias_s[pair, p * 6 + variant * 2 + a][:, :nk] for a in range(2)]
        return dict(p=p, krows=krows, out_rows=strided(r + d * qb * QB, QB, d), s=s)

    def softmax(t):
        t["m"] = [jnp.max(s, axis=1, keepdims=True) for s in t["s"]]
        t["e"] = [jnp.exp2(s - m).astype(BF16) for s, m in zip(t.pop("s"), t["m"])]

    def outputs(t):
        p = t["p"]
        acc = [_dot(t["e"][a], v_s[a][p, t["krows"], :]) for a in range(2)]
        num = jnp.where(in_head[0], acc[0], acc[1])
        den = pltpu.roll(jnp.where(in_head[0], acc[1], acc[0]), ATTN_HEAD_DIM, 1)
        o_s[p, t["out_rows"], :] = num / den
        l_s[p, t["out_rows"], :] = jnp.where(in_head[0], t["m"][0], t["m"][1]) + jnp.log2(den)

    units = [(p, d, r, qb) for p, d in enumerate(DILATIONS) for r in range(d) for qb in range(S // d // QB)]
    prev = []
    for i in range(0, len(units), ATTN_GROUP):
        cur = [logits(*u) for u in units[i:i + ATTN_GROUP]]
        for t in prev:
            outputs(t)
        for t in cur:
            softmax(t)
        prev = cur
    for t in prev:
        outputs(t)

    mx = jnp.maximum(jnp.maximum(l_s[0], l_s[1]), l_s[2])
    num = jnp.zeros((S, V7X_LANES), F32)
    den = jnp.zeros((S, V7X_LANES), F32)
    for p in range(len(DILATIONS)):
        w = jnp.exp2(l_s[p] - mx)
        num = num + w * o_s[p]
        den = den + w
    y_ref[0] = (num / den).astype(y_ref.dtype)


def _attn(a_q, a_k, a_v, diag):
    B, S, W = a_q.shape
    P = ATTN_HEADS // 2
    NP = len(DILATIONS)
    pair = lambda b, p: (b, 0, p)
    blk = pl.BlockSpec((1, S, V7X_LANES), pair)
    return pl.pallas_call(
        _attn_kernel,
        grid=(B, P),
        in_specs=[blk, blk, blk, pl.BlockSpec((1,) + diag.shape[1:], lambda b, p: (p, 0, 0))],
        out_specs=blk,
        out_shape=jax.ShapeDtypeStruct((B, S, W), BF16),
        scratch_shapes=[pltpu.VMEM((NP, S, V7X_LANES), BF16)] * 5 + [pltpu.VMEM((NP, S, V7X_LANES), F32)] * 2
                       + [pltpu.VMEM((P, diag.shape[1], ATTN_QBLK, ATTN_KBLK), F32)],
        compiler_params=_params(("arbitrary", "arbitrary"), V7X_VMEM_LIMIT),
        name="attn",
    )(a_q, a_k, a_v, diag)


def _outproj_kernel(ym_ref, ya_ref, x_ref, g1_ref, w1_ref, w2_ref, o_ref):
    mix = _dot(ym_ref[0], w1_ref[...]) + _dot(ya_ref[0], w2_ref[...])
    o_ref[0] = x_ref[0] + g1_ref[0] * mix


def _outproj(y_m, y_a, x, gate1, w_out):
    B, S, D = x.shape
    Wm = y_m.shape[-1]
    tm = OUT_ROW_TILE
    w1 = w_out[:Wm].astype(BF16)
    w2 = w_out[Wm:].astype(BF16)
    row = lambda b, i: (b, i, 0)
    const = lambda b, i: (0, 0)
    return pl.pallas_call(
        _outproj_kernel,
        grid=(B, S // tm),
        in_specs=[pl.BlockSpec((1, tm, Wm), row),
                  pl.BlockSpec((1, tm, y_a.shape[-1]), row),
                  pl.BlockSpec((1, tm, D), row),
                  pl.BlockSpec((1, 1, D), lambda b, i: (b, 0, 0)),
                  pl.BlockSpec(w1.shape, const),
                  pl.BlockSpec(w2.shape, const)],
        out_specs=pl.BlockSpec((1, tm, D), row),
        out_shape=jax.ShapeDtypeStruct((B, S, D), F32),
        compiler_params=_params(("parallel", "arbitrary"), V7X_VMEM_LIMIT),
        name="outproj",
    )(y_m, y_a, x, gate1, w1, w2)


def _select_kernel(x_ref, sc_ref, sh_ref, g_ref, whl_ref, br_ref, tri_ref,
                   h_ref, pos_ref, gate_ref, off_ref, *, cap):
    S = x_ref.shape[1]
    NE = N_EXPERTS
    h = _modulated_norm(x_ref[0], g_ref[...], sc_ref[0], sh_ref[0])
    hi, lo = _split_bf16(h)
    h_ref[0] = hi
    both = _dot(hi, whl_ref[...])
    logits = both[:, :V7X_LANES] + both[:, V7X_LANES:] + _dot(lo, whl_ref[:, :V7X_LANES])
    lt = logits.T[:NE, :] + br_ref[...]
    ex = jnp.exp(lt - jnp.max(lt, axis=0, keepdims=True))
    aff = ex / jnp.sum(ex, axis=0, keepdims=True)
    gate_ref[0] = aff

    bits = pltpu.bitcast(aff, I32)

    def count_ge(cand):
        return jnp.sum((bits >= cand).astype(F32), axis=1, keepdims=True)

    def search(i, v):
        shift = 27 - 3 * i
        best = v
        for c in range(1, 8):
            cand = v | lax.shift_left(jnp.int32(c), shift)
            best = jnp.where(count_ge(cand) >= cap, cand, best)
        return best

    top = jnp.full((NE, 1), 1 << 30, I32)
    thr = lax.fori_loop(0, 10, search, jnp.where(count_ge(top) >= cap, top, 0))
    gt = (bits > thr).astype(F32)
    eq = (bits == thr).astype(F32)
    need = cap - jnp.sum(gt, axis=1, keepdims=True)

    def prefix_count(mask):
        off = jnp.zeros((NE, 1), F32)
        parts, starts = [], []
        for j in range(S // V7X_LANES):
            t = mask[:, j * V7X_LANES:(j + 1) * V7X_LANES]
            starts.append(off)
            parts.append(_dot(t.astype(BF16), tri_ref[...]) + off)
            off = off + jnp.sum(t, axis=1, keepdims=True)
        return jnp.concatenate(parts, axis=1), starts + [off]

    sel = jnp.maximum(gt, jnp.where(prefix_count(eq)[0] < need, eq, 0.0))
    slot, starts = prefix_count(sel)
    pos_ref[0] = jnp.where(sel > 0.0, slot, -1.0).astype(I32)
    step = MOE_TOKEN_TILE // V7X_LANES
    off_ref[0] = jnp.concatenate(starts[::step], axis=1).astype(I32)


def _select(x1, scale, shift, g, w_router, b_router):
    B, S, D = x1.shape
    NE = N_EXPERTS
    cap = (EC_CAPACITY_FACTOR * S) // NE
    nt = S // MOE_TOKEN_TILE
    wpad = jnp.zeros((D, V7X_LANES), F32).at[:, :NE].set(w_router)
    whl = jnp.concatenate(_split_bf16(wpad), axis=1)
    ti = jnp.arange(V7X_LANES)
    tri = (ti[:, None] < ti[None, :]).astype(BF16)
    vec = lambda b: (b, 0, 0)
    const = lambda b: (0, 0)
    return pl.pallas_call(
        functools.partial(_select_kernel, cap=cap),
        grid=(B,),
        in_specs=[pl.BlockSpec((1, S, D), vec),
                  pl.BlockSpec((1, 1, D), vec),
                  pl.BlockSpec((1, 1, D), vec),
                  pl.BlockSpec((1, D), const),
                  pl.BlockSpec((D, 2 * V7X_LANES), const),
                  pl.BlockSpec((NE, 1), const),
                  pl.BlockSpec((V7X_LANES, V7X_LANES), const)],
        out_specs=[pl.BlockSpec((1, S, D), vec),
                   pl.BlockSpec((1, NE, S), vec),
                   pl.BlockSpec((1, NE, S), vec),
                   pl.BlockSpec((1, NE, nt + 1), vec)],
        out_shape=[jax.ShapeDtypeStruct((B, S, D), BF16),
                   jax.ShapeDtypeStruct((B, NE, S), I32),
                   jax.ShapeDtypeStruct((B, NE, S), F32),
                   jax.ShapeDtypeStruct((B, NE, nt + 1), I32)],
        compiler_params=_params(("parallel",), V7X_VMEM_LIMIT),
        name="select",
    )(x1, scale, shift, g, whl, b_router.reshape(NE, 1), tri)


def _window_start(off, cap, rows):
    return pl.multiple_of(jnp.minimum((off // 16) * 16, cap - rows), 16)


def _gather_kernel(off_ref, h_ref, pos_ref, xin_ref):
    NE, _, cap, D = xin_ref.shape
    TT, W = MOE_TOKEN_TILE, MOE_WINDOW
    b, j = pl.program_id(0), pl.program_id(1)
    nt = pl.num_programs(1) * GATHER_STEP_TILES

    @pl.when(j == 0)
    def _():
        xin_ref[...] = jnp.zeros(xin_ref.shape, xin_ref.dtype)

    for t in range(GATHER_STEP_TILES):
        tok = slice(t * TT, (t + 1) * TT)
        base = (b * NE) * (nt + 1) + j * GATHER_STEP_TILES + t
        offs = [off_ref[base + e * (nt + 1)] for e in range(NE)]
        ends = [off_ref[base + e * (nt + 1) + 1] for e in range(NE)]

        def onehot(e, start, tok=tok):
            slot = start + lax.broadcasted_iota(I32, (W, TT), 0)
            return jnp.where(pos_ref[0, e:e + 1, tok] == slot, 1.0, 0.0).astype(BF16)

        def place(e, start, new, offs=offs):
            slot = start + lax.broadcasted_iota(I32, (W, 1), 0)
            old = xin_ref[e, 0, pl.ds(start, W), :].astype(F32)
            xin_ref[e, 0, pl.ds(start, W), :] = jnp.where(slot >= offs[e], new, old).astype(xin_ref.dtype)

        starts = [_window_start(offs[e], cap, W) for e in range(NE)]
        res = _dot(jnp.concatenate([onehot(e, starts[e]) for e in range(NE)], axis=0), h_ref[0, tok, :])
        for e in range(NE):
            place(e, starts[e], res[e * W:(e + 1) * W])

        n_extra = [(jnp.maximum(ends[e] - starts[e] - W, 0) + W - 1) // W for e in range(NE)]

        @pl.when(functools.reduce(jnp.maximum, n_extra) > 0)
        def _(tok=tok, starts=starts, n_extra=n_extra, onehot=onehot, place=place):
            for e in range(NE):
                def extra(c, carry, e=e):
                    start = _window_start(starts[e] + W + c * W, cap, W)
                    place(e, start, _dot(onehot(e, start), h_ref[0, tok, :]))
                    return carry

                lax.fori_loop(0, n_extra[e], extra, 0)


def _gather(h2, pos, offs):
    B, S, D = h2.shape
    NE = N_EXPERTS
    cap = (EC_CAPACITY_FACTOR * S) // NE
    ts = MOE_TOKEN_TILE * GATHER_STEP_TILES
    grid_spec = pltpu.PrefetchScalarGridSpec(
        num_scalar_prefetch=1,
        grid=(B, S // ts),
        in_specs=[pl.BlockSpec((1, ts, D), lambda b, j, o: (b, j, 0)),
                  pl.BlockSpec((1, NE, ts), lambda b, j, o: (b, 0, j))],
        out_specs=pl.BlockSpec((NE, 1, cap, D), lambda b, j, o: (0, b, 0, 0)),
    )
    return pl.pallas_call(
        _gather_kernel,
        grid_spec=grid_spec,
        out_shape=jax.ShapeDtypeStruct((NE, B, cap, D), BF16),
        compiler_params=_params(("parallel", "arbitrary"), V7X_VMEM_LIMIT),
        name="gather",
    )(offs.reshape(-1), h2, pos)


def _expert_kernel(x_ref, wg_ref, wu_ref, wd_ref, o_ref, acc_s, wg_s, wu_s, wd_s, *, row_tile):
    f = pl.program_id(2)
    nf = pl.num_programs(2)
    n_tiles = x_ref.shape[1] // row_tile

    def sweep(first, last):
        wg_s[...] = wg_ref[0].astype(BF16)
        wu_s[...] = wu_ref[0].astype(BF16)
        wd_s[...] = wd_ref[0].astype(BF16)

        def down(r, hid):
            y = _dot(hid, wd_s[...])
            if not first:
                y = y + acc_s[r, :]
            if last:
                o_ref[0, r, :] = y.astype(o_ref.dtype)
            else:
                acc_s[r, :] = y

        pending = None
        for i in range(n_tiles):
            r = slice(i * row_tile, (i + 1) * row_tile)
            xb = x_ref[0, r, :]
            g = _dot(xb, wg_s[...])
            u = _dot(xb, wu_s[...])
            if pending is not None:
                down(*pending)
            pending = (r, (g * _sigmoid(g) * u).astype(BF16))
        down(*pending)

    @pl.when(f == 0)
    def _():
        sweep(True, False)

    @pl.when(jnp.logical_and(f > 0, f < nf - 1))
    def _():
        sweep(False, False)

    @pl.when(f == nf - 1)
    def _():
        sweep(False, True)


def _experts(xin, w_gate, w_up, w_down):
    NE, R, D = xin.shape
    F = w_gate.shape[-1]
    tr = min(R, EXPERT_ROWS)
    tf = EXPERT_FF_TILE
    row_tile = min(tr, EXPERT_ROW_TILE)
    return pl.pallas_call(
        functools.partial(_expert_kernel, row_tile=row_tile),
        grid=(NE, R // tr, F // tf),
        in_specs=[pl.BlockSpec((1, tr, D), lambda e, r, f: (e, r, 0)),
                  pl.BlockSpec((1, D, tf), lambda e, r, f: (e, 0, f)),
                  pl.BlockSpec((1, D, tf), lambda e, r, f: (e, 0, f)),
                  pl.BlockSpec((1, tf, D), lambda e, r, f: (e, f, 0))],
        out_specs=pl.BlockSpec((1, tr, D), lambda e, r, f: (e, r, 0)),
        out_shape=jax.ShapeDtypeStruct((NE, R, D), BF16),
        scratch_shapes=[pltpu.VMEM((tr, D), F32), pltpu.VMEM((D, tf), BF16),
                        pltpu.VMEM((D, tf), BF16), pltpu.VMEM((tf, D), BF16)],
        compiler_params=_params(("parallel", "parallel", "arbitrary"), V7X_VMEM_LIMIT),
        name="experts",
    )(xin, w_gate, w_up, w_down)


def _combine_kernel(off_ref, y_ref, pos_ref, gate_ref, x_ref, g2_ref, o_ref):
    NE, _, cap, D = y_ref.shape
    TT, W = MOE_TOKEN_TILE, MOE_WINDOW
    b, j = pl.program_id(0), pl.program_id(1)
    nt = pl.num_programs(1) * MOE_STEP_TILES

    for t in range(MOE_STEP_TILES):
        tok = slice(t * TT, (t + 1) * TT)
        base = (b * NE) * (nt + 1) + j * MOE_STEP_TILES + t
        offs = [off_ref[base + e * (nt + 1)] for e in range(NE)]
        ends = [off_ref[base + e * (nt + 1) + 1] for e in range(NE)]
        starts = [_window_start(offs[e], cap, W) for e in range(NE)]

        def weighted_onehot(e, start, lo, tok=tok):
            slot = start + lax.broadcasted_iota(I32, (W, TT), 0)
            hit = (pos_ref[0, e:e + 1, tok] == slot) & (slot >= lo)
            return jnp.where(hit, gate_ref[0, e:e + 1, tok], 0.0).astype(BF16)

        scatter = jnp.concatenate([weighted_onehot(e, starts[e], 0) for e in range(NE)], axis=0)
        ystack = jnp.concatenate([y_ref[e, 0, pl.ds(starts[e], W), :] for e in range(NE)], axis=0)
        o_ref[0, tok, :] = x_ref[0, tok, :] + g2_ref[0] * _dot_tn(scatter, ystack)

        n_extra = [(jnp.maximum(ends[e] - starts[e] - W, 0) + W - 1) // W for e in range(NE)]

        @pl.when(functools.reduce(jnp.maximum, n_extra) > 0)
        def _(tok=tok, starts=starts, n_extra=n_extra, weighted_onehot=weighted_onehot):
            for e in range(NE):
                def extra(c, carry, e=e):
                    lo = starts[e] + W + c * W
                    start = _window_start(lo, cap, W)
                    part = _dot_tn(weighted_onehot(e, start, lo), y_ref[e, 0, pl.ds(start, W), :])
                    o_ref[0, tok, :] = o_ref[0, tok, :] + g2_ref[0] * part
                    return carry

                lax.fori_loop(0, n_extra[e], extra, 0)


def _combine(y, pos, gate, offs, x1, gate2):
    B, S, D = x1.shape
    NE, _, cap, _ = y.shape
    ts = MOE_TOKEN_TILE * MOE_STEP_TILES
    row = lambda b, j, o: (b, j, 0)
    grid_spec = pltpu.PrefetchScalarGridSpec(
        num_scalar_prefetch=1,
        grid=(B, S // ts),
        in_specs=[pl.BlockSpec((NE, 1, cap, D), lambda b, j, o: (0, b, 0, 0)),
                  pl.BlockSpec((1, NE, ts), lambda b, j, o: (b, 0, j)),
                  pl.BlockSpec((1, NE, ts), lambda b, j, o: (b, 0, j)),
                  pl.BlockSpec((1, ts, D), row),
                  pl.BlockSpec((1, 1, D), lambda b, j, o: (b, 0, 0))],
        out_specs=pl.BlockSpec((1, ts, D), row),
    )
    return pl.pallas_call(
        _combine_kernel,
        grid_spec=grid_spec,
        out_shape=jax.ShapeDtypeStruct((B, S, D), F32),
        compiler_params=_params(("parallel", "arbitrary"), V7X_VMEM_LIMIT),
        name="combine",
    )(offs.reshape(-1), y, pos, gate, x1, gate2)


def _inproj_weight(w_in):
    D = w_in.shape[0]
    H = MLSTM_HEADS
    g0 = 2 * MLSTM_WIDTH
    gates = w_in[:, g0:g0 + N_GATES].reshape(D, 4, H).transpose(0, 2, 1).reshape(D, N_GATES)
    main = jnp.concatenate([w_in[:, :g0], w_in[:, g0 + N_GATES:], gates,
                            jnp.zeros((D, V7X_LANES - N_GATES), w_in.dtype)], axis=1)
    return main.astype(BF16)


def kernel(x, c, w_ada, b_ada, norm1_g, w_in, conv_w, conv_b, w_q_blk, w_k_blk, w_v_blk, b_igate, b_fgate,
           mlstm_norm_g, mlstm_skip, q_norm_g, k_norm_g, rel_bias, w_out, norm2_g, w_router, b_router,
           w_gate, w_up, w_down):
    B, S, D = x.shape
    depth = w_ada.shape[0]
    diag = _attn_bias_diagonals(rel_bias)
    for l in range(depth):
        mod = _ada(c, w_ada[l], b_ada[l])
        shift1, scale1, gate1, shift2, scale2, gate2 = (
            mod[:, i * D:(i + 1) * D].reshape(B, 1, D) for i in range(N_MOD))

        x_m, o_pre, a_q, a_k, a_v, gates = _inproj(x, scale1, shift1, norm1_g[l].reshape(1, D),
                                                   _inproj_weight(w_in[l]), q_norm_g[l], k_norm_g[l])
        y_m = _mlstm(x_m, o_pre, gates, conv_w[l], conv_b[l], w_q_blk[l], w_k_blk[l], w_v_blk[l],
                     b_igate[l], b_fgate[l], mlstm_norm_g[l], mlstm_skip[l])
        y_a = _attn(a_q, a_k, a_v, diag)
        x1 = _outproj(y_m, y_a, x, gate1, w_out[l])

        h2, pos, aff, offs = _select(x1, scale2, shift2, norm2_g[l].reshape(1, D), w_router[l], b_router[l])
        xin = _gather(h2, pos, offs)
        NE, _, cap, _ = xin.shape
        y = _experts(xin.reshape(NE, B * cap, D), w_gate[l], w_up[l], w_down[l]).reshape(NE, B, cap, D)
        x = _combine(y, pos, aff, offs, x1, gate2)
    return x
```

```python
import functools
import math

import numpy as np
import jax
import jax.numpy as jnp
from jax import lax
from jax.experimental import pallas as pl
from jax.experimental.pallas import tpu as pltpu

F32 = jnp.float32
BF16 = jnp.bfloat16
I32 = jnp.int32

NORM_EPS = 1e-6
N_MOD = 6
ADA_COL_TILE = 1024
ROW_TILE = 1024
OUT_ROW_TILE = 2048
MLSTM_HEADS = 4
MLSTM_HEAD_DIM = 128
MLSTM_WIDTH = MLSTM_HEADS * MLSTM_HEAD_DIM
N_GATES = 4 * MLSTM_HEADS
MLSTM_QKV_BLOCK = 4
MLSTM_CONV = 5
MLSTM_CHUNK = 128
MLSTM_STEP_HEADS = 4
MLSTM_ONES_ROWS = 16
ATTN_HEADS = 8
ATTN_HEAD_DIM = 64
ATTN_WIDTH = ATTN_HEADS * ATTN_HEAD_DIM
DILATIONS = (1, 4, 16)
ATTN_HALF = 64
ATTN_QBLK = 128
ATTN_KBLK = 256
ATTN_DIAG = 512
ATTN_GROUP = 2
LOG2E = math.log2(math.e)
REL_BUCKETS = 32
REL_MAX_DIST = 1024
N_EXPERTS = 16
EC_CAPACITY_FACTOR = 2
EXPERT_ROWS = 2048
EXPERT_ROW_TILE = 1024
EXPERT_FF_TILE = 512
MOE_TOKEN_TILE = 256
MOE_STEP_TILES = 4
GATHER_STEP_TILES = 4
MOE_WINDOW = 64
NEG_BIG = -1e30

V7X_LANES = 128
V7X_VMEM_LIMIT = 56 * 1024 * 1024


def _sigmoid(x):
    return 1.0 / (1.0 + jnp.exp(-x))


def _dot(a, b):
    return jnp.dot(a, b, preferred_element_type=F32)


def _dot_nt(a, b):
    return lax.dot_general(a, b, (((1,), (1,)), ((), ())), preferred_element_type=F32)


def _dot_tn(a, b):
    return lax.dot_general(a, b, (((0,), (0,)), ((), ())), preferred_element_type=F32)


def _split_bf16(x):
    hi = x.astype(BF16)
    lo = (x - hi.astype(F32)).astype(BF16)
    return hi, lo


def _params(sem, vmem=None):
    return pltpu.CompilerParams(dimension_semantics=sem, vmem_limit_bytes=vmem)


def _ada_kernel(c_ref, w_ref, b_ref, o_ref):
    c = c_ref[...]
    s = c * _sigmoid(c)
    o_ref[...] = jnp.dot(s, w_ref[...], preferred_element_type=F32,
                         precision=lax.Precision.HIGHEST) + b_ref[...]


def _ada(c, w, b):
    B, D = c.shape
    N = w.shape[1]
    tn = ADA_COL_TILE
    return pl.pallas_call(
        _ada_kernel,
        grid=(N // tn,),
        in_specs=[pl.BlockSpec((B, D), lambda j: (0, 0)),
                  pl.BlockSpec((D, tn), lambda j: (0, j)),
                  pl.BlockSpec((1, tn), lambda j: (0, j))],
        out_specs=pl.BlockSpec((B, tn), lambda j: (0, j)),
        out_shape=jax.ShapeDtypeStruct((B, N), F32),
        compiler_params=_params(("arbitrary",)),
        name="ada",
    )(c, w, b.reshape(1, N))


def _modulated_norm(x, g, scale, shift):
    ms = jnp.mean(x * x, axis=-1, keepdims=True)
    return x * lax.rsqrt(ms + NORM_EPS) * (g * (1.0 + scale)) + shift


def _inproj_kernel(x_ref, sc_ref, sh_ref, g_ref, w_ref, hm_ref, qg_ref, kg_ref,
                   xm_ref, op_ref, q_ref, k_ref, v_ref, gt_ref):
    h = _modulated_norm(x_ref[0], g_ref[...], sc_ref[0], sh_ref[0]).astype(BF16)
    cw = MLSTM_WIDTH
    col = lambda i: _dot(h, w_ref[:, cw * i:cw * (i + 1)])
    q = col(2)
    k = col(3)
    xm_ref[0] = col(0)
    op_ref[0] = col(1)
    v_ref[0] = col(4)
    gt_ref[0] = _dot(h, w_ref[:, 5 * cw:5 * cw + V7X_LANES]).T[:N_GATES, :]

    def head_norm(t, g):
        ms = _dot((t * t).astype(BF16), hm_ref[...])
        return t * lax.rsqrt(ms + NORM_EPS) * g

    q_ref[0] = head_norm(q, qg_ref[...])
    k_ref[0] = head_norm(k, kg_ref[...])


def _inproj(x, scale, shift, g, w, q_norm_g, k_norm_g):
    B, S, D = x.shape
    tm = ROW_TILE
    W = ATTN_WIDTH
    assert W == MLSTM_WIDTH
    hid = jnp.arange(W) // ATTN_HEAD_DIM
    head_mean = jnp.where(hid[:, None] == hid[None, :], 1.0 / ATTN_HEAD_DIM, 0.0).astype(BF16)
    qg = jnp.tile(q_norm_g, ATTN_HEADS).reshape(1, W) * (LOG2E / math.sqrt(ATTN_HEAD_DIM))
    kg = jnp.tile(k_norm_g, ATTN_HEADS).reshape(1, W)
    row = lambda b, i: (b, i, 0)
    vec = lambda b, i: (b, 0, 0)
    const = lambda b, i: (0, 0)
    outs = ([jax.ShapeDtypeStruct((B, S, MLSTM_WIDTH), F32)] * 2 + [jax.ShapeDtypeStruct((B, S, W), F32)] * 3
            + [jax.ShapeDtypeStruct((B, N_GATES, S), F32)])
    return pl.pallas_call(
        _inproj_kernel,
        grid=(B, S // tm),
        in_specs=[pl.BlockSpec((1, tm, D), row),
                  pl.BlockSpec((1, 1, D), vec),
                  pl.BlockSpec((1, 1, D), vec),
                  pl.BlockSpec((1, D), const),
                  pl.BlockSpec(w.shape, const),
                  pl.BlockSpec((W, W), const),
                  pl.BlockSpec((1, W), const),
                  pl.BlockSpec((1, W), const)],
        out_specs=[pl.BlockSpec((1, tm, W), row)] * 5 + [pl.BlockSpec((1, N_GATES, tm), lambda b, i: (b, 0, i))],
        out_shape=outs,
        compiler_params=_params(("parallel", "arbitrary"), V7X_VMEM_LIMIT),
        name="inproj",
    )(x, scale, shift, g, w, head_mean, qg, kg)


def _chunk_scan(x, op, reverse):
    n = x.shape[1]
    idx = lax.broadcasted_iota(I32, x.shape, 1) & (MLSTM_CHUNK - 1)
    s = 1
    while s < MLSTM_CHUNK:
        if reverse:
            r = pltpu.roll(x, n - s, 1)
            x = jnp.where(idx < MLSTM_CHUNK - s, op(x, r), x)
        else:
            r = pltpu.roll(x, s, 1)
            x = jnp.where(idx >= s, op(x, r), x)
        s *= 2
    return x


def _log_sigmoid(x):
    return jnp.minimum(x, 0.0) - jnp.log(1.0 + jnp.exp(-jnp.abs(x)))


def _mlstm_kernel(xm_ref, op_ref, grow_ref, brow_ref, cw_ref, cb_ref,
                  wqt_ref, wk_ref, wvt_ref, ng_ref, sk_ref, y_ref,
                  qt_s, k_s, vat_s, xc_s, hf_s, hb_s, row_s, col_s):
    S = xm_ref.shape[1]
    L = MLSTM_CHUNK
    NC = S // L
    E = MLSTM_HEAD_DIM
    A = E + MLSTM_ONES_ROWS
    NH = MLSTM_STEP_HEADS
    half = MLSTM_CONV // 2
    rows = lax.broadcasted_iota(I32, (S, E), 0)
    kj = lax.broadcasted_iota(I32, (L, L), 0)
    qi = lax.broadcasted_iota(I32, (L, L), 1)
    causal = (kj <= qi, kj >= qi)

    def projections(hh):
        lanes = slice(hh * E, (hh + 1) * E)
        xm = xm_ref[0, :, lanes]
        conv = xm * cw_ref[half:half + 1, lanes]
        for j in range(MLSTM_CONV):
            off = j - half
            if off == 0:
                continue
            shifted = pltpu.roll(xm, (-off) % S, 0)
            valid = rows >= -off if off < 0 else rows < S - off
            conv = conv + jnp.where(valid, shifted, 0.0) * cw_ref[j:j + 1, lanes]
        conv = conv + cb_ref[:, lanes]
        xc = conv * _sigmoid(conv)
        xc_s[hh] = xc
        xcb = xc.astype(BF16)
        qt_s[hh] = _dot_nt(wqt_ref[hh], xcb).astype(BF16)
        k_s[hh] = (_dot(xcb, wk_ref[hh]) * (1.0 / math.sqrt(E))).astype(BF16)
        vat_s[hh, :E, :] = _dot_nt(wvt_ref[hh], xm.astype(BF16))
        vat_s[hh, E:, :] = jnp.ones((MLSTM_ONES_ROWS, S), F32)

    def gate_stats(hh):
        gr = grow_ref[0, 4 * hh:4 * hh + 4, :] + brow_ref[hh]
        kind = lax.broadcasted_iota(I32, gr.shape, 0)
        gr = jnp.where((kind & 1) == 1, _log_sigmoid(gr), gr)
        b_f = _chunk_scan(gr[1:2], jnp.add, False)
        b_b = _chunk_scan(gr[3:4], jnp.add, True)
        u_f = gr[0:1] - b_f
        u_b = gr[2:3] - b_b
        m_f = _chunk_scan(u_f, jnp.maximum, False)
        m_b = _chunk_scan(u_b, jnp.maximum, True)
        row_s[hh] = jnp.concatenate([b_f, m_f, u_f, gr[1:2], b_b, m_b, u_b, gr[3:4]], axis=0)
        for k, stat in enumerate((u_f, u_b)):
            for c in range(NC):
                col_s[hh, k, c * L:(c + 1) * L, :] = jnp.broadcast_to(stat[:, c * L:(c + 1) * L], (L, L)).T

    def local_part(hh, c, dirn):
        r0 = c * L
        qt = qt_s[hh, :, pl.ds(r0, L)]
        kc = k_s[hh, pl.ds(r0, L), :]
        vat = vat_s[hh, :, pl.ds(r0, L)]
        rr = row_s[hh, :, pl.ds(r0, L)]
        b_r, m_r, u_r = (rr[4 * dirn + i:4 * dirn + i + 1, :] for i in range(3))
        g = jnp.sum(rr[4 * dirn + 3:4 * dirn + 4, :], axis=1, keepdims=True)
        m_loc = g + jnp.max(u_r, axis=1, keepdims=True)
        return dict(
            hh=hh, r0=r0, dirn=dirn, qt=qt, vat=vat.astype(BF16), b_r=b_r, m_r=m_r, g=g, m_loc=m_loc,
            u_c=col_s[hh, dirn, pl.ds(r0, L), :],
            s=_dot(kc, qt),
            d_state=_dot((vat * jnp.exp(g + u_r - m_loc)).astype(BF16), kc))

    def intra_part(t):
        p = jnp.exp(jnp.where(causal[t["dirn"]], t["u_c"] - t["m_r"], NEG_BIG)) * t.pop("s")
        t["y_loc"] = _dot(t["vat"], p.astype(BF16))

    def state_part(t, state):
        Ct, m = state
        x_st = _dot(Ct.astype(BF16), t["qt"])
        mm = jnp.maximum(m, t["m_r"])
        z = jnp.exp(m - mm) * x_st + jnp.exp(t["m_r"] - mm) * t["y_loc"]
        h = z[:E, :] / jnp.maximum(jnp.abs(z[E:E + 1, :]), jnp.exp(-t["b_r"] - mm))
        m_new = jnp.maximum(t["g"] + m, t["m_loc"])
        Ct_new = jnp.exp(t["g"] + m - m_new) * Ct + jnp.exp(t["m_loc"] - m_new) * t["d_state"]
        return h, (Ct_new, m_new)

    for hh in range(NH):
        projections(hh)
    for hh in range(NH):
        gate_stats(hh)
    work = [local_part(hh, c if dirn == 0 else NC - 1 - c, dirn)
            for c in range(NC) for hh in range(NH) for dirn in range(2)]
    for t in work:
        intra_part(t)
    out_s = (hf_s, hb_s)
    states = {(hh, dirn): (jnp.zeros((A, E), F32), jnp.zeros((1, 1), F32)) for hh in range(NH) for dirn in range(2)}
    for t in work:
        key = (t["hh"], t["dirn"])
        h, states[key] = state_part(t, states[key])
        out_s[t["dirn"]][t["hh"], :, pl.ds(t["r0"], L)] = h

    for hh in range(NH):
        lanes = slice(hh * E, (hh + 1) * E)
        h = (hf_s[hh] + hb_s[hh]).T
        hn = h * lax.rsqrt(jnp.mean(h * h, axis=-1, keepdims=True) + NORM_EPS) * ng_ref[:, lanes]
        y = (hn + sk_ref[:, lanes] * xc_s[hh]) * _sigmoid(op_ref[0, :, lanes])
        y_ref[0, :, lanes] = y.astype(y_ref.dtype)


def _blockdiag_dense(w_blk, transposed=False):
    E, Q = MLSTM_HEAD_DIM, MLSTM_QKV_BLOCK
    rows = w_blk.reshape(MLSTM_HEADS, E, Q)
    idx = np.arange(E)
    spread = jnp.asarray(idx[None, :] % Q == np.arange(Q)[:, None], w_blk.dtype)
    same_block = jnp.asarray(idx[:, None] // Q == idx[None, :] // Q, w_blk.dtype)
    out = 'hcr' if transposed else 'hrc'
    return jnp.einsum(f'hrj,jc->{out}', rows, spread, precision=lax.Precision.HIGHEST) * same_block


def _mlstm(x_m, o_pre, gates, conv_w, conv_b, w_q_blk, w_k_blk, w_v_blk, b_igate, b_fgate, norm_g, skip):
    B, S, W = x_m.shape
    H, E, NH = MLSTM_HEADS, MLSTM_HEAD_DIM, MLSTM_STEP_HEADS
    A = E + MLSTM_ONES_ROWS
    assert S % MLSTM_CHUNK == 0 and H % NH == 0
    bk = jnp.stack([b_igate[0], b_fgate[0], b_igate[1], b_fgate[1]], axis=0)
    brow = bk.T.reshape(H, 4, 1)
    wqt = _blockdiag_dense(w_q_blk, transposed=True).astype(BF16)
    wk = _blockdiag_dense(w_k_blk).astype(BF16)
    wvt = _blockdiag_dense(w_v_blk, transposed=True).astype(BF16)
    headcol = lambda b, h: (b, 0, h)
    perhead = lambda b, h: (h, 0, 0)
    lanes = lambda b, h: (0, h)
    return pl.pallas_call(
        _mlstm_kernel,
        grid=(B, H // NH),
        in_specs=[pl.BlockSpec((1, S, NH * E), headcol),
                  pl.BlockSpec((1, S, NH * E), headcol),
                  pl.BlockSpec((1, 4 * NH, S), lambda b, h: (b, h, 0)),
                  pl.BlockSpec((NH, 4, 1), perhead),
                  pl.BlockSpec((MLSTM_CONV, NH * E), lanes),
                  pl.BlockSpec((1, NH * E), lanes),
                  pl.BlockSpec((NH, E, E), perhead),
                  pl.BlockSpec((NH, E, E), perhead),
                  pl.BlockSpec((NH, E, E), perhead),
                  pl.BlockSpec((1, NH * E), lanes),
                  pl.BlockSpec((1, NH * E), lanes)],
        out_specs=pl.BlockSpec((1, S, NH * E), headcol),
        out_shape=jax.ShapeDtypeStruct((B, S, W), BF16),
        scratch_shapes=[pltpu.VMEM((NH, E, S), BF16), pltpu.VMEM((NH, S, E), BF16),
                        pltpu.VMEM((NH, A, S), F32),
                        pltpu.VMEM((NH, S, E), F32), pltpu.VMEM((NH, E, S), F32), pltpu.VMEM((NH, E, S), F32),
                        pltpu.VMEM((NH, 8, S), F32), pltpu.VMEM((NH, 2, S, E), F32)],
        compiler_params=_params(("parallel", "arbitrary"), V7X_VMEM_LIMIT),
        name="mlstm",
    )(x_m, o_pre, gates, brow, conv_w, conv_b.reshape(1, W), wqt, wk, wvt,
      norm_g.reshape(1, W), skip.reshape(1, W))


def _t5_bucket_static(rel):
    half = REL_BUCKETS // 2
    exact = half // 2
    n = np.abs(rel)
    log_ratio = (np.log(np.maximum(n, 1).astype(np.float32) / np.float32(exact))
                 / np.float32(math.log(REL_MAX_DIST / exact)))
    large = np.minimum(exact + (log_ratio * np.float32(half - exact)).astype(np.int32), half - 1)
    return np.where(rel > 0, half, 0) + np.where(n < exact, n, large)


def _attn_bias_diagonals(rel_bias):
    n = ATTN_DIAG
    nv = 3 * len(DILATIONS)
    x = np.arange(n)
    offset = np.where(x <= ATTN_KBLK, x, x - n)
    rel = offset[None, :] - ATTN_HALF * np.arange(3)[:, None]
    valid = np.tile(np.abs(rel) <= ATTN_HALF, (len(DILATIONS), 1))
    bucket = np.concatenate([_t5_bucket_static(rel * d) for d in DILATIONS], axis=0)
    onehot = (bucket[..., None] == np.arange(REL_BUCKETS)) & valid[..., None]
    w = jnp.einsum('vnb,bh->vhn', jnp.asarray(onehot, F32), rel_bias.astype(F32) * LOG2E,
                   precision=lax.Precision.HIGHEST)
    w = jnp.where(jnp.asarray(valid)[:, None, :], w, NEG_BIG)
    w = w.reshape(nv, ATTN_HEADS // 2, 2, n).transpose(1, 0, 2, 3)
    return w.reshape(ATTN_HEADS // 2, 2 * nv, n)


def _attn_kernel(q_ref, k_ref, v_ref, diag_ref, y_ref, q0_s, q1_s, k_s, v0_s, v1_s, o_s, l_s, bias_s):
    S = q_ref.shape[1]
    QB, KB = ATTN_QBLK, ATTN_KBLK
    lane = lax.broadcasted_iota(I32, (1, V7X_LANES), 1)
    in_head = (lane < ATTN_HEAD_DIM, lane >= ATTN_HEAD_DIM)
    q_s = (q0_s, q1_s)
    v_s = (v0_s, v1_s)

    pair = pl.program_id(1)

    @pl.when(pl.program_id(0) == 0)
    def _():
        for i in range(bias_s.shape[1]):
            rows_i = jnp.broadcast_to(diag_ref[0, i:i + 1, :], (QB, ATTN_DIAG))
            bias_s[pair, i] = pltpu.roll(rows_i, 0, 1, stride=1, stride_axis=0)[:, :KB]

    def strided(start, size, d):
        return pl.ds(start, size) if d == 1 else pl.ds(start, size, stride=d)

    for p, d in enumerate(DILATIONS):
        L = S // d
        for r in range(d):
            src = strided(r, L, d)
            dst = slice(r * L, (r + 1) * L)
            q = q_ref[0, src, :]
            v = v_ref[0, src, :]
            k_s[p, dst, :] = k_ref[0, src, :].astype(BF16)
            for a in range(2):
                q_s[a][p, dst, :] = jnp.where(in_head[a], q, 0.0).astype(BF16)
                v_s[a][p, dst, :] = jnp.where(in_head[a], v, 1.0).astype(BF16)

    def logits(p, d, r, qb):
        L = S // d
        nqb = L // QB
        nk = min(L, KB)
        if nqb == 1 or qb == 0:
            k0, variant = 0, 0
        elif qb == nqb - 1:
            k0, variant = L - nk, 2
        else:
            k0, variant = qb * QB - ATTN_HALF, 1
        qrows = slice(r * L + qb * QB, r * L + (qb + 1) * QB)
        krows = slice(r * L + k0, r * L + k0 + nk)
        kt = k_s[p, krows, :]
        s = [_dot_nt(q_s[a][p, qrows, :], kt) + bias_s[pair, p * 6 + variant * 2 + a][:, :nk] for a in range(2)]
        return dict(p=p, krows=krows, out_rows=strided(r + d * qb * QB, QB, d), s=s)

    def softmax(t):
        t["m"] = [jnp.max(s, axis=1, keepdims=True) for s in t["s"]]
        t["e"] = [jnp.exp2(s - m).astype(BF16) for s, m in zip(t.pop("s"), t["m"])]

    def outputs(t):
        p = t["p"]
        acc = [_dot(t["e"][a], v_s[a][p, t["krows"], :]) for a in range(2)]
        num = jnp.where(in_head[0], acc[0], acc[1])
        den = pltpu.roll(jnp.where(in_head[0], acc[1], acc[0]), ATTN_HEAD_DIM, 1)
        o_s[p, t["out_rows"], :] = num / den
        l_s[p, t["out_rows"], :] = jnp.where(in_head[0], t["m"][0], t["m"][1]) + jnp.log2(den)

    units = [(p, d, r, qb) for p, d in enumerate(DILATIONS) for r in range(d) for qb in range(S // d // QB)]
    prev = []
    for i in range(0, len(units), ATTN_GROUP):
        cur = [logits(*u) for u in units[i:i + ATTN_GROUP]]
        for t in prev:
            outputs(t)
        for t in cur:
            softmax(t)
        prev = cur
    for t in prev:
        outputs(t)

    mx = jnp.maximum(jnp.maximum(l_s[0], l_s[1]), l_s[2])
    num = jnp.zeros((S, V7X_LANES), F32)
    den = jnp.zeros((S, V7X_LANES), F32)
    for p in range(len(DILATIONS)):
        w = jnp.exp2(l_s[p] - mx)
        num = num + w * o_s[p]
        den = den + w
    y_ref[0] = (num / den).astype(y_ref.dtype)


def _attn(a_q, a_k, a_v, diag):
    B, S, W = a_q.shape
    P = ATTN_HEADS // 2
    NP = len(DILATIONS)
    pair = lambda b, p: (b, 0, p)
    blk = pl.BlockSpec((1, S, V7X_LANES), pair)
    return pl.pallas_call(
        _attn_kernel,
        grid=(B, P),
        in_specs=[blk, blk, blk, pl.BlockSpec((1,) + diag.shape[1:], lambda b, p: (p, 0, 0))],
        out_specs=blk,
        out_shape=jax.ShapeDtypeStruct((B, S, W), BF16),
        scratch_shapes=[pltpu.VMEM((NP, S, V7X_LANES), BF16)] * 5 + [pltpu.VMEM((NP, S, V7X_LANES), F32)] * 2
                       + [pltpu.VMEM((P, diag.shape[1], ATTN_QBLK, ATTN_KBLK), F32)],
        compiler_params=_params(("arbitrary", "arbitrary"), V7X_VMEM_LIMIT),
        name="attn",
    )(a_q, a_k, a_v, diag)


def _outproj_kernel(ym_ref, ya_ref, x_ref, g1_ref, w1_ref, w2_ref, o_ref):
    mix = _dot(ym_ref[0], w1_ref[...]) + _dot(ya_ref[0], w2_ref[...])
    o_ref[0] = x_ref[0] + g1_ref[0] * mix


def _outproj(y_m, y_a, x, gate1, w_out):
    B, S, D = x.shape
    Wm = y_m.shape[-1]
    tm = OUT_ROW_TILE
    w1 = w_out[:Wm].astype(BF16)
    w2 = w_out[Wm:].astype(BF16)
    row = lambda b, i: (b, i, 0)
    const = lambda b, i: (0, 0)
    return pl.pallas_call(
        _outproj_kernel,
        grid=(B, S // tm),
        in_specs=[pl.BlockSpec((1, tm, Wm), row),
                  pl.BlockSpec((1, tm, y_a.shape[-1]), row),
                  pl.BlockSpec((1, tm, D), row),
                  pl.BlockSpec((1, 1, D), lambda b, i: (b, 0, 0)),
                  pl.BlockSpec(w1.shape, const),
                  pl.BlockSpec(w2.shape, const)],
        out_specs=pl.BlockSpec((1, tm, D), row),
        out_shape=jax.ShapeDtypeStruct((B, S, D), F32),
        compiler_params=_params(("parallel", "arbitrary"), V7X_VMEM_LIMIT),
        name="outproj",
    )(y_m, y_a, x, gate1, w1, w2)


def _select_kernel(x_ref, sc_ref, sh_ref, g_ref, whl_ref, br_ref, tri_ref,
                   h_ref, pos_ref, gate_ref, off_ref, *, cap):
    S = x_ref.shape[1]
    NE = N_EXPERTS
    h = _modulated_norm(x_ref[0], g_ref[...], sc_ref[0], sh_ref[0])
    hi, lo = _split_bf16(h)
    h_ref[0] = hi
    both = _dot(hi, whl_ref[...])
    logits = both[:, :V7X_LANES] + both[:, V7X_LANES:] + _dot(lo, whl_ref[:, :V7X_LANES])
    lt = logits.T[:NE, :] + br_ref[...]
    ex = jnp.exp(lt - jnp.max(lt, axis=0, keepdims=True))
    aff = ex / jnp.sum(ex, axis=0, keepdims=True)
    gate_ref[0] = aff

    bits = pltpu.bitcast(aff, I32)

    def count_ge(cand):
        return jnp.sum((bits >= cand).astype(F32), axis=1, keepdims=True)

    def search(i, v):
        shift = 27 - 3 * i
        best = v
        for c in range(1, 8):
            cand = v | lax.shift_left(jnp.int32(c), shift)
            best = jnp.where(count_ge(cand) >= cap, cand, best)
        return best

    top = jnp.full((NE, 1), 1 << 30, I32)
    thr = lax.fori_loop(0, 10, search, jnp.where(count_ge(top) >= cap, top, 0))
    gt = (bits > thr).astype(F32)
    eq = (bits == thr).astype(F32)
    need = cap - jnp.sum(gt, axis=1, keepdims=True)

    def prefix_count(mask):
        off = jnp.zeros((NE, 1), F32)
        parts, starts = [], []
        for j in range(S // V7X_LANES):
            t = mask[:, j * V7X_LANES:(j + 1) * V7X_LANES]
            starts.append(off)
            parts.append(_dot(t.astype(BF16), tri_ref[...]) + off)
            off = off + jnp.sum(t, axis=1, keepdims=True)
        return jnp.concatenate(parts, axis=1), starts + [off]

    sel = jnp.maximum(gt, jnp.where(prefix_count(eq)[0] < need, eq, 0.0))
    slot, starts = prefix_count(sel)
    pos_ref[0] = jnp.where(sel > 0.0, slot, -1.0).astype(I32)
    step = MOE_TOKEN_TILE // V7X_LANES
    off_ref[0] = jnp.concatenate(starts[::step], axis=1).astype(I32)


def _select(x1, scale, shift, g, w_router, b_router):
    B, S, D = x1.shape
    NE = N_EXPERTS
    cap = (EC_CAPACITY_FACTOR * S) // NE
    nt = S // MOE_TOKEN_TILE
    wpad = jnp.zeros((D, V7X_LANES), F32).at[:, :NE].set(w_router)
    whl = jnp.concatenate(_split_bf16(wpad), axis=1)
    ti = jnp.arange(V7X_LANES)
    tri = (ti[:, None] < ti[None, :]).astype(BF16)
    vec = lambda b: (b, 0, 0)
    const = lambda b: (0, 0)
    return pl.pallas_call(
        functools.partial(_select_kernel, cap=cap),
        grid=(B,),
        in_specs=[pl.BlockSpec((1, S, D), vec),
                  pl.BlockSpec((1, 1, D), vec),
                  pl.BlockSpec((1, 1, D), vec),
                  pl.BlockSpec((1, D), const),
                  pl.BlockSpec((D, 2 * V7X_LANES), const),
                  pl.BlockSpec((NE, 1), const),
                  pl.BlockSpec((V7X_LANES, V7X_LANES), const)],
        out_specs=[pl.BlockSpec((1, S, D), vec),
                   pl.BlockSpec((1, NE, S), vec),
                   pl.BlockSpec((1, NE, S), vec),
                   pl.BlockSpec((1, NE, nt + 1), vec)],
        out_shape=[jax.ShapeDtypeStruct((B, S, D), BF16),
                   jax.ShapeDtypeStruct((B, NE, S), I32),
                   jax.ShapeDtypeStruct((B, NE, S), F32),
                   jax.ShapeDtypeStruct((B, NE, nt + 1), I32)],
        compiler_params=_params(("parallel",), V7X_VMEM_LIMIT),
        name="select",
    )(x1, scale, shift, g, whl, b_router.reshape(NE, 1), tri)


def _window_start(off, cap, rows):
    return pl.multiple_of(jnp.minimum((off // 16) * 16, cap - rows), 16)


def _gather_kernel(off_ref, h_ref, pos_ref, xin_ref):
    NE, _, cap, D = xin_ref.shape
    TT, W = MOE_TOKEN_TILE, MOE_WINDOW
    b, j = pl.program_id(0), pl.program_id(1)
    nt = pl.num_programs(1) * GATHER_STEP_TILES

    @pl.when(j == 0)
    def _():
        xin_ref[...] = jnp.zeros(xin_ref.shape, xin_ref.dtype)

    for t in range(GATHER_STEP_TILES):
        tok = slice(t * TT, (t + 1) * TT)
        base = (b * NE) * (nt + 1) + j * GATHER_STEP_TILES + t
        offs = [off_ref[base + e * (nt + 1)] for e in range(NE)]
        ends = [off_ref[base + e * (nt + 1) + 1] for e in range(NE)]

        def onehot(e, start, tok=tok):
            slot = start + lax.broadcasted_iota(I32, (W, TT), 0)
            return jnp.where(pos_ref[0, e:e + 1, tok] == slot, 1.0, 0.0).astype(BF16)

        def place(e, start, new, offs=offs):
            slot = start + lax.broadcasted_iota(I32, (W, 1), 0)
            old = xin_ref[e, 0, pl.ds(start, W), :].astype(F32)
            xin_ref[e, 0, pl.ds(start, W), :] = jnp.where(slot >= offs[e], new, old).astype(xin_ref.dtype)

        starts = [_window_start(offs[e], cap, W) for e in range(NE)]
        res = _dot(jnp.concatenate([onehot(e, starts[e]) for e in range(NE)], axis=0), h_ref[0, tok, :])
        for e in range(NE):
            place(e, starts[e], res[e * W:(e + 1) * W])

        n_extra = [(jnp.maximum(ends[e] - starts[e] - W, 0) + W - 1) // W for e in range(NE)]

        @pl.when(functools.reduce(jnp.maximum, n_extra) > 0)
        def _(tok=tok, starts=starts, n_extra=n_extra, onehot=onehot, place=place):
            for e in range(NE):
                def extra(c, carry, e=e):
                    start = _window_start(starts[e] + W + c * W, cap, W)
                    place(e, start, _dot(onehot(e, start), h_ref[0, tok, :]))
                    return carry

                lax.fori_loop(0, n_extra[e], extra, 0)


def _gather(h2, pos, offs):
    B, S, D = h2.shape
    NE = N_EXPERTS
    cap = (EC_CAPACITY_FACTOR * S) // NE
    ts = MOE_TOKEN_TILE * GATHER_STEP_TILES
    grid_spec = pltpu.PrefetchScalarGridSpec(
        num_scalar_prefetch=1,
        grid=(B, S // ts),
        in_specs=[pl.BlockSpec((1, ts, D), lambda b, j, o: (b, j, 0)),
                  pl.BlockSpec((1, NE, ts), lambda b, j, o: (b, 0, j))],
        out_specs=pl.BlockSpec((NE, 1, cap, D), lambda b, j, o: (0, b, 0, 0)),
    )
    return pl.pallas_call(
        _gather_kernel,
        grid_spec=grid_spec,
        out_shape=jax.ShapeDtypeStruct((NE, B, cap, D), BF16),
        compiler_params=_params(("parallel", "arbitrary"), V7X_VMEM_LIMIT),
        name="gather",
    )(offs.reshape(-1), h2, pos)


def _expert_kernel(x_ref, wg_ref, wu_ref, wd_ref, o_ref, acc_s, wg_s, wu_s, wd_s, *, row_tile):
    f = pl.program_id(2)
    nf = pl.num_programs(2)
    n_tiles = x_ref.shape[1] // row_tile

    def sweep(first, last):
        wg_s[...] = wg_ref[0].astype(BF16)
        wu_s[...] = wu_ref[0].astype(BF16)
        wd_s[...] = wd_ref[0].astype(BF16)

        def down(r, hid):
            y = _dot(hid, wd_s[...])
            if not first:
                y = y + acc_s[r, :]
            if last:
                o_ref[0, r, :] = y.astype(o_ref.dtype)
            else:
                acc_s[r, :] = y

        pending = None
        for i in range(n_tiles):
            r = slice(i * row_tile, (i + 1) * row_tile)
            xb = x_ref[0, r, :]
            g = _dot(xb, wg_s[...])
            u = _dot(xb, wu_s[...])
            if pending is not None:
                down(*pending)
            pending = (r, (g * _sigmoid(g) * u).astype(BF16))
        down(*pending)

    @pl.when(f == 0)
    def _():
        sweep(True, False)

    @pl.when(jnp.logical_and(f > 0, f < nf - 1))
    def _():
        sweep(False, False)

    @pl.when(f == nf - 1)
    def _():
        sweep(False, True)


def _experts(xin, w_gate, w_up, w_down):
    NE, R, D = xin.shape
    F = w_gate.shape[-1]
    tr = min(R, EXPERT_ROWS)
    tf = EXPERT_FF_TILE
    row_tile = min(tr, EXPERT_ROW_TILE)
    return pl.pallas_call(
        functools.partial(_expert_kernel, row_tile=row_tile),
        grid=(NE, R // tr, F // tf),
        in_specs=[pl.BlockSpec((1, tr, D), lambda e, r, f: (e, r, 0)),
                  pl.BlockSpec((1, D, tf), lambda e, r, f: (e, 0, f)),
                  pl.BlockSpec((1, D, tf), lambda e, r, f: (e, 0, f)),
                  pl.BlockSpec((1, tf, D), lambda e, r, f: (e, f, 0))],
        out_specs=pl.BlockSpec((1, tr, D), lambda e, r, f: (e, r, 0)),
        out_shape=jax.ShapeDtypeStruct((NE, R, D), BF16),
        scratch_shapes=[pltpu.VMEM((tr, D), F32), pltpu.VMEM((D, tf), BF16),
                        pltpu.VMEM((D, tf), BF16), pltpu.VMEM((tf, D), BF16)],
        compiler_params=_params(("parallel", "parallel", "arbitrary"), V7X_VMEM_LIMIT),
        name="experts",
    )(xin, w_gate, w_up, w_down)


def _combine_kernel(off_ref, y_ref, pos_ref, gate_ref, x_ref, g2_ref, o_ref):
    NE, _, cap, D = y_ref.shape
    TT, W = MOE_TOKEN_TILE, MOE_WINDOW
    b, j = pl.program_id(0), pl.program_id(1)
    nt = pl.num_programs(1) * MOE_STEP_TILES

    for t in range(MOE_STEP_TILES):
        tok = slice(t * TT, (t + 1) * TT)
        base = (b * NE) * (nt + 1) + j * MOE_STEP_TILES + t
        offs = [off_ref[base + e * (nt + 1)] for e in range(NE)]
        ends = [off_ref[base + e * (nt + 1) + 1] for e in range(NE)]
        starts = [_window_start(offs[e], cap, W) for e in range(NE)]

        def weighted_onehot(e, start, lo, tok=tok):
            slot = start + lax.broadcasted_iota(I32, (W, TT), 0)
            hit = (pos_ref[0, e:e + 1, tok] == slot) & (slot >= lo)
            return jnp.where(hit, gate_ref[0, e:e + 1, tok], 0.0).astype(BF16)

        scatter = jnp.concatenate([weighted_onehot(e, starts[e], 0) for e in range(NE)], axis=0)
        ystack = jnp.concatenate([y_ref[e, 0, pl.ds(starts[e], W), :] for e in range(NE)], axis=0)
        o_ref[0, tok, :] = x_ref[0, tok, :] + g2_ref[0] * _dot_tn(scatter, ystack)

        n_extra = [(jnp.maximum(ends[e] - starts[e] - W, 0) + W - 1) // W for e in range(NE)]

        @pl.when(functools.reduce(jnp.maximum, n_extra) > 0)
        def _(tok=tok, starts=starts, n_extra=n_extra, weighted_onehot=weighted_onehot):
            for e in range(NE):
                def extra(c, carry, e=e):
                    lo = starts[e] + W + c * W
                    start = _window_start(lo, cap, W)
                    part = _dot_tn(weighted_onehot(e, start, lo), y_ref[e, 0, pl.ds(start, W), :])
                    o_ref[0, tok, :] = o_ref[0, tok, :] + g2_ref[0] * part
                    return carry

                lax.fori_loop(0, n_extra[e], extra, 0)


def _combine(y, pos, gate, offs, x1, gate2):
    B, S, D = x1.shape
    NE, _, cap, _ = y.shape
    ts = MOE_TOKEN_TILE * MOE_STEP_TILES
    row = lambda b, j, o: (b, j, 0)
    grid_spec = pltpu.PrefetchScalarGridSpec(
        num_scalar_prefetch=1,
        grid=(B, S // ts),
        in_specs=[pl.BlockSpec((NE, 1, cap, D), lambda b, j, o: (0, b, 0, 0)),
                  pl.BlockSpec((1, NE, ts), lambda b, j, o: (b, 0, j)),
                  pl.BlockSpec((1, NE, ts), lambda b, j, o: (b, 0, j)),
                  pl.BlockSpec((1, ts, D), row),
                  pl.BlockSpec((1, 1, D), lambda b, j, o: (b, 0, 0))],
        out_specs=pl.BlockSpec((1, ts, D), row),
    )
    return pl.pallas_call(
        _combine_kernel,
        grid_spec=grid_spec,
        out_shape=jax.ShapeDtypeStruct((B, S, D), F32),
        compiler_params=_params(("parallel", "arbitrary"), V7X_VMEM_LIMIT),
        name="combine",
    )(offs.reshape(-1), y, pos, gate, x1, gate2)


def _inproj_weight(w_in):
    D = w_in.shape[0]
    H = MLSTM_HEADS
    g0 = 2 * MLSTM_WIDTH
    gates = w_in[:, g0:g0 + N_GATES].reshape(D, 4, H).transpose(0, 2, 1).reshape(D, N_GATES)
    main = jnp.concatenate([w_in[:, :g0], w_in[:, g0 + N_GATES:], gates,
                            jnp.zeros((D, V7X_LANES - N_GATES), w_in.dtype)], axis=1)
    return main.astype(BF16)


def kernel(x, c, w_ada, b_ada, norm1_g, w_in, conv_w, conv_b, w_q_blk, w_k_blk, w_v_blk, b_igate, b_fgate,
           mlstm_norm_g, mlstm_skip, q_norm_g, k_norm_g, rel_bias, w_out, norm2_g, w_router, b_router,
           w_gate, w_up, w_down):
    B, S, D = x.shape
    depth = w_ada.shape[0]
    diag = _attn_bias_diagonals(rel_bias)
    for l in range(depth):
        mod = _ada(c, w_ada[l], b_ada[l])
        shift1, scale1, gate1, shift2, scale2, gate2 = (
            mod[:, i * D:(i + 1) * D].reshape(B, 1, D) for i in range(N_MOD))

        x_m, o_pre, a_q, a_k, a_v, gates = _inproj(x, scale1, shift1, norm1_g[l].reshape(1, D),
                                                   _inproj_weight(w_in[l]), q_norm_g[l], k_norm_g[l])
        y_m = _mlstm(x_m, o_pre, gates, conv_w[l], conv_b[l], w_q_blk[l], w_k_blk[l], w_v_blk[l],
                     b_igate[l], b_fgate[l], mlstm_norm_g[l], mlstm_skip[l])
        y_a = _attn(a_q, a_k, a_v, diag)
        x1 = _outproj(y_m, y_a, x, gate1, w_out[l])

        h2, pos, aff, offs = _select(x1, scale2, shift2, norm2_g[l].reshape(1, D), w_router[l], b_router[l])
        xin = _gather(h2, pos, offs)
        NE, _, cap, _ = xin.shape
        y = _experts(xin.reshape(NE, B * cap, D), w_gate[l], w_up[l], w_down[l]).reshape(NE, B, cap, D)
        x = _combine(y, pos, aff, offs, x1, gate2)
    return x
```
